```python
import math
import jax
import jax.numpy as jnp
from jax import lax
import numpy as np

D_MODEL = 2048
BATCH = 8
SEQ = 2048
DEPTH = 1
DEC_BATCH = 32
DEC_SEQ = 8
PAST_LEN = 8192
PAGE_SIZE = 128

D_HEAD = 128
N_HEADS_A = 8
N_IDX_HEADS = 16
D_IDX = 64
TOPK_MAX = 256
N_HEADS_B = 8
ROPE_THETA = 10000.0
N_GROUPS = 4
EXPERTS_PER_GROUP = 8
N_EXPERTS = N_GROUPS * EXPERTS_PER_GROUP
TOP_K_EXPERTS = 2
D_EXPERT = 512
EXPERT_BLOCK = 128
Q_BLOCK = 128
NORM_EPS = 1e-6
A_WIDTH = N_HEADS_A * D_HEAD
B_WIDTH = N_HEADS_B * D_HEAD
IN_SPLITS = (A_WIDTH, D_HEAD, D_HEAD, N_IDX_HEADS * D_IDX, D_IDX, N_IDX_HEADS,
             B_WIDTH, B_WIDTH, B_WIDTH, D_MODEL, D_MODEL)
IN_COLS = sum(IN_SPLITS)

kernel_name = "hybrid_dsa_stickbreak_hmoe_step"


def rmsnorm(x, g):
    xf = x.astype(jnp.float32)
    inv = lax.rsqrt(jnp.mean(xf * xf, axis=-1, keepdims=True) + NORM_EPS)
    return (xf * inv * g.astype(jnp.float32)).astype(x.dtype)


def rope(x, pos):
    d = x.shape[-1]
    half = d // 2
    inv_freq = jnp.exp(jnp.arange(half, dtype=jnp.float32) * (-2.0 * math.log(ROPE_THETA) / d))
    ang = pos.astype(jnp.float32)[:, None] * inv_freq[None, :]
    ang = ang.reshape((1, pos.shape[0]) + (1,) * (x.ndim - 3) + (half,))
    cos, sin = jnp.cos(ang), jnp.sin(ang)
    x1 = x[..., :half].astype(jnp.float32)
    x2 = x[..., half:].astype(jnp.float32)
    return jnp.concatenate([x1 * cos - x2 * sin, x2 * cos + x1 * sin], axis=-1).astype(x.dtype)


def take_rows(a, idx):
    return jax.vmap(lambda ab, ib: ab[ib])(a, idx)


def physical_rows(page_table, idx):
    return jax.vmap(lambda pt, ib: pt[ib // PAGE_SIZE] * PAGE_SIZE + ib % PAGE_SIZE)(page_table, idx)


def gather_pages(pool, page_table):
    g = pool[page_table]
    return g.reshape((g.shape[0], g.shape[1] * g.shape[2]) + g.shape[3:])


def map_query_blocks(fn, *xs):
    T = xs[0].shape[1]
    if T <= Q_BLOCK:
        return fn(*xs)
    nb = T // Q_BLOCK
    split = lambda a: jnp.moveaxis(a.reshape((a.shape[0], nb, Q_BLOCK) + a.shape[2:]), 1, 0)
    out = lax.map(lambda args: fn(*args), tuple(split(a) for a in xs))
    out = jnp.moveaxis(out, 0, 1)
    return out.reshape((out.shape[0], T) + out.shape[3:])


def dsa_attend(q, q_i, w_i, q_pos, k_i, k_pos, gather_kv, k_top):
    rel = jax.nn.relu(jnp.einsum('bthd,bsd->bths', q_i, k_i).astype(jnp.float32) * (D_IDX ** -0.5))
    score = jnp.einsum('bths,bth->bts', rel, w_i.astype(jnp.float32) * (N_IDX_HEADS ** -0.5))
    causal = k_pos[None, None, :] <= q_pos[..., None]
    score = jnp.where(causal, score, -jnp.inf)
    _, idx = lax.top_k(score, k_top)
    valid = k_pos[idx] <= q_pos[..., None]
    k_sel, v_sel = gather_kv(idx)
    s = jnp.einsum('bthd,btkd->bthk', q, k_sel).astype(jnp.float32) * (D_HEAD ** -0.5)
    s = jnp.where(valid[:, :, None, :], s, -jnp.inf)
    p = jax.nn.softmax(s, axis=-1).astype(v_sel.dtype)
    return jnp.einsum('bthk,btkd->bthd', p, v_sel)


def stick_breaking(q, k, v, q_pos, k_pos):
    z = jnp.einsum('bthd,bshd->bhts', q, k).astype(jnp.float32) * (D_HEAD ** -0.5)
    before = k_pos[None, :] < q_pos[0][:, None]
    log_keep = jnp.where(before, jax.nn.log_sigmoid(-z), 0.0)
    rev = lax.cumsum(log_keep, axis=3, reverse=True)
    between = jnp.concatenate([rev[..., 1:], jnp.zeros_like(rev[..., :1])], axis=-1)
    a = jnp.where(before, jnp.exp(jax.nn.log_sigmoid(z) + between), 0.0)
    return jnp.einsum('bhts,bshd->bthd', a.astype(v.dtype), v)


def mixer_inputs(x, pos, g_mix, w_in):
    B, T, _ = x.shape
    points = np.cumsum(IN_SPLITS)[:-1].tolist()
    (q_a, k_a, v_a, q_i, k_i, w_i, q_b, k_b, v_b, g_a, g_b) = jnp.split(rmsnorm(x, g_mix) @ w_in, points, axis=-1)
    q_a = rope(q_a.reshape(B, T, N_HEADS_A, D_HEAD), pos)
    k_a = rope(k_a, pos)
    q_i = rope(q_i.reshape(B, T, N_IDX_HEADS, D_IDX), pos)
    k_i = rope(k_i, pos)
    heads_b = lambda a: a.reshape(B, T, N_HEADS_B, D_HEAD)
    return (q_a, k_a, v_a, q_i, k_i, w_i, heads_b(q_b), heads_b(k_b), heads_b(v_b), g_a, g_b)


def prompt_mixers(x, g_mix, w_in):
    S = x.shape[1]
    pos = jnp.arange(S)
    k_top = min(TOPK_MAX, S // 4)
    (q_a, k_a, v_a, q_i, k_i, w_i, q_b, k_b, v_b, g_a, g_b) = mixer_inputs(x, pos, g_mix, w_in)
    gather_kv = lambda idx: (take_rows(k_a, idx), take_rows(v_a, idx))
    o_a = map_query_blocks(lambda q, qi, wi, qp: dsa_attend(q, qi, wi, qp, k_i, pos, gather_kv, k_top),
                           q_a, q_i, w_i, pos[None])
    o_b = map_query_blocks(lambda q, qp: stick_breaking(q, k_b, v_b, qp, pos), q_b, pos[None])
    return o_a, o_b, g_a, g_b, (k_a, v_a, k_i, k_b, v_b)


def sample_mixers(x, cache_a_k, cache_a_v, cache_idx_k, cache_b_k, cache_b_v, page_table, g_mix, w_in):
    T = x.shape[1]
    past = page_table.shape[1] * PAGE_SIZE
    pos = past + jnp.arange(T)
    key_pos = jnp.arange(past + T)
    k_top = min(TOPK_MAX, (past + T) // 4)
    (q_a, k_a, v_a, q_i, k_i, w_i, q_b, k_b, v_b, g_a, g_b) = mixer_inputs(x, pos, g_mix, w_in)
    k_i_all = jnp.concatenate([gather_pages(cache_idx_k, page_table), k_i], axis=1)
    flat_k = cache_a_k.reshape(-1, D_HEAD)
    flat_v = cache_a_v.reshape(-1, D_HEAD)

    def gather_kv(idx):
        in_past = (idx < past)[..., None]
        rows = physical_rows(page_table, jnp.minimum(idx, past - 1))
        new = jnp.clip(idx - past, 0, T - 1)
        return (jnp.where(in_past, flat_k[rows], take_rows(k_a, new)),
                jnp.where(in_past, flat_v[rows], take_rows(v_a, new)))

    o_a = map_query_blocks(lambda q, qi, wi, qp: dsa_attend(q, qi, wi, qp, k_i_all, key_pos, gather_kv, k_top),
                           q_a, q_i, w_i, pos[None])
    k_b_all = jnp.concatenate([gather_pages(cache_b_k, page_table), k_b], axis=1)
    v_b_all = jnp.concatenate([gather_pages(cache_b_v, page_table), v_b], axis=1)
    o_b = map_query_blocks(lambda q, qp: stick_breaking(q, k_b_all, v_b_all, qp, key_pos), q_b, pos[None])
    return o_a, o_b, g_a, g_b, (k_a, v_a, k_i, k_b, v_b)


def grouped_experts(xf, expert_id, gate, w_e_gate, w_e_up, w_e_down):
    n_tok, k = expert_id.shape
    nk = n_tok * k
    R = EXPERT_BLOCK
    flat_e = expert_id.reshape(-1)
    order = jnp.argsort(flat_e)
    se = flat_e[order]
    counts = jnp.bincount(flat_e, length=N_EXPERTS)
    padded = (counts + R - 1) // R * R
    pad_end = jnp.cumsum(padded)
    pad_start = pad_end - padded
    start = jnp.cumsum(counts) - counts
    dest = pad_start[se] + jnp.arange(nk) - start[se]
    nb = -(-(nk + N_EXPERTS * (R - 1)) // R)
    row_tok = jnp.full((nb * R,), n_tok, dtype=jnp.int32).at[dest].set((order // k).astype(jnp.int32))
    blk_e = jnp.minimum(jnp.searchsorted(pad_end, jnp.arange(nb) * R, side='right'), N_EXPERTS - 1)
    x_pad = jnp.concatenate([xf, jnp.zeros((1, xf.shape[1]), xf.dtype)], axis=0)

    def run(args):
        tok, e = args
        xb = x_pad[tok]
        return (jax.nn.silu(xb @ w_e_gate[e]) * (xb @ w_e_up[e])) @ w_e_down[e]

    out = lax.map(run, (row_tok.reshape(nb, R), blk_e)).reshape(nb * R, -1)
    dest_orig = jnp.zeros((nk,), jnp.int32).at[order].set(dest.astype(jnp.int32))
    y = out[dest_orig].reshape(n_tok, k, -1)
    return jnp.sum(y * gate[..., None], axis=1)


def hier_moe(x, w_rg, b_rg, w_re, b_re, w_e_gate, w_e_up, w_e_down):
    lead = x.shape[:-1]
    xf = x.reshape(-1, D_MODEL)
    n_tok = xf.shape[0]
    group_logits = (xf @ w_rg).astype(jnp.float32) + b_rg.astype(jnp.float32)
    group_prob = jax.nn.softmax(group_logits, axis=-1)
    _, g_sel = lax.top_k(group_logits, 1)
    p_group = jnp.take_along_axis(group_prob, g_sel, axis=1)
    exp_logits = ((xf @ w_re).astype(jnp.float32) + b_re.astype(jnp.float32)).reshape(n_tok, N_GROUPS, EXPERTS_PER_GROUP)
    in_group = exp_logits[jnp.arange(n_tok), g_sel[:, 0]]
    top_logits, local = lax.top_k(in_group, TOP_K_EXPERTS)
    gate = p_group * jax.nn.softmax(top_logits, axis=-1)
    expert_id = g_sel * EXPERTS_PER_GROUP + local
    y = grouped_experts(xf, expert_id, gate.astype(xf.dtype), w_e_gate, w_e_up, w_e_down)
    return y.reshape(lead + (D_MODEL,))


def merge_and_channel_mix(x, o_a, o_b, g_a, g_b, w_up_a, w_up_b, w_out,
                          g_ffn, w_rg, b_rg, w_re, b_re, w_e_gate, w_e_up, w_e_down):
    B, T, _ = x.shape
    merged = (jax.nn.sigmoid(g_a) * (o_a.reshape(B, T, A_WIDTH) @ w_up_a)
              + jax.nn.sigmoid(g_b) * (o_b.reshape(B, T, B_WIDTH) @ w_up_b))
    h = x + merged @ w_out
    return h + hier_moe(rmsnorm(h, g_ffn), w_rg, b_rg, w_re, b_re, w_e_gate, w_e_up, w_e_down)


def setup_inputs(seed: int = 0) -> dict:
    key = jax.random.key(seed)
    ks = jax.random.split(key, 22)
    n_pages = PAST_LEN // PAGE_SIZE
    n_used = DEC_BATCH * n_pages
    n_phys = n_used + max(1, n_used // 4)
    nrm = lambda k, shape, scale=1.0: jax.random.normal(k, shape, jnp.float32) * scale
    gain = lambda k: 1.0 + 0.02 * jax.random.normal(k, (DEPTH, D_MODEL), jnp.float32)
    page_table = jax.random.permutation(ks[7], n_phys)[:n_used].reshape(DEC_BATCH, n_pages).astype(jnp.int32)
    return {
        "x_prompt": nrm(ks[0], (BATCH, SEQ, D_MODEL)),
        "x_sample": nrm(ks[1], (DEC_BATCH, DEC_SEQ, D_MODEL)),
        "cache_a_k": nrm(ks[2], (DEPTH, n_phys, PAGE_SIZE, D_HEAD)),
        "cache_a_v": nrm(ks[3], (DEPTH, n_phys, PAGE_SIZE, D_HEAD)),
        "cache_idx_k": nrm(ks[4], (DEPTH, n_phys, PAGE_SIZE, D_IDX)),
        "cache_b_k": nrm(ks[5], (DEPTH, n_phys, PAGE_SIZE, N_HEADS_B, D_HEAD)),
        "cache_b_v": nrm(ks[6], (DEPTH, n_phys, PAGE_SIZE, N_HEADS_B, D_HEAD)),
        "page_table": page_table,
        "g_mix": gain(ks[8]),
        "w_in": nrm(ks[9], (DEPTH, D_MODEL, IN_COLS), D_MODEL ** -0.5),
        "w_up_a": nrm(ks[10], (DEPTH, A_WIDTH, D_MODEL), A_WIDTH ** -0.5),
        "w_up_b": nrm(ks[11], (DEPTH, B_WIDTH, D_MODEL), B_WIDTH ** -0.5),
        "w_out": nrm(ks[12], (DEPTH, D_MODEL, D_MODEL), D_MODEL ** -0.5),
        "g_ffn": gain(ks[13]),
        "w_rg": nrm(ks[14], (DEPTH, D_MODEL, N_GROUPS), D_MODEL ** -0.5),
        "b_rg": nrm(ks[15], (DEPTH, N_GROUPS), 0.01),
        "w_re": nrm(ks[16], (DEPTH, D_MODEL, N_EXPERTS), D_MODEL ** -0.5),
        "b_re": nrm(ks[17], (DEPTH, N_EXPERTS), 0.01),
        "w_e_gate": nrm(ks[18], (DEPTH, N_EXPERTS, D_MODEL, D_EXPERT), D_MODEL ** -0.5),
        "w_e_up": nrm(ks[19], (DEPTH, N_EXPERTS, D_MODEL, D_EXPERT), D_MODEL ** -0.5),
        "w_e_down": nrm(ks[20], (DEPTH, N_EXPERTS, D_EXPERT, D_MODEL), D_EXPERT ** -0.5),
        "g_final": 1.0 + 0.02 * jax.random.normal(ks[21], (D_MODEL,), jnp.float32),
    }


def reference(x_prompt, x_sample, cache_a_k, cache_a_v, cache_idx_k, cache_b_k, cache_b_v, page_table,
              g_mix, w_in, w_up_a, w_up_b, w_out, g_ffn, w_rg, b_rg, w_re, b_re,
              w_e_gate, w_e_up, w_e_down, g_final):
    hp, hs = x_prompt, x_sample
    rows_p, rows_s = [], []
    for l in range(DEPTH):
        chan_w = (w_up_a[l], w_up_b[l], w_out[l], g_ffn[l], w_rg[l], b_rg[l], w_re[l], b_re[l],
                  w_e_gate[l], w_e_up[l], w_e_down[l])
        o_a, o_b, g_a, g_b, new_p = prompt_mixers(hp, g_mix[l], w_in[l])
        hp = merge_and_channel_mix(hp, o_a, o_b, g_a, g_b, *chan_w)
        o_a, o_b, g_a, g_b, new_s = sample_mixers(hs, cache_a_k[l], cache_a_v[l], cache_idx_k[l],
                                                  cache_b_k[l], cache_b_v[l], page_table, g_mix[l], w_in[l])
        hs = merge_and_channel_mix(hs, o_a, o_b, g_a, g_b, *chan_w)
        rows_p.append(new_p)
        rows_s.append(new_s)
    stack = lambda rows, i: jnp.stack([r[i] for r in rows])
    y_prompt = rmsnorm(hp, g_final)
    y_sample = rmsnorm(hs, g_final)
    return (y_prompt, y_sample,
            stack(rows_p, 0), stack(rows_p, 1), stack(rows_p, 2), stack(rows_p, 3), stack(rows_p, 4),
            stack(rows_s, 0), stack(rows_s, 1), stack(rows_s, 2), stack(rows_s, 3), stack(rows_s, 4))
```

```python
import functools
import math

import jax
import jax.numpy as jnp
from jax import lax
from jax.experimental import pallas as pl
from jax.experimental.pallas import tpu as pltpu

D_MODEL = 2048
D_HEAD = 128
N_HEADS_A = 8
N_IDX_HEADS = 16
D_IDX = 64
TOPK_MAX = 256
N_HEADS_B = 8
ROPE_THETA = 10000.0
N_GROUPS = 4
EXPERTS_PER_GROUP = 8
N_EXPERTS = N_GROUPS * EXPERTS_PER_GROUP
TOP_K_EXPERTS = 2
D_EXPERT = 512
PAGE_SIZE = 128
NORM_EPS = 1e-6
A_WIDTH = N_HEADS_A * D_HEAD
B_WIDTH = N_HEADS_B * D_HEAD
IDX_WIDTH = N_IDX_HEADS * D_IDX

LANES = 128
VMEM_LIMIT = 56 * 1024 * 1024
EXPERT_ROWS = 256
ROUTER_LANES = 128

F32 = jnp.float32
BF16 = jnp.bfloat16
NEG_INF = float("-inf")
INT_MIN = -2 ** 31


def _dot(a, b):
    return jnp.dot(a, b, preferred_element_type=F32)


def _dot_nt(a, b):
    return lax.dot_general(a, b, (((1,), (1,)), ((), ())), preferred_element_type=F32)


def _params(semantics):
    return pltpu.CompilerParams(dimension_semantics=semantics, vmem_limit_bytes=VMEM_LIMIT)


def _rope_tables(pos, d):
    half = d // 2
    inv_freq = jnp.exp(jnp.arange(half, dtype=F32) * (-2.0 * math.log(ROPE_THETA) / d))
    ang = pos.astype(F32)[:, None] * inv_freq[None, :]
    cos, sin = jnp.cos(ang), jnp.sin(ang)
    reps = LANES // d
    cos_t = jnp.tile(jnp.concatenate([cos, cos], axis=-1), (1, reps))
    sin_t = jnp.tile(jnp.concatenate([-sin, sin], axis=-1), (1, reps))
    return cos_t, sin_t


def _swap_halves(blk, d):
    if d == LANES:
        return pltpu.roll(blk, LANES // 2, 1)
    lane = lax.broadcasted_iota(jnp.int32, blk.shape, 1)
    first_half = (lane % d) < (d // 2)
    return jnp.where(first_half, pltpu.roll(blk, LANES - d // 2, 1), pltpu.roll(blk, d // 2, 1))


def _rmsnorm_rows(x, g):
    inv = lax.rsqrt(jnp.mean(x * x, axis=-1, keepdims=True) + NORM_EPS)
    return x * inv * g


def _inproj_rope_kernel(x_ref, g_ref, w_ref, cos_ref, sin_ref, cosi_ref, sini_ref,
                        qa_ref, ka_ref, va_ref, qi_ref, kiwi_ref):
    xn = _rmsnorm_rows(x_ref[...], g_ref[...]).astype(BF16)
    y = _dot(xn, w_ref[...])
    cos, sin = cos_ref[...], sin_ref[...]
    for h in range(N_HEADS_A):
        blk = y[:, h * D_HEAD:(h + 1) * D_HEAD]
        qa_ref[:, h * D_HEAD:(h + 1) * D_HEAD] = (blk * cos + _swap_halves(blk, D_HEAD) * sin).astype(BF16)
    ka = y[:, A_WIDTH:A_WIDTH + D_HEAD]
    ka_ref[...] = ka * cos + _swap_halves(ka, D_HEAD) * sin
    va_ref[...] = y[:, A_WIDTH + D_HEAD:A_WIDTH + 2 * D_HEAD]
    cosi, sini = cosi_ref[...], sini_ref[...]
    base = A_WIDTH + 2 * D_HEAD
    for j in range(IDX_WIDTH // LANES):
        blk = y[:, base + j * LANES:base + (j + 1) * LANES]
        qi_ref[:, j * LANES:(j + 1) * LANES] = (blk * cosi + _swap_halves(blk, D_IDX) * sini).astype(BF16)
    blk = y[:, base + IDX_WIDTH:base + IDX_WIDTH + LANES]
    lane = lax.broadcasted_iota(jnp.int32, blk.shape, 1)
    is_k = lane < D_IDX
    kiwi_ref[...] = blk * jnp.where(is_k, cosi, 1.0) + _swap_halves(blk, D_IDX) * jnp.where(is_k, sini, 0.0)


def _inproj_rope(x, g, w_a, tabs, tm, n_pos_blocks):
    rows = x.shape[0]
    wa_cols = w_a.shape[1]
    cos, sin, cosi, sini = tabs
    row_spec = lambda width: pl.BlockSpec((tm, width), lambda i: (i, 0))
    tab_spec = pl.BlockSpec((tm, LANES), lambda i: (i % n_pos_blocks, 0))
    return pl.pallas_call(
        _inproj_rope_kernel,
        grid=(rows // tm,),
        in_specs=[row_spec(D_MODEL),
                  pl.BlockSpec((1, D_MODEL), lambda i: (0, 0)),
                  pl.BlockSpec((D_MODEL, wa_cols), lambda i: (0, 0)),
                  tab_spec, tab_spec, tab_spec, tab_spec],
        out_specs=[row_spec(A_WIDTH), row_spec(D_HEAD), row_spec(D_HEAD), row_spec(IDX_WIDTH), row_spec(LANES)],
        out_shape=[jax.ShapeDtypeStruct((rows, A_WIDTH), BF16),
                   jax.ShapeDtypeStruct((rows, D_HEAD), F32),
                   jax.ShapeDtypeStruct((rows, D_HEAD), F32),
                   jax.ShapeDtypeStruct((rows, IDX_WIDTH), BF16),
                   jax.ShapeDtypeStruct((rows, LANES), F32)],
        compiler_params=_params(("arbitrary",)),
        name="inproj_rope",
    )(x, g, w_a, cos, sin, cosi, sini)


def _inproj_plain_kernel(x_ref, g_ref, w_ref, qb_ref, kb_ref, vb_ref, sig_ref, xn_ref):
    j = pl.program_id(1)

    @pl.when(j == 0)
    def _():
        xn_ref[...] = _rmsnorm_rows(x_ref[...], g_ref[...]).astype(BF16)

    y = _dot(xn_ref[...], w_ref[...])

    @pl.when(j == 0)
    def _():
        qb_ref[...] = y.astype(BF16)

    @pl.when(j == 1)
    def _():
        kb_ref[...] = y

    @pl.when(j == 2)
    def _():
        vb_ref[...] = y

    @pl.when(j >= 3)
    def _():
        sig_ref[...] = (1.0 / (1.0 + jnp.exp(-y))).astype(BF16)


def _inproj_plain(x, g, w_bg, tm):
    rows = x.shape[0]
    tn = B_WIDTH
    n_col = w_bg.shape[1] // tn
    fixed = pl.BlockSpec((tm, tn), lambda i, j: (i, 0))
    return pl.pallas_call(
        _inproj_plain_kernel,
        grid=(rows // tm, n_col),
        in_specs=[pl.BlockSpec((tm, D_MODEL), lambda i, j: (i, 0)),
                  pl.BlockSpec((1, D_MODEL), lambda i, j: (0, 0)),
                  pl.BlockSpec((D_MODEL, tn), lambda i, j: (0, j))],
        out_specs=[fixed, fixed, fixed,
                   pl.BlockSpec((tm, tn), lambda i, j: (i, jnp.maximum(j - 3, 0)))],
        out_shape=[jax.ShapeDtypeStruct((rows, tn), BF16),
                   jax.ShapeDtypeStruct((rows, tn), F32),
                   jax.ShapeDtypeStruct((rows, tn), F32),
                   jax.ShapeDtypeStruct((rows, (n_col - 3) * tn), BF16)],
        scratch_shapes=[pltpu.VMEM((tm, D_MODEL), BF16)],
        compiler_params=_params(("arbitrary", "arbitrary")),
        name="inproj_plain",
    )(x, g, w_bg)


def _sortable(x):
    b = lax.bitcast_convert_type(x, jnp.int32)
    return b ^ ((b >> 31) & jnp.int32(0x7FFFFFFF))


def _select_topk(keys_ref, neg_ref, valid_fn, k_top):
    rows, width = keys_ref.shape
    n_chunks = width // LANES
    ones = jnp.ones((LANES, LANES), BF16)
    lane = lax.broadcasted_iota(jnp.int32, (rows, LANES), 1)

    def lane_total(acc):
        return _dot(acc.astype(BF16), ones)

    def count(pred):
        acc = jnp.zeros((rows, LANES), F32)
        for c in range(n_chunks):
            acc = acc + jnp.where(pred(keys_ref[:, c * LANES:(c + 1) * LANES], c), 1.0, 0.0)
        return lane_total(acc)

    def value_step(i, ans):
        cand_u = ans | lax.shift_left(jnp.int32(1), 31 - i)
        cand = cand_u ^ jnp.int32(INT_MIN)
        cnt = count(lambda k, c: k >= cand)
        return jnp.where(cnt >= k_top, cand_u, ans)

    thr = lax.fori_loop(0, 32, value_step, jnp.zeros((rows, LANES), jnp.int32)) ^ jnp.int32(INT_MIN)
    n_gt = count(lambda k, c: k > thr)
    n_ge = count(lambda k, c: k >= thr)
    need = k_top - n_gt

    n_bits = (width - 1).bit_length()

    def index_step(i, cst):
        cand = cst | lax.shift_left(jnp.int32(1), n_bits - 1 - i)
        cnt = count(lambda k, c: (k == thr) & (lane + c * LANES < cand))
        return jnp.where(cnt < need, cand, cst)

    cut = lax.cond(jnp.max(n_ge) > k_top,
                   lambda: lax.fori_loop(0, n_bits, index_step, jnp.zeros((rows, LANES), jnp.int32)),
                   lambda: jnp.full((rows, LANES), width, jnp.int32))

    for c in range(n_chunks):
        k = keys_ref[:, c * LANES:(c + 1) * LANES]
        tie_keep = jnp.where(lane + c * LANES <= cut, 0.0, NEG_INF)
        keep = jnp.where(k > thr, 0.0, jnp.where(k == thr, tie_keep, NEG_INF))
        neg_ref[:, c * LANES:(c + 1) * LANES] = jnp.where(valid_fn(c), keep, NEG_INF)


def _dsa_prompt_kernel(qi_ref, wq_ref, kk_ref, qa_ref, ka_ref, va_ref, o_ref,
                       ki16, ka16, va16, keys_ref, neg_ref, *, k_top):
    qb = pl.program_id(1)
    tq, seq = keys_ref.shape

    @pl.when(qb == 0)
    def _():
        ki16[...] = kk_ref[:, :D_IDX].astype(BF16)
        ka16[...] = ka_ref[...].astype(BF16)
        va16[...] = va_ref[...].astype(BF16)

    w = wq_ref[:, D_IDX:D_IDX + N_IDX_HEADS] * (N_IDX_HEADS ** -0.5)
    k_idx = ki16[...]
    score = jnp.zeros((tq, seq), F32)
    for h in range(N_IDX_HEADS):
        rel = jnp.maximum(_dot_nt(qi_ref[:, h * D_IDX:(h + 1) * D_IDX], k_idx) * (D_IDX ** -0.5), 0.0)
        score = score + rel * w[:, h:h + 1]

    t_pos = qb * tq + lax.broadcasted_iota(jnp.int32, (tq, LANES), 0)
    lane = lax.broadcasted_iota(jnp.int32, (tq, LANES), 1)
    causal = lambda c: lane + c * LANES <= t_pos
    s_pos = lax.broadcasted_iota(jnp.int32, (tq, seq), 1)
    q_pos = qb * tq + lax.broadcasted_iota(jnp.int32, (tq, seq), 0)
    keys_ref[...] = _sortable(jnp.where(s_pos <= q_pos, score, NEG_INF))
    _select_topk(keys_ref, neg_ref, causal, k_top)

    k_att, v_att = ka16[...], va16[...]
    for h in range(N_HEADS_A):
        s = _dot_nt(qa_ref[:, h * D_HEAD:(h + 1) * D_HEAD], k_att) * (D_HEAD ** -0.5) + neg_ref[...]
        m = jnp.max(s, axis=-1, keepdims=True)
        e = jnp.exp(s - m)
        l = jnp.sum(e, axis=-1, keepdims=True)
        o = _dot(e.astype(BF16), v_att) / l
        o_ref[:, h * D_HEAD:(h + 1) * D_HEAD] = o.astype(BF16)


def _dsa_prompt(q_i, kiwi, q_a, k_a, v_a, batch, seq, tq):
    rows = batch * seq
    nq = seq // tq
    k_top = min(TOPK_MAX, seq // 4)
    q_spec = lambda width: pl.BlockSpec((tq, width), lambda b, q: (b * nq + q, 0))
    kv_spec = lambda width: pl.BlockSpec((seq, width), lambda b, q: (b, 0))
    return pl.pallas_call(
        functools.partial(_dsa_prompt_kernel, k_top=k_top),
        grid=(batch, nq),
        in_specs=[q_spec(IDX_WIDTH), q_spec(LANES), kv_spec(LANES),
                  q_spec(A_WIDTH), kv_spec(D_HEAD), kv_spec(D_HEAD)],
        out_specs=q_spec(A_WIDTH),
        out_shape=jax.ShapeDtypeStruct((rows, A_WIDTH), BF16),
        scratch_shapes=[pltpu.VMEM((seq, D_IDX), BF16), pltpu.VMEM((seq, D_HEAD), BF16),
                        pltpu.VMEM((seq, D_HEAD), BF16), pltpu.VMEM((tq, seq), jnp.int32),
                        pltpu.VMEM((tq, seq), F32)],
        compiler_params=_params(("arbitrary", "arbitrary")),
        name="dsa_prompt",
    )(q_i, kiwi, kiwi, q_a, k_a, v_a)


def _suffix_matrix(n, reps=1):
    j = lax.broadcasted_iota(jnp.int32, (n * reps, n * reps), 0)
    s = lax.broadcasted_iota(jnp.int32, (n * reps, n * reps), 1)
    same = (j % reps) == (s % reps)
    later = jnp.where(same & (j // reps > s // reps), 1.0, 0.0)
    total = jnp.where(same, 1.0, 0.0)
    return jnp.concatenate([later, total], axis=1).astype(BF16)


def _stick_terms(z, before):
    soft = jnp.log1p(jnp.exp(-jnp.abs(z)))
    log_keep = jnp.where(before, -(jnp.maximum(z, 0.0) + soft), 0.0)
    log_beta = jnp.minimum(z, 0.0) - soft
    return log_keep, log_beta


def _sb_prompt_kernel(q_ref, k_ref, v_ref, u_ref, o_ref, k16, v16, *, tk):
    qb = pl.program_id(2)
    tq = q_ref.shape[0]

    @pl.when(qb == 0)
    def _():
        k16[...] = k_ref[...].astype(BF16)
        v16[...] = v_ref[...].astype(BF16)

    q = q_ref[...]
    u = u_ref[...]
    t_pos = qb * tq + lax.broadcasted_iota(jnp.int32, (tq, tk), 0)
    lane = lax.broadcasted_iota(jnp.int32, (tq, tk), 1)
    n_blocks = (qb + 1) * (tq // tk)

    def body(it, carry):
        acc, later = carry
        kb = n_blocks - 1 - it
        start = pl.multiple_of(kb * tk, tk)
        z = _dot_nt(q, k16[pl.ds(start, tk), :]) * (D_HEAD ** -0.5)
        before = lane + kb * tk < t_pos
        log_keep, log_beta = _stick_terms(z, before)
        cum = _dot(log_keep.astype(BF16), u)
        a = jnp.where(before, jnp.exp(log_beta + cum[:, :tk] + later), 0.0)
        acc = acc + _dot(a.astype(BF16), v16[pl.ds(start, tk), :])
        return acc, later + cum[:, tk:]

    acc, _ = lax.fori_loop(0, n_blocks, body,
                           (jnp.zeros((tq, D_HEAD), F32), jnp.zeros((tq, tk), F32)))
    o_ref[...] = acc.astype(BF16)


def _sb_prompt(q_b, k_b, v_b, batch, seq, tq):
    rows = batch * seq
    nq = seq // tq
    tk = LANES
    q_spec = pl.BlockSpec((tq, D_HEAD), lambda b, h, q: (b * nq + q, h))
    kv_spec = pl.BlockSpec((seq, D_HEAD), lambda b, h, q: (b, h))
    return pl.pallas_call(
        functools.partial(_sb_prompt_kernel, tk=tk),
        grid=(batch, N_HEADS_B, nq),
        in_specs=[q_spec, kv_spec, kv_spec, pl.BlockSpec((tk, 2 * tk), lambda b, h, q: (0, 0))],
        out_specs=q_spec,
        out_shape=jax.ShapeDtypeStruct((rows, B_WIDTH), BF16),
        scratch_shapes=[pltpu.VMEM((seq, D_HEAD), BF16), pltpu.VMEM((seq, D_HEAD), BF16)],
        compiler_params=_params(("arbitrary", "arbitrary", "arbitrary")),
        name="sb_prompt",
    )(q_b, k_b, v_b, _suffix_matrix(tk))


def _page_copies(pt_ref, batch_idx, first_page, n_pages, pools, bufs, slot, sem, rows_per_page, n_logical):
    copies = []
    for p in range(n_pages):
        phys = pt_ref[batch_idx * n_logical + first_page + p]
        for pool, buf in zip(pools, bufs):
            copies.append(pltpu.make_async_copy(
                pool.at[phys], buf.at[slot, pl.ds(p * rows_per_page, rows_per_page)], sem.at[slot]))
    return copies


def _dsa_sample_kernel(pt_ref, qi_ref, wi_ref, kin_ref, qa_ref, kan_ref, van_ref,
                       idx_pool, k_pool, v_pool, o_ref,
                       idx_buf, k_buf, v_buf, keys_ref, neg_ref, sem, *, k_top, n_pages, n_new):
    b = pl.program_id(0)
    nb = pl.num_programs(0)
    past = n_pages * PAGE_SIZE
    slot = b % 2
    copies = lambda bb, ss: _page_copies(pt_ref, bb, 0, n_pages, (idx_pool, k_pool, v_pool),
                                         (idx_buf, k_buf, v_buf), ss, sem, PAGE_SIZE, n_pages)

    @pl.when(b == 0)
    def _():
        for c in copies(0, 0):
            c.start()

    @pl.when(b + 1 < nb)
    def _():
        for c in copies(b + 1, 1 - slot):
            c.start()

    for c in copies(b, slot):
        c.wait()

    k_idx = idx_buf[slot].astype(BF16)
    k_idx_new = kin_ref[0].astype(BF16)
    w = wi_ref[0] * (N_IDX_HEADS ** -0.5)
    n_q = w.shape[0]
    score = jnp.zeros((n_q, past), F32)
    score_new = jnp.zeros((n_q, LANES), F32)
    for h in range(N_IDX_HEADS):
        qh = qi_ref[0, h]
        wh = w[:, h:h + 1]
        score = score + jnp.maximum(_dot_nt(qh, k_idx) * (D_IDX ** -0.5), 0.0) * wh
        score_new = score_new + jnp.maximum(_dot_nt(qh, k_idx_new) * (D_IDX ** -0.5), 0.0) * wh

    t_idx = lax.broadcasted_iota(jnp.int32, (n_q, LANES), 0)
    lane = lax.broadcasted_iota(jnp.int32, (n_q, LANES), 1)
    new_ok = (lane <= t_idx) & (lane < n_new)
    keys_ref[:, :past] = _sortable(score)
    keys_ref[:, past:] = _sortable(jnp.where(new_ok, score_new, NEG_INF))
    n_past_chunks = past // LANES
    _select_topk(keys_ref, neg_ref, lambda c: new_ok if c >= n_past_chunks else (lane >= 0), k_top)

    k_att = k_buf[slot].astype(BF16)
    v_att = v_buf[slot].astype(BF16)
    q = qa_ref[0]
    scale = D_HEAD ** -0.5
    neg = jnp.concatenate([neg_ref[...]] * N_HEADS_A, axis=0)
    s_past = _dot_nt(q, k_att) * scale + neg[:, :past]
    s_new = _dot_nt(q, kan_ref[0].astype(BF16)) * scale + neg[:, past:]
    m = jnp.maximum(jnp.max(s_past, axis=-1, keepdims=True), jnp.max(s_new, axis=-1, keepdims=True))
    e_past = jnp.exp(s_past - m)
    e_new = jnp.exp(s_new - m)
    l = jnp.sum(e_past, axis=-1, keepdims=True) + jnp.sum(e_new, axis=-1, keepdims=True)
    o = _dot(e_past.astype(BF16), v_att) + _dot(e_new.astype(BF16), van_ref[0].astype(BF16))
    o_ref[0] = o / l


def _dsa_sample(page_table, q_i, w_i, k_i_new, q_a, k_a_new, v_a_new, idx_pool, k_pool, v_pool):
    batch, n_pages = page_table.shape
    n_q = w_i.shape[1]
    past = n_pages * PAGE_SIZE
    width = past + LANES
    k_top = min(TOPK_MAX, (past + n_q) // 4)
    blk = lambda shape: pl.BlockSpec((1,) + shape, lambda b, pt: (b,) + (0,) * len(shape))
    hbm = pl.BlockSpec(memory_space=pl.ANY)
    grid_spec = pltpu.PrefetchScalarGridSpec(
        num_scalar_prefetch=1,
        grid=(batch,),
        in_specs=[blk((N_IDX_HEADS, n_q, D_IDX)), blk((n_q, N_IDX_HEADS)), blk((LANES, D_IDX)),
                  blk((N_HEADS_A * n_q, D_HEAD)), blk((LANES, D_HEAD)), blk((LANES, D_HEAD)),
                  hbm, hbm, hbm],
        out_specs=blk((N_HEADS_A * n_q, D_HEAD)),
        scratch_shapes=[pltpu.VMEM((2, past, D_IDX), F32), pltpu.VMEM((2, past, D_HEAD), F32),
                        pltpu.VMEM((2, past, D_HEAD), F32), pltpu.VMEM((n_q, width), jnp.int32),
                        pltpu.VMEM((n_q, width), F32), pltpu.SemaphoreType.DMA((2,))],
    )
    return pl.pallas_call(
        functools.partial(_dsa_sample_kernel, k_top=k_top, n_pages=n_pages, n_new=n_q),
        grid_spec=grid_spec,
        out_shape=jax.ShapeDtypeStruct((batch, N_HEADS_A * n_q, D_HEAD), F32),
        compiler_params=_params(("arbitrary",)),
        name="dsa_sample",
    )(page_table.reshape(-1), q_i, w_i, k_i_new, q_a, k_a_new, v_a_new, idx_pool, k_pool, v_pool)


SB_CHUNK_PAGES = 8
SB_BLOCK_KEYS = 32


def _sb_sample_kernel(pt_ref, q_ref, kn_ref, vn_ref, u_ref, k_pool, v_pool, o_ref,
                      k_buf, v_buf, acc_ref, later_ref, sem, *, n_pages, n_new):
    b = pl.program_id(0)
    j = pl.program_id(1)
    n_chunks = pl.num_programs(1)
    step = b * n_chunks + j
    n_steps = pl.num_programs(0) * n_chunks
    slot = step % 2
    heads = N_HEADS_B
    rows_per_page = PAGE_SIZE * heads
    blk_lanes = SB_BLOCK_KEYS * heads

    def copies(st, ss):
        bb = st // n_chunks
        chunk = n_chunks - 1 - st % n_chunks
        return _page_copies(pt_ref, bb, chunk * SB_CHUNK_PAGES, SB_CHUNK_PAGES, (k_pool, v_pool),
                            (k_buf, v_buf), ss, sem, rows_per_page, n_pages)

    @pl.when(step == 0)
    def _():
        for c in copies(0, 0):
            c.start()

    @pl.when(step + 1 < n_steps)
    def _():
        for c in copies(step + 1, 1 - slot):
            c.start()

    q = q_ref[0]
    n_q = q.shape[0] // heads
    u = u_ref[...]
    scale = D_HEAD ** -0.5

    def head_diag(zt):
        lane_head = lax.broadcasted_iota(jnp.int32, (n_q, zt.shape[1]), 1) % heads
        out = jnp.zeros((n_q, zt.shape[1]), F32)
        for h in range(heads):
            out = out + jnp.where(lane_head == h, zt[h * n_q:(h + 1) * n_q, :], 0.0)
        return out

    def head_spread(a):
        lane_head = lax.broadcasted_iota(jnp.int32, a.shape, 1) % heads
        return jnp.concatenate([jnp.where(lane_head == h, a, 0.0) for h in range(heads)], axis=0).astype(BF16)

    def attend(k16, v16, before, later):
        n = k16.shape[0]
        z = head_diag(_dot_nt(q, k16) * scale)
        log_keep, log_beta = _stick_terms(z, before)
        pieces = []
        for blk in reversed(range(n // blk_lanes)):
            sl = slice(blk * blk_lanes, (blk + 1) * blk_lanes)
            cum = _dot(log_keep[:, sl].astype(BF16), u)
            pieces.append(log_beta[:, sl] + cum[:, :blk_lanes] + later)
            later = later + cum[:, blk_lanes:]
        a = jnp.exp(jnp.concatenate(pieces[::-1], axis=1))
        a = jnp.where(before, a, 0.0)
        return _dot(head_spread(a), v16), later

    @pl.when(j == 0)
    def _():
        t_idx = lax.broadcasted_iota(jnp.int32, (n_q, blk_lanes), 0)
        key = lax.broadcasted_iota(jnp.int32, (n_q, blk_lanes), 1) // heads
        before = (key < t_idx) & (key < n_new)
        out, later = attend(kn_ref[0].astype(BF16), vn_ref[0].astype(BF16), before,
                            jnp.zeros((n_q, blk_lanes), F32))
        acc_ref[...] = out
        later_ref[...] = later

    for c in copies(step, slot):
        c.wait()

    n_rows = SB_CHUNK_PAGES * rows_per_page
    out, later = attend(k_buf[slot].astype(BF16), v_buf[slot].astype(BF16),
                        jnp.full((n_q, n_rows), True), later_ref[...])
    acc_ref[...] += out
    later_ref[...] = later

    @pl.when(j == n_chunks - 1)
    def _():
        o_ref[0] = acc_ref[...]


def _sb_sample(page_table, q, k_new, v_new, k_pool, v_pool):
    batch, n_pages = page_table.shape
    rows_q = q.shape[1]
    n_q = rows_q // N_HEADS_B
    blk_lanes = SB_BLOCK_KEYS * N_HEADS_B
    n_chunks = n_pages // SB_CHUNK_PAGES
    chunk_rows = SB_CHUNK_PAGES * PAGE_SIZE * N_HEADS_B
    blk = lambda shape: pl.BlockSpec((1,) + shape, lambda b, j, pt: (b,) + (0,) * len(shape))
    hbm = pl.BlockSpec(memory_space=pl.ANY)
    grid_spec = pltpu.PrefetchScalarGridSpec(
        num_scalar_prefetch=1,
        grid=(batch, n_chunks),
        in_specs=[blk((rows_q, D_HEAD)), blk((blk_lanes, D_HEAD)), blk((blk_lanes, D_HEAD)),
                  pl.BlockSpec((blk_lanes, 2 * blk_lanes), lambda b, j, pt: (0, 0)), hbm, hbm],
        out_specs=blk((rows_q, D_HEAD)),
        scratch_shapes=[pltpu.VMEM((2, chunk_rows, D_HEAD), F32), pltpu.VMEM((2, chunk_rows, D_HEAD), F32),
                        pltpu.VMEM((rows_q, D_HEAD), F32), pltpu.VMEM((n_q, blk_lanes), F32),
                        pltpu.SemaphoreType.DMA((2,))],
    )
    return pl.pallas_call(
        functools.partial(_sb_sample_kernel, n_pages=n_pages, n_new=n_q),
        grid_spec=grid_spec,
        out_shape=jax.ShapeDtypeStruct((batch, rows_q, D_HEAD), F32),
        compiler_params=_params(("arbitrary", "arbitrary")),
        name="sb_sample",
    )(page_table.reshape(-1), q, k_new, v_new, _suffix_matrix(SB_BLOCK_KEYS, N_HEADS_B), k_pool, v_pool)


def _merge_kernel(x_ref, oa_ref, ob_ref, sig_ref, wua_ref, wub_ref, wo_ref, gf_ref, wr_hi_ref, wr_lo_ref, br_ref,
                  h_ref, hn_ref, lg_ref):
    sig_a = sig_ref[:, :D_MODEL].astype(F32)
    sig_b = sig_ref[:, D_MODEL:].astype(F32)
    merged = sig_a * _dot(oa_ref[...], wua_ref[...]) + sig_b * _dot(ob_ref[...], wub_ref[...])
    h = x_ref[...] + _dot(merged.astype(BF16), wo_ref[...])
    h_ref[...] = h
    hn = _rmsnorm_rows(h, gf_ref[...])
    hn_ref[...] = hn
    hn_hi = hn.astype(BF16)
    hn_lo = (hn - hn_hi.astype(F32)).astype(BF16)
    lg_ref[...] = (_dot(hn_hi, wr_hi_ref[...]) + _dot(hn_lo, wr_hi_ref[...])
                   + _dot(hn_hi, wr_lo_ref[...]) + br_ref[...])


def _merge(x, o_a, o_b, sig, w_up_a, w_up_b, w_out, g_ffn, wr_hi, wr_lo, b_r, tm):
    rows = x.shape[0]
    row_spec = lambda width: pl.BlockSpec((tm, width), lambda i: (i, 0))
    const = lambda shape: pl.BlockSpec(shape, lambda i: (0, 0), pipeline_mode=pl.Buffered(1))
    return pl.pallas_call(
        _merge_kernel,
        grid=(rows // tm,),
        in_specs=[row_spec(D_MODEL), row_spec(A_WIDTH), row_spec(B_WIDTH), row_spec(2 * D_MODEL),
                  const((A_WIDTH, D_MODEL)), const((B_WIDTH, D_MODEL)), const((D_MODEL, D_MODEL)),
                  const((1, D_MODEL)), const((D_MODEL, ROUTER_LANES)), const((D_MODEL, ROUTER_LANES)),
                  const((1, ROUTER_LANES))],
        out_specs=[row_spec(D_MODEL), row_spec(D_MODEL), row_spec(ROUTER_LANES)],
        out_shape=[jax.ShapeDtypeStruct((rows, D_MODEL), F32),
                   jax.ShapeDtypeStruct((rows, D_MODEL), F32),
                   jax.ShapeDtypeStruct((rows, ROUTER_LANES), F32)],
        compiler_params=_params(("arbitrary",)),
        name="merge",
    )(x, o_a, o_b, sig, w_up_a, w_up_b, w_out, g_ffn, wr_hi, wr_lo, b_r)


def _router_kernel(lg_ref, eid_ref, gate_ref):
    row = lambda r: lg_ref[r:r + 1, :]
    groups = [row(g) for g in range(N_GROUPS)]
    g_max = functools.reduce(jnp.maximum, groups)
    g_sel = jnp.full(g_max.shape, N_GROUPS - 1, jnp.int32)
    for g in reversed(range(N_GROUPS)):
        g_sel = jnp.where(groups[g] == g_max, g, g_sel)
    p_group = 1.0 / functools.reduce(lambda a, c: a + c, [jnp.exp(v - g_max) for v in groups])
    inside = []
    for e in range(EXPERTS_PER_GROUP):
        v = jnp.zeros_like(g_max)
        for g in range(N_GROUPS):
            v = jnp.where(g_sel == g, row(N_GROUPS + g * EXPERTS_PER_GROUP + e), v)
        inside.append(v)

    def top1(vals, skip):
        best = functools.reduce(jnp.maximum, [jnp.where(skip == e, NEG_INF, v) for e, v in enumerate(vals)])
        idx = jnp.full(best.shape, EXPERTS_PER_GROUP - 1, jnp.int32)
        for e in reversed(range(EXPERTS_PER_GROUP)):
            idx = jnp.where((vals[e] == best) & (skip != e), e, idx)
        return best, idx

    t1, i1 = top1(inside, jnp.full(g_max.shape, -1, jnp.int32))
    t2, i2 = top1(inside, i1)
    r = jnp.exp(t2 - t1)
    eid_ref[0:1, :] = g_sel * EXPERTS_PER_GROUP + i1
    eid_ref[1:2, :] = g_sel * EXPERTS_PER_GROUP + i2
    gate_ref[0:1, :] = p_group * (1.0 / (1.0 + r))
    gate_ref[1:2, :] = p_group * (r / (1.0 + r))


def _router(logits_t, tn):
    n_rows, n_tok = logits_t.shape
    return pl.pallas_call(
        _router_kernel,
        grid=(n_tok // tn,),
        in_specs=[pl.BlockSpec((n_rows, tn), lambda i: (0, i))],
        out_specs=[pl.BlockSpec((TOP_K_EXPERTS, tn), lambda i: (0, i)),
                   pl.BlockSpec((TOP_K_EXPERTS, tn), lambda i: (0, i))],
        out_shape=[jax.ShapeDtypeStruct((TOP_K_EXPERTS, n_tok), jnp.int32),
                   jax.ShapeDtypeStruct((TOP_K_EXPERTS, n_tok), F32)],
        compiler_params=_params(("arbitrary",)),
        name="router",
    )(logits_t)


def _row_copies(idx_ref, base, n, src, dst, sem):
    return [pltpu.make_async_copy(src.at[pl.ds(idx_ref[base + r], 1)], dst.at[pl.ds(r, 1)], sem) for r in range(n)]


def _experts_kernel(blk_e_ref, row_tok_ref, nact_ref, hn_hbm, wg_ref, wu_ref, wd_ref, out_ref, x_buf, sem):
    i = pl.program_id(0)
    n_act = nact_ref[0]
    rows = x_buf.shape[1]
    slot = i % 2

    def gather(blk, ss, do):
        def body(r, carry):
            tok = row_tok_ref[blk * rows + r]
            cp = pltpu.make_async_copy(hn_hbm.at[pl.ds(tok, 1)], x_buf.at[ss, pl.ds(r, 1)], sem.at[ss])
            do(cp)
            return carry
        lax.fori_loop(0, rows, body, 0)

    @pl.when(i == 0)
    def _():
        gather(0, 0, lambda cp: cp.start())

    @pl.when(i + 1 < n_act)
    def _():
        gather(i + 1, 1 - slot, lambda cp: cp.start())

    @pl.when(i < n_act)
    def _():
        gather(i, slot, lambda cp: cp.wait())
        xb = x_buf[slot].astype(BF16)
        g = _dot(xb, wg_ref[0])
        u = _dot(xb, wu_ref[0])
        act = (g * (1.0 / (1.0 + jnp.exp(-g))) * u).astype(BF16)
        out_ref[...] = _dot(act, wd_ref[0])

    @pl.when(i >= n_act)
    def _():
        out_ref[...] = jnp.zeros_like(out_ref)


def _experts(blk_e, row_tok, n_act, hn, w_gate, w_up, w_down):
    n_blocks = blk_e.shape[0]
    rows = EXPERT_ROWS
    grid_spec = pltpu.PrefetchScalarGridSpec(
        num_scalar_prefetch=3,
        grid=(n_blocks,),
        in_specs=[pl.BlockSpec(memory_space=pl.ANY),
                  pl.BlockSpec((1, D_MODEL, D_EXPERT), lambda i, be, rt, na: (be[i], 0, 0)),
                  pl.BlockSpec((1, D_MODEL, D_EXPERT), lambda i, be, rt, na: (be[i], 0, 0)),
                  pl.BlockSpec((1, D_EXPERT, D_MODEL), lambda i, be, rt, na: (be[i], 0, 0))],
        out_specs=pl.BlockSpec((rows, D_MODEL), lambda i, be, rt, na: (i, 0)),
        scratch_shapes=[pltpu.VMEM((2, rows, D_MODEL), F32), pltpu.SemaphoreType.DMA((2,))],
    )
    return pl.pallas_call(
        _experts_kernel,
        grid_spec=grid_spec,
        out_shape=jax.ShapeDtypeStruct((n_blocks * rows, D_MODEL), F32),
        compiler_params=_params(("arbitrary",)),
        name="experts",
    )(blk_e, row_tok, n_act, hn, w_gate, w_up, w_down)


def _combine_kernel(dest_ref, h_ref, gate_ref, gfin_ref, ys_hbm, y_ref, r_buf, sem, *, final_norm):
    i = pl.program_id(0)
    n = pl.num_programs(0)
    tt = h_ref.shape[0]
    slot = i % 2

    def gather(tile, ss, do):
        def body(r, carry):
            for k in range(TOP_K_EXPERTS):
                d = dest_ref[(tile * tt + r) * TOP_K_EXPERTS + k]
                do(pltpu.make_async_copy(ys_hbm.at[pl.ds(d, 1)], r_buf.at[ss, k, pl.ds(r, 1)], sem.at[ss]))
            return carry
        lax.fori_loop(0, tt, body, 0)

    @pl.when(i == 0)
    def _():
        gather(0, 0, lambda cp: cp.start())

    @pl.when(i + 1 < n)
    def _():
        gather(i + 1, 1 - slot, lambda cp: cp.start())

    gather(i, slot, lambda cp: cp.wait())
    y = h_ref[...]
    for k in range(TOP_K_EXPERTS):
        y = y + r_buf[slot, k] * gate_ref[:, k:k + 1]
    y_ref[...] = _rmsnorm_rows(y, gfin_ref[...]) if final_norm else y


def _combine(dest, h, gates, g_final, ys, tt, final_norm):
    n_tok = h.shape[0]
    grid_spec = pltpu.PrefetchScalarGridSpec(
        num_scalar_prefetch=1,
        grid=(n_tok // tt,),
        in_specs=[pl.BlockSpec((tt, D_MODEL), lambda i, d: (i, 0)),
                  pl.BlockSpec((tt, TOP_K_EXPERTS), lambda i, d: (i, 0)),
                  pl.BlockSpec((1, D_MODEL), lambda i, d: (0, 0)),
                  pl.BlockSpec(memory_space=pl.ANY)],
        out_specs=pl.BlockSpec((tt, D_MODEL), lambda i, d: (i, 0)),
        scratch_shapes=[pltpu.VMEM((2, TOP_K_EXPERTS, tt, D_MODEL), F32), pltpu.SemaphoreType.DMA((2,))],
    )
    return pl.pallas_call(
        functools.partial(_combine_kernel, final_norm=final_norm),
        grid_spec=grid_spec,
        out_shape=jax.ShapeDtypeStruct((n_tok, D_MODEL), F32),
        compiler_params=_params(("arbitrary",)),
        name="combine",
    )(dest, h, gates, g_final, ys)


def _moe(h, hn, logits, g_final, w_gate, w_up, w_down, tt, final_norm):
    n_tok = h.shape[0]
    n_rows = N_GROUPS + N_EXPERTS
    pad_rows = -(-n_rows // 8) * 8
    logits_t = jnp.pad(logits[:, :n_rows].T, ((0, pad_rows - n_rows), (0, 0)))
    eid, gates = _router(logits_t, min(n_tok, 2048))
    flat_e = eid.T.reshape(-1)
    nk = flat_e.shape[0]
    rows = EXPERT_ROWS
    order = jnp.argsort(flat_e, stable=True).astype(jnp.int32)
    counts = jnp.zeros((N_EXPERTS,), jnp.int32).at[flat_e].add(1)
    padded = (counts + rows - 1) // rows * rows
    pad_end = jnp.cumsum(padded)
    pad_start = pad_end - padded
    start = jnp.cumsum(counts) - counts
    se = flat_e[order]
    dest_sorted = pad_start[se] + jnp.arange(nk, dtype=jnp.int32) - start[se]
    dest = jnp.zeros((nk,), jnp.int32).at[order].set(dest_sorted.astype(jnp.int32))
    n_blocks = -(-(nk + N_EXPERTS * (rows - 1)) // rows)
    blk_pos = jnp.arange(n_blocks, dtype=jnp.int32) * rows
    blk_e = jnp.minimum(jnp.searchsorted(pad_end, blk_pos, side="right"), N_EXPERTS - 1).astype(jnp.int32)
    n_act = (pad_end[-1] // rows).astype(jnp.int32).reshape(1)
    pos = jnp.arange(n_blocks * rows, dtype=jnp.int32)
    e_of_pos = jnp.repeat(blk_e, rows)
    rank = pos - pad_start[e_of_pos]
    is_row = (rank < counts[e_of_pos]) & (pos < pad_end[-1])
    src = jnp.clip(start[e_of_pos] + rank, 0, nk - 1)
    row_tok = jnp.where(is_row, order[src] // TOP_K_EXPERTS, 0).astype(jnp.int32)
    ys = _experts(blk_e, row_tok, n_act, hn, w_gate, w_up, w_down)
    return _combine(dest, h, gates.T, g_final, ys, tt, final_norm)


def _split_router(w_rg, b_rg, w_re, b_re):
    w = jnp.concatenate([w_rg, w_re], axis=1)
    w = jnp.pad(w, ((0, 0), (0, ROUTER_LANES - w.shape[1])))
    b = jnp.pad(jnp.concatenate([b_rg, b_re]), (0, ROUTER_LANES - N_GROUPS - N_EXPERTS)).reshape(1, ROUTER_LANES)
    w_hi = w.astype(BF16)
    w_lo = (w - w_hi.astype(F32)).astype(BF16)
    return w_hi, w_lo, b


def _layer_weights(l, g_mix, w_in, w_up_a, w_up_b, w_out, g_ffn, w_rg, b_rg, w_re, b_re, w_e_gate, w_e_up, w_e_down):
    n_rope = A_WIDTH + 2 * D_HEAD + IDX_WIDTH + D_IDX + N_IDX_HEADS
    w = w_in[l]
    w_a = jnp.pad(w[:, :n_rope], ((0, 0), (0, -n_rope % LANES))).astype(BF16)
    w_bg = w[:, n_rope:].astype(BF16)
    return dict(
        g_mix=g_mix[l].reshape(1, D_MODEL), w_a=w_a, w_bg=w_bg,
        w_up_a=w_up_a[l].astype(BF16), w_up_b=w_up_b[l].astype(BF16), w_out=w_out[l].astype(BF16),
        g_ffn=g_ffn[l].reshape(1, D_MODEL), router=_split_router(w_rg[l], b_rg[l], w_re[l], b_re[l]),
        w_e_gate=w_e_gate[l].astype(BF16), w_e_up=w_e_up[l].astype(BF16), w_e_down=w_e_down[l].astype(BF16))


def _project(x2d, wts, pos, tm):
    n_pos_blocks = pos.shape[0] // tm
    tabs = _rope_tables(pos, D_HEAD) + _rope_tables(pos, D_IDX)
    q_a, k_a, v_a, q_i, kiwi = _inproj_rope(x2d, wts["g_mix"], wts["w_a"], tabs, tm, n_pos_blocks)
    q_b, k_b, v_b, sig = _inproj_plain(x2d, wts["g_mix"], wts["w_bg"], tm)
    return q_a, k_a, v_a, q_i, kiwi, q_b, k_b, v_b, sig


def _pad_rows(a, n):
    return jnp.pad(a, ((0, 0), (0, n - a.shape[1]), (0, 0)))


def kernel(x_prompt, x_sample, cache_a_k, cache_a_v, cache_idx_k, cache_b_k, cache_b_v, page_table,
           g_mix, w_in, w_up_a, w_up_b, w_out, g_ffn, w_rg, b_rg, w_re, b_re,
           w_e_gate, w_e_up, w_e_down, g_final):
    depth = w_in.shape[0]
    batch, seq, _ = x_prompt.shape
    dec_batch, dec_seq, _ = x_sample.shape
    n_pages = page_table.shape[1]
    past = n_pages * PAGE_SIZE
    n_phys = cache_a_k.shape[1]
    gfin = g_final.reshape(1, D_MODEL)
    hp = x_prompt.reshape(batch * seq, D_MODEL)
    hs = x_sample.reshape(dec_batch * dec_seq, D_MODEL)
    rows_s = dec_batch * dec_seq
    new_p, new_s = [], []
    for l in range(depth):
        wts = _layer_weights(l, g_mix, w_in, w_up_a, w_up_b, w_out, g_ffn, w_rg, b_rg, w_re, b_re,
                             w_e_gate, w_e_up, w_e_down)
        last = l == depth - 1

        q_a, k_a, v_a, q_i, kiwi, q_b, k_b, v_b, sig = _project(hp, wts, jnp.arange(seq), 512)
        o_a = _dsa_prompt(q_i, kiwi, q_a, k_a, v_a, batch, seq, 256)
        o_b = _sb_prompt(q_b, k_b, v_b, batch, seq, 256)
        h, hn, logits = _merge(hp, o_a, o_b, sig, wts["w_up_a"], wts["w_up_b"], wts["w_out"], wts["g_ffn"],
                               *wts["router"], 256)
        new_p.append((k_a.reshape(batch, seq, D_HEAD), v_a.reshape(batch, seq, D_HEAD),
                      kiwi[:, :D_IDX].reshape(batch, seq, D_IDX),
                      k_b.reshape(batch, seq, N_HEADS_B, D_HEAD), v_b.reshape(batch, seq, N_HEADS_B, D_HEAD)))
        hp = _moe(h, hn, logits, gfin, wts["w_e_gate"], wts["w_e_up"], wts["w_e_down"], 256, last)

        pos_s = past + jnp.arange(dec_seq)
        q_a, k_a, v_a, q_i, kiwi, q_b, k_b, v_b, sig = _project(hs, wts, jnp.tile(pos_s, dec_batch), rows_s)
        per_b = lambda a: a.reshape(dec_batch, dec_seq, -1)
        heads_first = lambda a, nh, d: per_b(a).reshape(dec_batch, dec_seq, nh, d).transpose(0, 2, 1, 3)
        qi_s = heads_first(q_i, N_IDX_HEADS, D_IDX)
        wi_s = per_b(kiwi)[:, :, D_IDX:D_IDX + N_IDX_HEADS]
        kin_s = _pad_rows(per_b(kiwi)[:, :, :D_IDX], LANES)
        qa_s = heads_first(q_a, N_HEADS_A, D_HEAD).reshape(dec_batch, N_HEADS_A * dec_seq, D_HEAD)
        o_a = _dsa_sample(page_table, qi_s, wi_s, kin_s, qa_s, _pad_rows(per_b(k_a), LANES),
                          _pad_rows(per_b(v_a), LANES), cache_idx_k[l], cache_a_k[l], cache_a_v[l])
        qb_s = heads_first(q_b, N_HEADS_B, D_HEAD).reshape(dec_batch, N_HEADS_B * dec_seq, D_HEAD)
        blk_lanes = SB_BLOCK_KEYS * N_HEADS_B
        kb_new = _pad_rows(k_b.reshape(dec_batch, dec_seq * N_HEADS_B, D_HEAD), blk_lanes)
        vb_new = _pad_rows(v_b.reshape(dec_batch, dec_seq * N_HEADS_B, D_HEAD), blk_lanes)
        pool_rows = lambda c: c.reshape(n_phys, PAGE_SIZE * N_HEADS_B, D_HEAD)
        o_b = _sb_sample(page_table, qb_s, kb_new, vb_new, pool_rows(cache_b_k[l]), pool_rows(cache_b_v[l]))
        tok_major = lambda o, nh: (o.reshape(dec_batch, nh, dec_seq, D_HEAD).transpose(0, 2, 1, 3)
                                   .reshape(rows_s, nh * D_HEAD).astype(BF16))
        o_a = tok_major(o_a, N_HEADS_A)
        o_b = tok_major(o_b, N_HEADS_B)
        h, hn, logits = _merge(hs, o_a, o_b, sig, wts["w_up_a"], wts["w_up_b"], wts["w_out"], wts["g_ffn"],
                               *wts["router"], rows_s)
        new_s.append((k_a.reshape(dec_batch, dec_seq, D_HEAD), v_a.reshape(dec_batch, dec_seq, D_HEAD),
                      kiwi[:, :D_IDX].reshape(dec_batch, dec_seq, D_IDX),
                      k_b.reshape(dec_batch, dec_seq, N_HEADS_B, D_HEAD),
                      v_b.reshape(dec_batch, dec_seq, N_HEADS_B, D_HEAD)))
        hs = _moe(h, hn, logits, gfin, wts["w_e_gate"], wts["w_e_up"], wts["w_e_down"], rows_s, last)

    stack = lambda rows_, i: jnp.stack([r[i] for r in rows_])
    return (hp.reshape(batch, seq, D_MODEL), hs.reshape(dec_batch, dec_seq, D_MODEL),
            stack(new_p, 0), stack(new_p, 1), stack(new_p, 2), stack(new_p, 3), stack(new_p, 4),
            stack(new_s, 0), stack(new_s, 1), stack(new_s, 2), stack(new_s, 3), stack(new_s, 4))
```

```python
import functools
import math

import jax
import jax.numpy as jnp
from jax import lax
from jax.experimental import pallas as pl
from jax.experimental.pallas import tpu as pltpu

D_MODEL = 2048
D_HEAD = 128
N_HEADS_A = 8
N_IDX_HEADS = 16
D_IDX = 64
TOPK_MAX = 256
N_HEADS_B = 8
ROPE_THETA = 10000.0
N_GROUPS = 4
EXPERTS_PER_GROUP = 8
N_EXPERTS = N_GROUPS * EXPERTS_PER_GROUP
TOP_K_EXPERTS = 2
D_EXPERT = 512
PAGE_SIZE = 128
NORM_EPS = 1e-6
A_WIDTH = N_HEADS_A * D_HEAD
B_WIDTH = N_HEADS_B * D_HEAD
IDX_WIDTH = N_IDX_HEADS * D_IDX

LANES = 128
VMEM_LIMIT = 56 * 1024 * 1024
EXPERT_ROWS = 256
ROUTER_LANES = 128

assert TOP_K_EXPERTS == 2

F32 = jnp.float32
BF16 = jnp.bfloat16
NEG_INF = float("-inf")
INT_MIN = -2 ** 31


def _dot(a, b):
    return jnp.dot(a, b, preferred_element_type=F32)


def _dot_nt(a, b):
    return lax.dot_general(a, b, (((1,), (1,)), ((), ())), preferred_element_type=F32)


def _params(semantics):
    return pltpu.CompilerParams(dimension_semantics=semantics, vmem_limit_bytes=VMEM_LIMIT)


def _rope_tables(pos, d):
    half = d // 2
    inv_freq = jnp.exp(jnp.arange(half, dtype=F32) * (-2.0 * math.log(ROPE_THETA) / d))
    ang = pos.astype(F32)[:, None] * inv_freq[None, :]
    cos, sin = jnp.cos(ang), jnp.sin(ang)
    reps = LANES // d
    cos_t = jnp.tile(jnp.concatenate([cos, cos], axis=-1), (1, reps))
    sin_t = jnp.tile(jnp.concatenate([-sin, sin], axis=-1), (1, reps))
    return cos_t, sin_t


def _swap_halves(blk, d):
    if d == LANES:
        return pltpu.roll(blk, LANES // 2, 1)
    lane = lax.broadcasted_iota(jnp.int32, blk.shape, 1)
    first_half = (lane % d) < (d // 2)
    return jnp.where(first_half, pltpu.roll(blk, LANES - d // 2, 1), pltpu.roll(blk, d // 2, 1))


def _rmsnorm_rows(x, g):
    inv = lax.rsqrt(jnp.mean(x * x, axis=-1, keepdims=True) + NORM_EPS)
    return x * inv * g


def _inproj_rope_kernel(x_ref, g_ref, w_ref, cos_ref, sin_ref, cosi_ref, sini_ref,
                        qa_ref, ka_ref, va_ref, qi_ref, kiwi_ref):
    xn = _rmsnorm_rows(x_ref[...], g_ref[...]).astype(BF16)
    y = _dot(xn, w_ref[...])
    cos, sin = cos_ref[...], sin_ref[...]
    for h in range(N_HEADS_A):
        blk = y[:, h * D_HEAD:(h + 1) * D_HEAD]
        qa_ref[:, h * D_HEAD:(h + 1) * D_HEAD] = (blk * cos + _swap_halves(blk, D_HEAD) * sin).astype(BF16)
    ka = y[:, A_WIDTH:A_WIDTH + D_HEAD]
    ka_ref[...] = ka * cos + _swap_halves(ka, D_HEAD) * sin
    va_ref[...] = y[:, A_WIDTH + D_HEAD:A_WIDTH + 2 * D_HEAD]
    cosi, sini = cosi_ref[...], sini_ref[...]
    base = A_WIDTH + 2 * D_HEAD
    for j in range(IDX_WIDTH // LANES):
        blk = y[:, base + j * LANES:base + (j + 1) * LANES]
        qi_ref[:, j * LANES:(j + 1) * LANES] = (blk * cosi + _swap_halves(blk, D_IDX) * sini).astype(BF16)
    blk = y[:, base + IDX_WIDTH:base + IDX_WIDTH + LANES]
    lane = lax.broadcasted_iota(jnp.int32, blk.shape, 1)
    is_k = lane < D_IDX
    kiwi_ref[...] = blk * jnp.where(is_k, cosi, 1.0) + _swap_halves(blk, D_IDX) * jnp.where(is_k, sini, 0.0)


def _inproj_rope(x, g, w_a, tabs, tm, n_pos_blocks):
    rows = x.shape[0]
    wa_cols = w_a.shape[1]
    cos, sin, cosi, sini = tabs
    row_spec = lambda width: pl.BlockSpec((tm, width), lambda i: (i, 0))
    tab_spec = pl.BlockSpec((tm, LANES), lambda i: (i % n_pos_blocks, 0))
    return pl.pallas_call(
        _inproj_rope_kernel,
        grid=(rows // tm,),
        in_specs=[row_spec(D_MODEL),
                  pl.BlockSpec((1, D_MODEL), lambda i: (0, 0)),
                  pl.BlockSpec((D_MODEL, wa_cols), lambda i: (0, 0)),
                  tab_spec, tab_spec, tab_spec, tab_spec],
        out_specs=[row_spec(A_WIDTH), row_spec(D_HEAD), row_spec(D_HEAD), row_spec(IDX_WIDTH), row_spec(LANES)],
        out_shape=[jax.ShapeDtypeStruct((rows, A_WIDTH), BF16),
                   jax.ShapeDtypeStruct((rows, D_HEAD), F32),
                   jax.ShapeDtypeStruct((rows, D_HEAD), F32),
                   jax.ShapeDtypeStruct((rows, IDX_WIDTH), BF16),
                   jax.ShapeDtypeStruct((rows, LANES), F32)],
        compiler_params=_params(("arbitrary",)),
        name="inproj_rope",
    )(x, g, w_a, cos, sin, cosi, sini)


def _inproj_plain_kernel(x_ref, g_ref, w_ref, qb_ref, kb_ref, vb_ref, sig_ref, xn_ref):
    j = pl.program_id(1)

    @pl.when(j == 0)
    def _():
        xn_ref[...] = _rmsnorm_rows(x_ref[...], g_ref[...]).astype(BF16)

    y = _dot(xn_ref[...], w_ref[...])

    @pl.when(j == 0)
    def _():
        qb_ref[...] = y.astype(BF16)

    @pl.when(j == 1)
    def _():
        kb_ref[...] = y

    @pl.when(j == 2)
    def _():
        vb_ref[...] = y

    @pl.when(j >= 3)
    def _():
        sig_ref[...] = (1.0 / (1.0 + jnp.exp(-y))).astype(BF16)


def _inproj_plain(x, g, w_bg, tm):
    rows = x.shape[0]
    tn = B_WIDTH
    n_col = w_bg.shape[1] // tn
    fixed = pl.BlockSpec((tm, tn), lambda i, j: (i, 0))
    return pl.pallas_call(
        _inproj_plain_kernel,
        grid=(rows // tm, n_col),
        in_specs=[pl.BlockSpec((tm, D_MODEL), lambda i, j: (i, 0)),
                  pl.BlockSpec((1, D_MODEL), lambda i, j: (0, 0)),
                  pl.BlockSpec((D_MODEL, tn), lambda i, j: (0, j))],
        out_specs=[fixed, fixed, fixed,
                   pl.BlockSpec((tm, tn), lambda i, j: (i, jnp.maximum(j - 3, 0)))],
        out_shape=[jax.ShapeDtypeStruct((rows, tn), BF16),
                   jax.ShapeDtypeStruct((rows, tn), F32),
                   jax.ShapeDtypeStruct((rows, tn), F32),
                   jax.ShapeDtypeStruct((rows, (n_col - 3) * tn), BF16)],
        scratch_shapes=[pltpu.VMEM((tm, D_MODEL), BF16)],
        compiler_params=_params(("arbitrary", "arbitrary")),
        name="inproj_plain",
    )(x, g, w_bg)


def _sortable(x):
    b = lax.bitcast_convert_type(x, jnp.int32)
    return b ^ ((b >> 31) & jnp.int32(0x7FFFFFFF))


def _select_topk(keys_ref, neg_ref, valid_fn, k_top):
    rows, width = keys_ref.shape
    n_chunks = width // LANES
    ones = jnp.ones((LANES, LANES), BF16)
    lane = lax.broadcasted_iota(jnp.int32, (rows, LANES), 1)

    def lane_total(acc):
        return _dot(acc.astype(BF16), ones)

    def count(pred):
        acc = jnp.zeros((rows, LANES), F32)
        for c in range(n_chunks):
            acc = acc + jnp.where(pred(keys_ref[:, c * LANES:(c + 1) * LANES], c), 1.0, 0.0)
        return lane_total(acc)

    def value_step(i, ans):
        cand_u = ans | lax.shift_left(jnp.int32(1), 31 - i)
        cand = cand_u ^ jnp.int32(INT_MIN)
        cnt = count(lambda k, c: k >= cand)
        return jnp.where(cnt >= k_top, cand_u, ans)

    thr = lax.fori_loop(0, 32, value_step, jnp.zeros((rows, LANES), jnp.int32)) ^ jnp.int32(INT_MIN)
    n_gt = count(lambda k, c: k > thr)
    n_ge = count(lambda k, c: k >= thr)
    need = k_top - n_gt

    n_bits = (width - 1).bit_length()

    def index_step(i, cst):
        cand = cst | lax.shift_left(jnp.int32(1), n_bits - 1 - i)
        cnt = count(lambda k, c: (k == thr) & (lane + c * LANES < cand))
        return jnp.where(cnt < need, cand, cst)

    cut = lax.cond(jnp.max(n_ge) > k_top,
                   lambda: lax.fori_loop(0, n_bits, index_step, jnp.zeros((rows, LANES), jnp.int32)),
                   lambda: jnp.full((rows, LANES), width, jnp.int32))

    for c in range(n_chunks):
        k = keys_ref[:, c * LANES:(c + 1) * LANES]
        tie_keep = jnp.where(lane + c * LANES <= cut, 0.0, NEG_INF)
        keep = jnp.where(k > thr, 0.0, jnp.where(k == thr, tie_keep, NEG_INF))
        neg_ref[:, c * LANES:(c + 1) * LANES] = jnp.where(valid_fn(c), keep, NEG_INF)


def _dsa_prompt_kernel(qi_ref, wq_ref, kk_ref, qa_ref, ka_ref, va_ref, *rest, k_top, q_start):
    _, o_ref, keys_ref, neg_ref = rest
    tq, width = keys_ref.shape

    w = wq_ref[0, :, D_IDX:D_IDX + N_IDX_HEADS] * ((N_IDX_HEADS * D_IDX) ** -0.5)
    k_idx = kk_ref[0, :, :D_IDX].astype(BF16)
    score = jnp.zeros((tq, width), F32)
    for h in range(N_IDX_HEADS):
        rel = jnp.maximum(_dot_nt(qi_ref[0, :, h * D_IDX:(h + 1) * D_IDX], k_idx), 0.0)
        score = score + rel * w[:, h:h + 1]

    t_pos = q_start + lax.broadcasted_iota(jnp.int32, (tq, LANES), 0)
    lane = lax.broadcasted_iota(jnp.int32, (tq, LANES), 1)
    causal = lambda c: lane + c * LANES <= t_pos
    s_pos = lax.broadcasted_iota(jnp.int32, (tq, width), 1)
    q_pos = q_start + lax.broadcasted_iota(jnp.int32, (tq, width), 0)
    keys_ref[...] = _sortable(jnp.where(s_pos <= q_pos, score, NEG_INF))
    _select_topk(keys_ref, neg_ref, causal, k_top)

    k_att = ka_ref[0].astype(BF16)
    v_att = va_ref[0].astype(BF16)
    exp2_scale = (D_HEAD ** -0.5) * math.log2(math.e)
    for h in range(N_HEADS_A):
        s = _dot_nt(qa_ref[0, :, h * D_HEAD:(h + 1) * D_HEAD], k_att) + neg_ref[...]
        m = jnp.max(s, axis=-1, keepdims=True)
        e = jnp.exp2((s - m) * exp2_scale)
        l = jnp.sum(e, axis=-1, keepdims=True)
        o = _dot(e.astype(BF16), v_att) / l
        o_ref[0, :, h * D_HEAD:(h + 1) * D_HEAD] = o.astype(BF16)


def _dsa_prompt(q_i, kiwi, q_a, k_a, v_a, batch, seq, tq):
    k_top = min(TOPK_MAX, seq // 4)
    out = jnp.zeros((batch, seq, A_WIDTH), BF16)
    for qb in range(seq // tq):
        width = (qb + 1) * tq
        q_spec = lambda w_: pl.BlockSpec((1, tq, w_), lambda b: (b, qb, 0))
        kv_spec = lambda w_: pl.BlockSpec((1, width, w_), lambda b: (b, 0, 0))
        out = pl.pallas_call(
            functools.partial(_dsa_prompt_kernel, k_top=k_top, q_start=qb * tq),
            grid=(batch,),
            in_specs=[q_spec(IDX_WIDTH), q_spec(LANES), kv_spec(LANES), q_spec(A_WIDTH), kv_spec(D_HEAD),
                      kv_spec(D_HEAD), pl.BlockSpec(memory_space=pl.ANY)],
            out_specs=q_spec(A_WIDTH),
            out_shape=jax.ShapeDtypeStruct((batch, seq, A_WIDTH), BF16),
            scratch_shapes=[pltpu.VMEM((tq, width), jnp.int32), pltpu.VMEM((tq, width), F32)],
            input_output_aliases={6: 0},
            compiler_params=_params(("arbitrary",)),
            name=f"dsa_prompt_q{qb}",
        )(q_i, kiwi, kiwi, q_a, k_a, v_a, out)
    return out


def _suffix_matrix(n, reps=1):
    j = lax.broadcasted_iota(jnp.int32, (n * reps, n * reps), 0)
    s = lax.broadcasted_iota(jnp.int32, (n * reps, n * reps), 1)
    same = (j % reps) == (s % reps)
    later = jnp.where(same & (j // reps > s // reps), 1.0, 0.0)
    total = jnp.where(same, 1.0, 0.0)
    return jnp.concatenate([later, total], axis=1).astype(BF16)


def _softplus(z):
    return jnp.maximum(z, 0.0) + jnp.log(1.0 + jnp.exp(-jnp.abs(z)))


def _sb_prompt_kernel(q_ref, k_ref, v_ref, u_ref, o_ref, k16, v16, acc_ref, later_ref):
    qb = pl.program_id(2)
    tq = q_ref.shape[0]
    sub = LANES
    n_sub = tq // sub

    @pl.when(qb == 0)
    def _():
        k16[...] = k_ref[...].astype(BF16)
        v16[...] = v_ref[...].astype(BF16)

    q = q_ref[...]
    u = u_ref[...]

    def block(kb, diagonal):
        start = pl.multiple_of(kb * tq, tq)
        z = _dot_nt(q, k16[pl.ds(start, tq), :]) * (D_HEAD ** -0.5)
        sp = _softplus(z)
        log_beta = z - sp
        if diagonal:
            before = (lax.broadcasted_iota(jnp.int32, (tq, tq), 1)
                      < lax.broadcasted_iota(jnp.int32, (tq, tq), 0))
            sp = jnp.where(before, sp, 0.0)
        sp16 = sp.astype(BF16)
        later = later_ref[...]
        args = [None] * n_sub
        for j in reversed(range(n_sub)):
            cum = _dot(sp16[:, j * sub:(j + 1) * sub], u)
            args[j] = log_beta[:, j * sub:(j + 1) * sub] - cum[:, :sub] - later
            later = later + cum[:, sub:]
        a = jnp.exp(jnp.concatenate(args, axis=1))
        if diagonal:
            a = jnp.where(before, a, 0.0)
        acc_ref[...] += _dot(a.astype(BF16), v16[pl.ds(start, tq), :])
        later_ref[...] = later

    acc_ref[...] = jnp.zeros_like(acc_ref)
    later_ref[...] = jnp.zeros_like(later_ref)
    block(qb, True)

    def body(it, carry):
        block(qb - 1 - it, False)
        return carry

    lax.fori_loop(0, qb, body, 0)
    o_ref[...] = acc_ref[...].astype(BF16)


def _sb_prompt(q_b, k_b, v_b, batch, seq, tq):
    rows = batch * seq
    nq = seq // tq
    q_spec = pl.BlockSpec((tq, D_HEAD), lambda b, h, q: (b * nq + q, h))
    kv_spec = pl.BlockSpec((seq, D_HEAD), lambda b, h, q: (b, h))
    return pl.pallas_call(
        _sb_prompt_kernel,
        grid=(batch, N_HEADS_B, nq),
        in_specs=[q_spec, kv_spec, kv_spec, pl.BlockSpec((LANES, 2 * LANES), lambda b, h, q: (0, 0))],
        out_specs=q_spec,
        out_shape=jax.ShapeDtypeStruct((rows, B_WIDTH), BF16),
        scratch_shapes=[pltpu.VMEM((seq, D_HEAD), BF16), pltpu.VMEM((seq, D_HEAD), BF16),
                        pltpu.VMEM((tq, D_HEAD), F32), pltpu.VMEM((tq, LANES), F32)],
        compiler_params=_params(("arbitrary", "arbitrary", "arbitrary")),
        name="sb_prompt",
    )(q_b, k_b, v_b, _suffix_matrix(LANES))


def _page_copies(pt_ref, batch_idx, first_page, n_pages, pools, bufs, slot, sem, rows_per_page, n_logical):
    copies = []
    for p in range(n_pages):
        phys = pt_ref[batch_idx * n_logical + first_page + p]
        for pool, buf in zip(pools, bufs):
            copies.append(pltpu.make_async_copy(
                pool.at[phys], buf.at[slot, pl.ds(p * rows_per_page, rows_per_page)], sem.at[slot]))
    return copies


def _dsa_sample_kernel(pt_ref, qi_ref, wi_ref, kin_ref, qa_ref, kan_ref, van_ref,
                       idx_pool, k_pool, v_pool, o_ref,
                       idx_buf, k_buf, v_buf, keys_ref, neg_ref, sem, *, k_top, n_pages, n_new):
    b = pl.program_id(0)
    nb = pl.num_programs(0)
    past = n_pages * PAGE_SIZE
    slot = b % 2
    copies = lambda bb, ss: _page_copies(pt_ref, bb, 0, n_pages, (idx_pool, k_pool, v_pool),
                                         (idx_buf, k_buf, v_buf), ss, sem, PAGE_SIZE, n_pages)

    @pl.when(b == 0)
    def _():
        for c in copies(0, 0):
            c.start()

    @pl.when(b + 1 < nb)
    def _():
        for c in copies(b + 1, 1 - slot):
            c.start()

    for c in copies(b, slot):
        c.wait()

    k_idx = idx_buf[slot].astype(BF16)
    k_idx_new = kin_ref[0].astype(BF16)
    w = wi_ref[0] * (N_IDX_HEADS ** -0.5)
    n_q = w.shape[0]
    score = jnp.zeros((n_q, past), F32)
    score_new = jnp.zeros((n_q, LANES), F32)
    for h in range(N_IDX_HEADS):
        qh = qi_ref[0, h]
        wh = w[:, h:h + 1]
        score = score + jnp.maximum(_dot_nt(qh, k_idx) * (D_IDX ** -0.5), 0.0) * wh
        score_new = score_new + jnp.maximum(_dot_nt(qh, k_idx_new) * (D_IDX ** -0.5), 0.0) * wh

    t_idx = lax.broadcasted_iota(jnp.int32, (n_q, LANES), 0)
    lane = lax.broadcasted_iota(jnp.int32, (n_q, LANES), 1)
    new_ok = (lane <= t_idx) & (lane < n_new)
    keys_ref[:, :past] = _sortable(score)
    keys_ref[:, past:] = _sortable(jnp.where(new_ok, score_new, NEG_INF))
    n_past_chunks = past // LANES
    _select_topk(keys_ref, neg_ref, lambda c: new_ok if c >= n_past_chunks else (lane >= 0), k_top)

    k_att = k_buf[slot].astype(BF16)
    v_att = v_buf[slot].astype(BF16)
    q = qa_ref[0]
    scale = D_HEAD ** -0.5
    neg = jnp.concatenate([neg_ref[...]] * N_HEADS_A, axis=0)
    s_past = _dot_nt(q, k_att) * scale + neg[:, :past]
    s_new = _dot_nt(q, kan_ref[0].astype(BF16)) * scale + neg[:, past:]
    m = jnp.maximum(jnp.max(s_past, axis=-1, keepdims=True), jnp.max(s_new, axis=-1, keepdims=True))
    e_past = jnp.exp(s_past - m)
    e_new = jnp.exp(s_new - m)
    l = jnp.sum(e_past, axis=-1, keepdims=True) + jnp.sum(e_new, axis=-1, keepdims=True)
    o = _dot(e_past.astype(BF16), v_att) + _dot(e_new.astype(BF16), van_ref[0].astype(BF16))
    o_ref[0] = o / l


def _dsa_sample(page_table, q_i, w_i, k_i_new, q_a, k_a_new, v_a_new, idx_pool, k_pool, v_pool):
    batch, n_pages = page_table.shape
    n_q = w_i.shape[1]
    past = n_pages * PAGE_SIZE
    width = past + LANES
    k_top = min(TOPK_MAX, (past + n_q) // 4)
    blk = lambda shape: pl.BlockSpec((1,) + shape, lambda b, pt: (b,) + (0,) * len(shape))
    hbm = pl.BlockSpec(memory_space=pl.ANY)
    grid_spec = pltpu.PrefetchScalarGridSpec(
        num_scalar_prefetch=1,
        grid=(batch,),
        in_specs=[blk((N_IDX_HEADS, n_q, D_IDX)), blk((n_q, N_IDX_HEADS)), blk((LANES, D_IDX)),
                  blk((N_HEADS_A * n_q, D_HEAD)), blk((LANES, D_HEAD)), blk((LANES, D_HEAD)),
                  hbm, hbm, hbm],
        out_specs=blk((N_HEADS_A * n_q, D_HEAD)),
        scratch_shapes=[pltpu.VMEM((2, past, D_IDX), F32), pltpu.VMEM((2, past, D_HEAD), F32),
                        pltpu.VMEM((2, past, D_HEAD), F32), pltpu.VMEM((n_q, width), jnp.int32),
                        pltpu.VMEM((n_q, width), F32), pltpu.SemaphoreType.DMA((2,))],
    )
    return pl.pallas_call(
        functools.partial(_dsa_sample_kernel, k_top=k_top, n_pages=n_pages, n_new=n_q),
        grid_spec=grid_spec,
        out_shape=jax.ShapeDtypeStruct((batch, N_HEADS_A * n_q, D_HEAD), F32),
        compiler_params=_params(("arbitrary",)),
        name="dsa_sample",
    )(page_table.reshape(-1), q_i, w_i, k_i_new, q_a, k_a_new, v_a_new, idx_pool, k_pool, v_pool)


SB_CHUNK_PAGES = 8
SB_BLOCK_KEYS = 32


def _sb_sample_kernel(pt_ref, q_ref, kn_ref, vn_ref, u_ref, k_pool, v_pool, o_ref,
                      k_buf, v_buf, acc_ref, later_ref, sem, *, n_pages, n_new):
    b = pl.program_id(0)
    j = pl.program_id(1)
    n_chunks = pl.num_programs(1)
    step = b * n_chunks + j
    n_steps = pl.num_programs(0) * n_chunks
    slot = step % 2
    heads = N_HEADS_B
    rows_per_page = PAGE_SIZE * heads
    blk_lanes = SB_BLOCK_KEYS * heads

    def copies(st, ss):
        bb = st // n_chunks
        chunk = n_chunks - 1 - st % n_chunks
        return _page_copies(pt_ref, bb, chunk * SB_CHUNK_PAGES, SB_CHUNK_PAGES, (k_pool, v_pool),
                            (k_buf, v_buf), ss, sem, rows_per_page, n_pages)

    @pl.when(step == 0)
    def _():
        for c in copies(0, 0):
            c.start()

    @pl.when(step + 1 < n_steps)
    def _():
        for c in copies(step + 1, 1 - slot):
            c.start()

    q = q_ref[0]
    n_q = q.shape[0] // heads
    u = u_ref[...]
    scale = D_HEAD ** -0.5

    def head_diag(zt):
        lane_head = lax.broadcasted_iota(jnp.int32, (n_q, zt.shape[1]), 1) % heads
        out = jnp.zeros((n_q, zt.shape[1]), F32)
        for h in range(heads):
            out = out + jnp.where(lane_head == h, zt[h * n_q:(h + 1) * n_q, :], 0.0)
        return out

    def head_spread(a):
        lane_head = lax.broadcasted_iota(jnp.int32, a.shape, 1) % heads
        return jnp.concatenate([jnp.where(lane_head == h, a, 0.0) for h in range(heads)], axis=0).astype(BF16)

    def attend(k16, v16, before, later):
        n = k16.shape[0]
        z = head_diag(_dot_nt(q, k16) * scale)
        sp = _softplus(z)
        log_beta = z - sp
        sp16 = jnp.where(before, sp, 0.0).astype(BF16)
        pieces = []
        for blk in reversed(range(n // blk_lanes)):
            sl = slice(blk * blk_lanes, (blk + 1) * blk_lanes)
            cum = _dot(sp16[:, sl], u)
            pieces.append(log_beta[:, sl] - cum[:, :blk_lanes] - later)
            later = later + cum[:, blk_lanes:]
        a = jnp.exp(jnp.concatenate(pieces[::-1], axis=1))
        a = jnp.where(before, a, 0.0)
        return _dot(head_spread(a), v16), later

    @pl.when(j == 0)
    def _():
        t_idx = lax.broadcasted_iota(jnp.int32, (n_q, blk_lanes), 0)
        key = lax.broadcasted_iota(jnp.int32, (n_q, blk_lanes), 1) // heads
        before = (key < t_idx) & (key < n_new)
        out, later = attend(kn_ref[0].astype(BF16), vn_ref[0].astype(BF16), before,
                            jnp.zeros((n_q, blk_lanes), F32))
        acc_ref[...] = out
        later_ref[...] = later

    for c in copies(step, slot):
        c.wait()

    n_rows = SB_CHUNK_PAGES * rows_per_page
    out, later = attend(k_buf[slot].astype(BF16), v_buf[slot].astype(BF16),
                        jnp.full((n_q, n_rows), True), later_ref[...])
    acc_ref[...] += out
    later_ref[...] = later

    @pl.when(j == n_chunks - 1)
    def _():
        o_ref[0] = acc_ref[...]


def _sb_sample(page_table, q, k_new, v_new, k_pool, v_pool):
    batch, n_pages = page_table.shape
    rows_q = q.shape[1]
    n_q = rows_q // N_HEADS_B
    blk_lanes = SB_BLOCK_KEYS * N_HEADS_B
    n_chunks = n_pages // SB_CHUNK_PAGES
    chunk_rows = SB_CHUNK_PAGES * PAGE_SIZE * N_HEADS_B
    blk = lambda shape: pl.BlockSpec((1,) + shape, lambda b, j, pt: (b,) + (0,) * len(shape))
    hbm = pl.BlockSpec(memory_space=pl.ANY)
    grid_spec = pltpu.PrefetchScalarGridSpec(
        num_scalar_prefetch=1,
        grid=(batch, n_chunks),
        in_specs=[blk((rows_q, D_HEAD)), blk((blk_lanes, D_HEAD)), blk((blk_lanes, D_HEAD)),
                  pl.BlockSpec((blk_lanes, 2 * blk_lanes), lambda b, j, pt: (0, 0)), hbm, hbm],
        out_specs=blk((rows_q, D_HEAD)),
        scratch_shapes=[pltpu.VMEM((2, chunk_rows, D_HEAD), F32), pltpu.VMEM((2, chunk_rows, D_HEAD), F32),
                        pltpu.VMEM((rows_q, D_HEAD), F32), pltpu.VMEM((n_q, blk_lanes), F32),
                        pltpu.SemaphoreType.DMA((2,))],
    )
    return pl.pallas_call(
        functools.partial(_sb_sample_kernel, n_pages=n_pages, n_new=n_q),
        grid_spec=grid_spec,
        out_shape=jax.ShapeDtypeStruct((batch, rows_q, D_HEAD), F32),
        compiler_params=_params(("arbitrary", "arbitrary")),
        name="sb_sample",
    )(page_table.reshape(-1), q, k_new, v_new, _suffix_matrix(SB_BLOCK_KEYS, N_HEADS_B), k_pool, v_pool)


ROW_TILES = D_MODEL // LANES


def _store_row_tiles(ref, val):
    for j in range(ROW_TILES):
        ref[:, j, :] = val[:, j * LANES:(j + 1) * LANES]


def _load_row_tiles(ref):
    return jnp.concatenate([ref[:, j, :] for j in range(ROW_TILES)], axis=1)


def _merge_kernel(x_ref, oa_ref, ob_ref, sig_ref, wua_ref, wub_ref, wo_ref, gf_ref, wr_hi_ref, wr_lo_ref, br_ref,
                  h_ref, hn_ref, lg_ref):
    sig_a = sig_ref[:, :D_MODEL].astype(F32)
    sig_b = sig_ref[:, D_MODEL:].astype(F32)
    merged = sig_a * _dot(oa_ref[...], wua_ref[...]) + sig_b * _dot(ob_ref[...], wub_ref[...])
    h = x_ref[...] + _dot(merged.astype(BF16), wo_ref[...])
    h_ref[...] = h
    hn = _rmsnorm_rows(h, gf_ref[...])
    _store_row_tiles(hn_ref, hn)
    hn_hi = hn.astype(BF16)
    hn_lo = (hn - hn_hi.astype(F32)).astype(BF16)
    lg_ref[...] = (_dot(hn_hi, wr_hi_ref[...]) + _dot(hn_lo, wr_hi_ref[...])
                   + _dot(hn_hi, wr_lo_ref[...]) + br_ref[...])


def _merge(x, o_a, o_b, sig, w_up_a, w_up_b, w_out, g_ffn, wr_hi, wr_lo, b_r, tm):
    rows = x.shape[0]
    row_spec = lambda width: pl.BlockSpec((tm, width), lambda i: (i, 0))
    const = lambda shape: pl.BlockSpec(shape, lambda i: (0, 0), pipeline_mode=pl.Buffered(1))
    return pl.pallas_call(
        _merge_kernel,
        grid=(rows // tm,),
        in_specs=[row_spec(D_MODEL), row_spec(A_WIDTH), row_spec(B_WIDTH), row_spec(2 * D_MODEL),
                  const((A_WIDTH, D_MODEL)), const((B_WIDTH, D_MODEL)), const((D_MODEL, D_MODEL)),
                  const((1, D_MODEL)), const((D_MODEL, ROUTER_LANES)), const((D_MODEL, ROUTER_LANES)),
                  const((1, ROUTER_LANES))],
        out_specs=[row_spec(D_MODEL), pl.BlockSpec((tm, ROW_TILES, LANES), lambda i: (i, 0, 0)),
                   row_spec(ROUTER_LANES)],
        out_shape=[jax.ShapeDtypeStruct((rows, D_MODEL), F32),
                   jax.ShapeDtypeStruct((rows, ROW_TILES, LANES), F32),
                   jax.ShapeDtypeStruct((rows, ROUTER_LANES), F32)],
        compiler_params=_params(("arbitrary",)),
        name="merge",
    )(x, o_a, o_b, sig, w_up_a, w_up_b, w_out, g_ffn, wr_hi, wr_lo, b_r)


def _router_kernel(lg_ref, eid_ref, gate_ref):
    row = lambda r: lg_ref[r:r + 1, :]
    groups = [row(g) for g in range(N_GROUPS)]
    g_max = functools.reduce(jnp.maximum, groups)
    g_sel = jnp.full(g_max.shape, N_GROUPS - 1, jnp.int32)
    for g in reversed(range(N_GROUPS)):
        g_sel = jnp.where(groups[g] == g_max, g, g_sel)
    p_group = 1.0 / functools.reduce(lambda a, c: a + c, [jnp.exp(v - g_max) for v in groups])
    inside = []
    for e in range(EXPERTS_PER_GROUP):
        v = jnp.zeros_like(g_max)
        for g in range(N_GROUPS):
            v = jnp.where(g_sel == g, row(N_GROUPS + g * EXPERTS_PER_GROUP + e), v)
        inside.append(v)

    def top1(vals, skip):
        best = functools.reduce(jnp.maximum, [jnp.where(skip == e, NEG_INF, v) for e, v in enumerate(vals)])
        idx = jnp.full(best.shape, EXPERTS_PER_GROUP - 1, jnp.int32)
        for e in reversed(range(EXPERTS_PER_GROUP)):
            idx = jnp.where((vals[e] == best) & (skip != e), e, idx)
        return best, idx

    t1, i1 = top1(inside, jnp.full(g_max.shape, -1, jnp.int32))
    t2, i2 = top1(inside, i1)
    r = jnp.exp(t2 - t1)
    eid_ref[0:1, :] = g_sel * EXPERTS_PER_GROUP + i1
    eid_ref[1:2, :] = g_sel * EXPERTS_PER_GROUP + i2
    gate_ref[0:1, :] = p_group * (1.0 / (1.0 + r))
    gate_ref[1:2, :] = p_group * (r / (1.0 + r))


def _router(logits_t, tn):
    n_rows, n_tok = logits_t.shape
    return pl.pallas_call(
        _router_kernel,
        grid=(n_tok // tn,),
        in_specs=[pl.BlockSpec((n_rows, tn), lambda i: (0, i))],
        out_specs=[pl.BlockSpec((TOP_K_EXPERTS, tn), lambda i: (0, i)),
                   pl.BlockSpec((TOP_K_EXPERTS, tn), lambda i: (0, i))],
        out_shape=[jax.ShapeDtypeStruct((TOP_K_EXPERTS, n_tok), jnp.int32),
                   jax.ShapeDtypeStruct((TOP_K_EXPERTS, n_tok), F32)],
        compiler_params=_params(("arbitrary",)),
        name="router",
    )(logits_t)


def _experts_kernel(blk_e_ref, blk_cnt_ref, row_src_ref, hn_hbm, wg_ref, wu_ref, wd_ref, ys_hbm,
                    x_buf, y_buf, in_sem, out_sem):
    i = pl.program_id(0)
    n = pl.num_programs(0)
    rows = x_buf.shape[1]
    slot = i % 2

    def gather(blk, ss, do):
        def body(r, carry):
            tok = lax.shift_right_logical(row_src_ref[blk * rows + r], 1)
            do(pltpu.make_async_copy(hn_hbm.at[tok], x_buf.at[ss, r], in_sem.at[ss]))
            return carry
        lax.fori_loop(0, blk_cnt_ref[blk], body, 0)

    def scatter(blk, ss, do):
        def body(r, carry):
            do(pltpu.make_async_copy(y_buf.at[ss, r], ys_hbm.at[row_src_ref[blk * rows + r]], out_sem.at[ss]))
            return carry
        lax.fori_loop(0, blk_cnt_ref[blk], body, 0)

    @pl.when(i == 0)
    def _():
        x_buf[...] = jnp.zeros_like(x_buf)
        gather(0, 0, lambda cp: cp.start())

    @pl.when(i + 1 < n)
    def _():
        gather(i + 1, 1 - slot, lambda cp: cp.start())

    gather(i, slot, lambda cp: cp.wait())

    @pl.when(blk_cnt_ref[i] > 0)
    def _():
        xb = _load_row_tiles(x_buf.at[slot]).astype(BF16)
        g = _dot(xb, wg_ref[0])
        u = _dot(xb, wu_ref[0])
        act = (g * (1.0 / (1.0 + jnp.exp(-g))) * u).astype(BF16)
        _store_row_tiles(y_buf.at[slot], _dot(act, wd_ref[0]))

    @pl.when(i > 0)
    def _():
        scatter(i - 1, 1 - slot, lambda cp: cp.wait())

    scatter(i, slot, lambda cp: cp.start())

    @pl.when(i == n - 1)
    def _():
        scatter(i, slot, lambda cp: cp.wait())


def _experts(blk_e, blk_cnt, row_src, hn, w_gate, w_up, w_down):
    n_blocks = blk_e.shape[0]
    n_pairs = hn.shape[0] * TOP_K_EXPERTS
    rows = EXPERT_ROWS
    w_spec = lambda shape: pl.BlockSpec((1,) + shape, lambda i, be, bc, rs: (be[i], 0, 0))
    grid_spec = pltpu.PrefetchScalarGridSpec(
        num_scalar_prefetch=3,
        grid=(n_blocks,),
        in_specs=[pl.BlockSpec(memory_space=pl.ANY),
                  w_spec((D_MODEL, D_EXPERT)), w_spec((D_MODEL, D_EXPERT)), w_spec((D_EXPERT, D_MODEL))],
        out_specs=pl.BlockSpec(memory_space=pl.ANY),
        scratch_shapes=[pltpu.VMEM((2, rows, ROW_TILES, LANES), F32), pltpu.VMEM((2, rows, ROW_TILES, LANES), F32),
                        pltpu.SemaphoreType.DMA((2,)), pltpu.SemaphoreType.DMA((2,))],
    )
    return pl.pallas_call(
        _experts_kernel,
        grid_spec=grid_spec,
        out_shape=jax.ShapeDtypeStruct((n_pairs, ROW_TILES, LANES), F32),
        compiler_params=_params(("arbitrary",)),
        name="experts",
    )(blk_e, blk_cnt, row_src, hn, w_gate, w_up, w_down)


def _combine_kernel(h_ref, gate_ref, gfin_ref, ys_ref, y_ref, *, final_norm):
    y = h_ref[...]
    for k in range(TOP_K_EXPERTS):
        y = y + _load_row_tiles(ys_ref.at[:, k]) * gate_ref[:, k:k + 1]
    y_ref[...] = _rmsnorm_rows(y, gfin_ref[...]) if final_norm else y


def _combine(h, gates, g_final, ys, tt, final_norm):
    n_tok = h.shape[0]
    return pl.pallas_call(
        functools.partial(_combine_kernel, final_norm=final_norm),
        grid=(n_tok // tt,),
        in_specs=[pl.BlockSpec((tt, D_MODEL), lambda i: (i, 0)),
                  pl.BlockSpec((tt, TOP_K_EXPERTS), lambda i: (i, 0)),
                  pl.BlockSpec((1, D_MODEL), lambda i: (0, 0)),
                  pl.BlockSpec((tt, TOP_K_EXPERTS, ROW_TILES, LANES), lambda i: (i, 0, 0, 0))],
        out_specs=pl.BlockSpec((tt, D_MODEL), lambda i: (i, 0)),
        out_shape=jax.ShapeDtypeStruct((n_tok, D_MODEL), F32),
        compiler_params=_params(("arbitrary",)),
        name="combine",
    )(h, gates, g_final, ys)


def _moe(h, hn, logits, g_final, w_gate, w_up, w_down, tt, final_norm):
    n_tok = h.shape[0]
    n_rows = N_GROUPS + N_EXPERTS
    pad_rows = -(-n_rows // 8) * 8
    logits_t = jnp.pad(logits[:, :n_rows].T, ((0, pad_rows - n_rows), (0, 0)))
    eid, gates = _router(logits_t, min(n_tok, 2048))
    flat_e = eid.T.reshape(-1)
    nk = flat_e.shape[0]
    rows = EXPERT_ROWS
    se, order = lax.sort_key_val(flat_e, jnp.arange(nk, dtype=jnp.int32))
    bounds = jnp.searchsorted(se, jnp.arange(N_EXPERTS + 1, dtype=jnp.int32), side="left").astype(jnp.int32)
    start, counts = bounds[:-1], bounds[1:] - bounds[:-1]
    blocks_per_e = (counts + rows - 1) // rows
    blk_end = jnp.cumsum(blocks_per_e)
    n_blocks = -(-(nk + N_EXPERTS * (rows - 1)) // rows)
    blk = jnp.arange(n_blocks, dtype=jnp.int32)
    blk_e = jnp.minimum(jnp.searchsorted(blk_end, blk, side="right"), N_EXPERTS - 1).astype(jnp.int32)
    first = (blk - (blk_end - blocks_per_e)[blk_e]) * rows
    blk_cnt = jnp.where(blk < blk_end[-1], jnp.clip(counts[blk_e] - first, 0, rows), 0).astype(jnp.int32)
    src = (start[blk_e] + first)[:, None] + jnp.arange(rows, dtype=jnp.int32)[None, :]
    row_src = order[jnp.clip(src, 0, nk - 1)].reshape(-1)
    ys = _experts(blk_e, blk_cnt, row_src, hn, w_gate, w_up, w_down)
    return _combine(h, gates.T, g_final, ys.reshape(n_tok, TOP_K_EXPERTS, ROW_TILES, LANES), tt, final_norm)


def _split_router(w_rg, b_rg, w_re, b_re):
    w = jnp.concatenate([w_rg, w_re], axis=1)
    w = jnp.pad(w, ((0, 0), (0, ROUTER_LANES - w.shape[1])))
    b = jnp.pad(jnp.concatenate([b_rg, b_re]), (0, ROUTER_LANES - N_GROUPS - N_EXPERTS)).reshape(1, ROUTER_LANES)
    w_hi = w.astype(BF16)
    w_lo = (w - w_hi.astype(F32)).astype(BF16)
    return w_hi, w_lo, b


def _layer_weights(l, g_mix, w_in, w_up_a, w_up_b, w_out, g_ffn, w_rg, b_rg, w_re, b_re, w_e_gate, w_e_up, w_e_down):
    n_rope = A_WIDTH + 2 * D_HEAD + IDX_WIDTH + D_IDX + N_IDX_HEADS
    w = w_in[l]
    w_a = jnp.pad(w[:, :n_rope], ((0, 0), (0, -n_rope % LANES))).astype(BF16)
    w_bg = w[:, n_rope:].astype(BF16)
    return dict(
        g_mix=g_mix[l].reshape(1, D_MODEL), w_a=w_a, w_bg=w_bg,
        w_up_a=w_up_a[l].astype(BF16), w_up_b=w_up_b[l].astype(BF16), w_out=w_out[l].astype(BF16),
        g_ffn=g_ffn[l].reshape(1, D_MODEL), router=_split_router(w_rg[l], b_rg[l], w_re[l], b_re[l]),
        w_e_gate=w_e_gate[l].astype(BF16), w_e_up=w_e_up[l].astype(BF16), w_e_down=w_e_down[l].astype(BF16))


def _project(x2d, wts, pos, tm):
    n_pos_blocks = pos.shape[0] // tm
    tabs = _rope_tables(pos, D_HEAD) + _rope_tables(pos, D_IDX)
    q_a, k_a, v_a, q_i, kiwi = _inproj_rope(x2d, wts["g_mix"], wts["w_a"], tabs, tm, n_pos_blocks)
    q_b, k_b, v_b, sig = _inproj_plain(x2d, wts["g_mix"], wts["w_bg"], tm)
    return q_a, k_a, v_a, q_i, kiwi, q_b, k_b, v_b, sig


def _pad_rows(a, n):
    return jnp.pad(a, ((0, 0), (0, n - a.shape[1]), (0, 0)))


def kernel(x_prompt, x_sample, cache_a_k, cache_a_v, cache_idx_k, cache_b_k, cache_b_v, page_table,
           g_mix, w_in, w_up_a, w_up_b, w_out, g_ffn, w_rg, b_rg, w_re, b_re,
           w_e_gate, w_e_up, w_e_down, g_final):
    depth = w_in.shape[0]
    batch, seq, _ = x_prompt.shape
    dec_batch, dec_seq, _ = x_sample.shape
    n_pages = page_table.shape[1]
    past = n_pages * PAGE_SIZE
    n_phys = cache_a_k.shape[1]
    gfin = g_final.reshape(1, D_MODEL)
    hp = x_prompt.reshape(batch * seq, D_MODEL)
    hs = x_sample.reshape(dec_batch * dec_seq, D_MODEL)
    rows_s = dec_batch * dec_seq
    new_p, new_s = [], []
    for l in range(depth):
        wts = _layer_weights(l, g_mix, w_in, w_up_a, w_up_b, w_out, g_ffn, w_rg, b_rg, w_re, b_re,
                             w_e_gate, w_e_up, w_e_down)
        last = l == depth - 1

        q_a, k_a, v_a, q_i, kiwi, q_b, k_b, v_b, sig = _project(hp, wts, jnp.arange(seq), 512)
        per_seq = lambda a: a.reshape(batch, seq, a.shape[-1])
        o_a = _dsa_prompt(per_seq(q_i), per_seq(kiwi), per_seq(q_a), per_seq(k_a), per_seq(v_a),
                          batch, seq, 256).reshape(batch * seq, A_WIDTH)
        o_b = _sb_prompt(q_b, k_b, v_b, batch, seq, 512)
        h, hn, logits = _merge(hp, o_a, o_b, sig, wts["w_up_a"], wts["w_up_b"], wts["w_out"], wts["g_ffn"],
                               *wts["router"], 256)
        new_p.append((k_a.reshape(batch, seq, D_HEAD), v_a.reshape(batch, seq, D_HEAD),
                      kiwi[:, :D_IDX].reshape(batch, seq, D_IDX),
                      k_b.reshape(batch, seq, N_HEADS_B, D_HEAD), v_b.reshape(batch, seq, N_HEADS_B, D_HEAD)))
        hp = _moe(h, hn, logits, gfin, wts["w_e_gate"], wts["w_e_up"], wts["w_e_down"], 256, last)

        pos_s = past + jnp.arange(dec_seq)
        q_a, k_a, v_a, q_i, kiwi, q_b, k_b, v_b, sig = _project(hs, wts, jnp.tile(pos_s, dec_batch), rows_s)
        per_b = lambda a: a.reshape(dec_batch, dec_seq, -1)
        heads_first = lambda a, nh, d: per_b(a).reshape(dec_batch, dec_seq, nh, d).transpose(0, 2, 1, 3)
        qi_s = heads_first(q_i, N_IDX_HEADS, D_IDX)
        wi_s = per_b(kiwi)[:, :, D_IDX:D_IDX + N_IDX_HEADS]
        kin_s = _pad_rows(per_b(kiwi)[:, :, :D_IDX], LANES)
        qa_s = heads_first(q_a, N_HEADS_A, D_HEAD).reshape(dec_batch, N_HEADS_A * dec_seq, D_HEAD)
        o_a = _dsa_sample(page_table, qi_s, wi_s, kin_s, qa_s, _pad_rows(per_b(k_a), LANES),
                          _pad_rows(per_b(v_a), LANES), cache_idx_k[l], cache_a_k[l], cache_a_v[l])
        qb_s = heads_first(q_b, N_HEADS_B, D_HEAD).reshape(dec_batch, N_HEADS_B * dec_seq, D_HEAD)
        blk_lanes = SB_BLOCK_KEYS * N_HEADS_B
        kb_new = _pad_rows(k_b.reshape(dec_batch, dec_seq * N_HEADS_B, D_HEAD), blk_lanes)
        vb_new = _pad_rows(v_b.reshape(dec_batch, dec_seq * N_HEADS_B, D_HEAD), blk_lanes)
        pool_rows = lambda c: c.reshape(n_phys, PAGE_SIZE * N_HEADS_B, D_HEAD)
        o_b = _sb_sample(page_table, qb_s, kb_new, vb_new, pool_rows(cache_b_k[l]), pool_rows(cache_b_v[l]))
        tok_major = lambda o, nh: (o.reshape(dec_batch, nh, dec_seq, D_HEAD).transpose(0, 2, 1, 3)
                                   .reshape(rows_s, nh * D_HEAD).astype(BF16))
        o_a = tok_major(o_a, N_HEADS_A)
        o_b = tok_major(o_b, N_HEADS_B)
        h, hn, logits = _merge(hs, o_a, o_b, sig, wts["w_up_a"], wts["w_up_b"], wts["w_out"], wts["g_ffn"],
                               *wts["router"], rows_s)
        new_s.append((k_a.reshape(dec_batch, dec_seq, D_HEAD), v_a.reshape(dec_batch, dec_seq, D_HEAD),
                      kiwi[:, :D_IDX].reshape(dec_batch, dec_seq, D_IDX),
                      k_b.reshape(dec_batch, dec_seq, N_HEADS_B, D_HEAD),
                      v_b.reshape(dec_batch, dec_seq, N_HEADS_B, D_HEAD)))
        hs = _moe(h, hn, logits, gfin, wts["w_e_gate"], wts["w_e_up"], wts["w_e_down"], rows_s, last)

    stack = lambda rows_, i: jnp.stack([r[i] for r in rows_])
    return (hp.reshape(batch, seq, D_MODEL), hs.reshape(dec_batch, dec_seq, D_MODEL),
            stack(new_p, 0), stack(new_p, 1), stack(new_p, 2), stack(new_p, 3), stack(new_p, 4),
            stack(new_s, 0), stack(new_s, 1), stack(new_s, 2), stack(new_s, 3), stack(new_s, 4))
```

```python
import functools
import math

import numpy as np
import jax
import jax.numpy as jnp
from jax import lax
from jax.experimental import pallas as pl
from jax.experimental.pallas import tpu as pltpu

D_MODEL = 2048
D_HEAD = 128
N_HEADS_A = 8
N_IDX_HEADS = 16
D_IDX = 64
TOPK_MAX = 256
N_HEADS_B = 8
ROPE_THETA = 10000.0
N_GROUPS = 4
EXPERTS_PER_GROUP = 8
N_EXPERTS = N_GROUPS * EXPERTS_PER_GROUP
TOP_K_EXPERTS = 2
D_EXPERT = 512
PAGE_SIZE = 128
NORM_EPS = 1e-6
A_WIDTH = N_HEADS_A * D_HEAD
B_WIDTH = N_HEADS_B * D_HEAD
IDX_WIDTH = N_IDX_HEADS * D_IDX

LANES = 128
VMEM_LIMIT = 56 * 1024 * 1024
EXPERT_ROWS = 256
ROUTER_LANES = 128

assert TOP_K_EXPERTS == 2

F32 = jnp.float32
BF16 = jnp.bfloat16
NEG_INF = float("-inf")
INT_MIN = -2 ** 31


def _dot(a, b):
    return jnp.dot(a, b, preferred_element_type=F32)


def _dot_nt(a, b):
    return lax.dot_general(a, b, (((1,), (1,)), ((), ())), preferred_element_type=F32)


def _params(semantics):
    return pltpu.CompilerParams(dimension_semantics=semantics, vmem_limit_bytes=VMEM_LIMIT)


def _rope_tables(pos, d):
    half = d // 2
    inv_freq = np.exp(np.arange(half, dtype=np.float32) * np.float32(-2.0 * math.log(ROPE_THETA) / d))
    ang = np.asarray(pos, np.float32)[:, None] * inv_freq[None, :]
    cos, sin = np.cos(ang), np.sin(ang)
    reps = LANES // d
    cos_t = np.tile(np.concatenate([cos, cos], axis=-1), (1, reps))
    sin_t = np.tile(np.concatenate([-sin, sin], axis=-1), (1, reps))
    return jnp.asarray(cos_t, F32), jnp.asarray(sin_t, F32)


def _swap_halves(blk, d):
    if d == LANES:
        return pltpu.roll(blk, LANES // 2, 1)
    lane = lax.broadcasted_iota(jnp.int32, blk.shape, 1)
    first_half = (lane % d) < (d // 2)
    return jnp.where(first_half, pltpu.roll(blk, LANES - d // 2, 1), pltpu.roll(blk, d // 2, 1))


def _rmsnorm_rows(x, g):
    inv = lax.rsqrt(jnp.mean(x * x, axis=-1, keepdims=True) + NORM_EPS)
    return x * inv * g


def _inproj_rope_kernel(x_ref, g_ref, w_ref, cos_ref, sin_ref, cosi_ref, sini_ref,
                        qa_ref, ka_ref, va_ref, qi_ref, kiwi_ref):
    xn = _rmsnorm_rows(x_ref[...], g_ref[...]).astype(BF16)
    y = _dot(xn, w_ref[...])
    cos, sin = cos_ref[...], sin_ref[...]
    for h in range(N_HEADS_A):
        blk = y[:, h * D_HEAD:(h + 1) * D_HEAD]
        qa_ref[:, h * D_HEAD:(h + 1) * D_HEAD] = (blk * cos + _swap_halves(blk, D_HEAD) * sin).astype(BF16)
    ka = y[:, A_WIDTH:A_WIDTH + D_HEAD]
    ka_ref[...] = ka * cos + _swap_halves(ka, D_HEAD) * sin
    va_ref[...] = y[:, A_WIDTH + D_HEAD:A_WIDTH + 2 * D_HEAD]
    cosi, sini = cosi_ref[...], sini_ref[...]
    base = A_WIDTH + 2 * D_HEAD
    for j in range(IDX_WIDTH // LANES):
        blk = y[:, base + j * LANES:base + (j + 1) * LANES]
        qi_ref[:, j * LANES:(j + 1) * LANES] = (blk * cosi + _swap_halves(blk, D_IDX) * sini).astype(BF16)
    blk = y[:, base + IDX_WIDTH:base + IDX_WIDTH + LANES]
    lane = lax.broadcasted_iota(jnp.int32, blk.shape, 1)
    is_k = lane < D_IDX
    kiwi_ref[...] = blk * jnp.where(is_k, cosi, 1.0) + _swap_halves(blk, D_IDX) * jnp.where(is_k, sini, 0.0)


def _inproj_rope(x, g, w_a, tabs, tm, n_pos_blocks):
    rows = x.shape[0]
    wa_cols = w_a.shape[1]
    cos, sin, cosi, sini = tabs
    row_spec = lambda width: pl.BlockSpec((tm, width), lambda i: (i, 0))
    tab_spec = pl.BlockSpec((tm, LANES), lambda i: (i % n_pos_blocks, 0))
    return pl.pallas_call(
        _inproj_rope_kernel,
        grid=(rows // tm,),
        in_specs=[row_spec(D_MODEL),
                  pl.BlockSpec((1, D_MODEL), lambda i: (0, 0)),
                  pl.BlockSpec((D_MODEL, wa_cols), lambda i: (0, 0)),
                  tab_spec, tab_spec, tab_spec, tab_spec],
        out_specs=[row_spec(A_WIDTH), row_spec(D_HEAD), row_spec(D_HEAD), row_spec(IDX_WIDTH), row_spec(LANES)],
        out_shape=[jax.ShapeDtypeStruct((rows, A_WIDTH), BF16),
                   jax.ShapeDtypeStruct((rows, D_HEAD), F32),
                   jax.ShapeDtypeStruct((rows, D_HEAD), F32),
                   jax.ShapeDtypeStruct((rows, IDX_WIDTH), BF16),
                   jax.ShapeDtypeStruct((rows, LANES), F32)],
        compiler_params=_params(("arbitrary",)),
        name="inproj_rope",
    )(x, g, w_a, cos, sin, cosi, sini)


N_GATE_TILES = 2 * D_MODEL // B_WIDTH


def _inproj_plain_kernel(x_ref, g_ref, w_ref, kv_ref, qs_ref, xn_ref):
    j = pl.program_id(1)

    @pl.when(j == 0)
    def _():
        xn_ref[...] = _rmsnorm_rows(x_ref[...], g_ref[...]).astype(BF16)

    product = lambda: _dot(xn_ref[...], w_ref[...])

    @pl.when(j == 0)
    def _():
        qs_ref[...] = product().astype(BF16)

    @pl.when((j == 1) | (j == 2))
    def _():
        kv_ref[...] = product()

    @pl.when(j >= 3)
    def _():
        qs_ref[...] = (1.0 / (1.0 + jnp.exp(-product()))).astype(BF16)


def _inproj_plain(x, g, w_bg, tm):
    rows = x.shape[0]
    tn = B_WIDTH
    n_col = w_bg.shape[1] // tn
    assert n_col == 3 + N_GATE_TILES
    kv_map = lambda i, j: (i, jnp.where(j >= 2, 1, 0))
    qs_map = lambda i, j: (i, jnp.where(j < 3, N_GATE_TILES, j - 3))
    return pl.pallas_call(
        _inproj_plain_kernel,
        grid=(rows // tm, n_col),
        in_specs=[pl.BlockSpec((tm, D_MODEL), lambda i, j: (i, 0)),
                  pl.BlockSpec((1, D_MODEL), lambda i, j: (0, 0)),
                  pl.BlockSpec((D_MODEL, tn), lambda i, j: (0, j))],
        out_specs=[pl.BlockSpec((tm, tn), kv_map), pl.BlockSpec((tm, tn), qs_map)],
        out_shape=[jax.ShapeDtypeStruct((rows, 2 * tn), F32),
                   jax.ShapeDtypeStruct((rows, (N_GATE_TILES + 1) * tn), BF16)],
        scratch_shapes=[pltpu.VMEM((tm, D_MODEL), BF16)],
        compiler_params=_params(("arbitrary", "arbitrary")),
        name="inproj_plain",
    )(x, g, w_bg)


def _sortable(x):
    b = lax.bitcast_convert_type(x, jnp.int32)
    return b ^ ((b >> 31) & jnp.int32(0x7FFFFFFF))


def _select_topk(keys_ref, neg_ref, valid_fn, k_top):
    rows, width = keys_ref.shape
    n_chunks = width // LANES
    ones = jnp.ones((LANES, LANES), BF16)
    lane = lax.broadcasted_iota(jnp.int32, (rows, LANES), 1)

    def lane_total(acc):
        return _dot(acc.astype(BF16), ones)

    def count(pred):
        acc = jnp.zeros((rows, LANES), F32)
        for c in range(n_chunks):
            acc = acc + jnp.where(pred(keys_ref[:, c * LANES:(c + 1) * LANES], c), 1.0, 0.0)
        return lane_total(acc)

    def value_step(i, ans):
        cand_u = ans | lax.shift_left(jnp.int32(1), 31 - i)
        cand = cand_u ^ jnp.int32(INT_MIN)
        cnt = count(lambda k, c: k >= cand)
        return jnp.where(cnt >= k_top, cand_u, ans)

    thr = lax.fori_loop(0, 32, value_step, jnp.zeros((rows, LANES), jnp.int32)) ^ jnp.int32(INT_MIN)
    n_gt = count(lambda k, c: k > thr)
    n_ge = count(lambda k, c: k >= thr)
    need = k_top - n_gt

    n_bits = (width - 1).bit_length()

    def index_step(i, cst):
        cand = cst | lax.shift_left(jnp.int32(1), n_bits - 1 - i)
        cnt = count(lambda k, c: (k == thr) & (lane + c * LANES < cand))
        return jnp.where(cnt < need, cand, cst)

    cut = lax.cond(jnp.max(n_ge) > k_top,
                   lambda: lax.fori_loop(0, n_bits, index_step, jnp.zeros((rows, LANES), jnp.int32)),
                   lambda: jnp.full((rows, LANES), width, jnp.int32))

    for c in range(n_chunks):
        k = keys_ref[:, c * LANES:(c + 1) * LANES]
        tie_keep = jnp.where(lane + c * LANES <= cut, 0.0, NEG_INF)
        keep = jnp.where(k > thr, 0.0, jnp.where(k == thr, tie_keep, NEG_INF))
        neg_ref[:, c * LANES:(c + 1) * LANES] = jnp.where(valid_fn(c), keep, NEG_INF)


def _dsa_prompt_kernel(qi_ref, wq_ref, kk_ref, qa_ref, ka_ref, va_ref, *rest, k_top, q_start):
    _, o_ref, keys_ref, neg_ref = rest
    tq, width = keys_ref.shape

    w = wq_ref[0, :, D_IDX:D_IDX + N_IDX_HEADS] * ((N_IDX_HEADS * D_IDX) ** -0.5)
    k_idx = kk_ref[0, :, :D_IDX].astype(BF16)
    score = jnp.zeros((tq, width), F32)
    for h in range(N_IDX_HEADS):
        rel = jnp.maximum(_dot_nt(qi_ref[0, :, h * D_IDX:(h + 1) * D_IDX], k_idx), 0.0)
        score = score + rel * w[:, h:h + 1]

    t_pos = q_start + lax.broadcasted_iota(jnp.int32, (tq, LANES), 0)
    lane = lax.broadcasted_iota(jnp.int32, (tq, LANES), 1)
    causal = lambda c: lane + c * LANES <= t_pos
    s_pos = lax.broadcasted_iota(jnp.int32, (tq, width), 1)
    q_pos = q_start + lax.broadcasted_iota(jnp.int32, (tq, width), 0)
    keys_ref[...] = _sortable(jnp.where(s_pos <= q_pos, score, NEG_INF))
    _select_topk(keys_ref, neg_ref, causal, k_top)

    k_att = ka_ref[0].astype(BF16)
    v_att = va_ref[0].astype(BF16)
    exp2_scale = (D_HEAD ** -0.5) * math.log2(math.e)
    for h in range(N_HEADS_A):
        s = _dot_nt(qa_ref[0, :, h * D_HEAD:(h + 1) * D_HEAD], k_att) + neg_ref[...]
        m = jnp.max(s, axis=-1, keepdims=True)
        e = jnp.exp2((s - m) * exp2_scale)
        l = jnp.sum(e, axis=-1, keepdims=True)
        o = _dot(e.astype(BF16), v_att) / l
        o_ref[0, :, h * D_HEAD:(h + 1) * D_HEAD] = o.astype(BF16)


def _dsa_prompt(q_i, kiwi, q_a, k_a, v_a, batch, seq, tq):
    k_top = min(TOPK_MAX, seq // 4)
    out = jnp.zeros((batch, seq, A_WIDTH), BF16)
    for qb in range(seq // tq):
        width = (qb + 1) * tq
        q_spec = lambda w_: pl.BlockSpec((1, tq, w_), lambda b: (b, qb, 0))
        kv_spec = lambda w_: pl.BlockSpec((1, width, w_), lambda b: (b, 0, 0))
        out = pl.pallas_call(
            functools.partial(_dsa_prompt_kernel, k_top=k_top, q_start=qb * tq),
            grid=(batch,),
            in_specs=[q_spec(IDX_WIDTH), q_spec(LANES), kv_spec(LANES), q_spec(A_WIDTH), kv_spec(D_HEAD),
                      kv_spec(D_HEAD), pl.BlockSpec(memory_space=pl.ANY)],
            out_specs=q_spec(A_WIDTH),
            out_shape=jax.ShapeDtypeStruct((batch, seq, A_WIDTH), BF16),
            scratch_shapes=[pltpu.VMEM((tq, width), jnp.int32), pltpu.VMEM((tq, width), F32)],
            input_output_aliases={6: 0},
            compiler_params=_params(("arbitrary",)),
            name=f"dsa_prompt_q{qb}",
        )(q_i, kiwi, kiwi, q_a, k_a, v_a, out)
    return out


def _suffix_matrix(n, reps=1):
    j = lax.broadcasted_iota(jnp.int32, (n * reps, n * reps), 0)
    s = lax.broadcasted_iota(jnp.int32, (n * reps, n * reps), 1)
    same = (j % reps) == (s % reps)
    later = jnp.where(same & (j // reps > s // reps), 1.0, 0.0)
    total = jnp.where(same, 1.0, 0.0)
    return jnp.concatenate([later, total], axis=1).astype(BF16)


def _softplus(z):
    return jnp.maximum(z, 0.0) + jnp.log(1.0 + jnp.exp(-jnp.abs(z)))


def _sb_prompt_kernel(q_ref, k_ref, v_ref, u_ref, o_ref, k16, v16, acc_ref, later_ref):
    qb = pl.program_id(2)
    tq = q_ref.shape[0]
    sub = LANES
    n_sub = tq // sub

    @pl.when(qb == 0)
    def _():
        k16[...] = k_ref[...].astype(BF16)
        v16[...] = v_ref[...].astype(BF16)

    q = q_ref[...]
    u = u_ref[...]

    def block(kb, diagonal):
        start = pl.multiple_of(kb * tq, tq)
        z = _dot_nt(q, k16[pl.ds(start, tq), :]) * (D_HEAD ** -0.5)
        sp = _softplus(z)
        log_beta = z - sp
        if diagonal:
            before = (lax.broadcasted_iota(jnp.int32, (tq, tq), 1)
                      < lax.broadcasted_iota(jnp.int32, (tq, tq), 0))
            sp = jnp.where(before, sp, 0.0)
        sp16 = sp.astype(BF16)
        later = later_ref[...]
        args = [None] * n_sub
        for j in reversed(range(n_sub)):
            cum = _dot(sp16[:, j * sub:(j + 1) * sub], u)
            args[j] = log_beta[:, j * sub:(j + 1) * sub] - cum[:, :sub] - later
            later = later + cum[:, sub:]
        a = jnp.exp(jnp.concatenate(args, axis=1))
        if diagonal:
            a = jnp.where(before, a, 0.0)
        acc_ref[...] += _dot(a.astype(BF16), v16[pl.ds(start, tq), :])
        later_ref[...] = later

    acc_ref[...] = jnp.zeros_like(acc_ref)
    later_ref[...] = jnp.zeros_like(later_ref)
    block(qb, True)

    def body(it, carry):
        block(qb - 1 - it, False)
        return carry

    lax.fori_loop(0, qb, body, 0)
    o_ref[...] = acc_ref[...].astype(BF16)


def _sb_prompt(qs, kv, batch, seq, tq):
    rows = batch * seq
    nq = seq // tq
    q_col = qs.shape[1] // D_HEAD - N_HEADS_B
    q_spec = pl.BlockSpec((tq, D_HEAD), lambda b, h, q: (b * nq + q, q_col + h))
    k_spec = pl.BlockSpec((seq, D_HEAD), lambda b, h, q: (b, h))
    v_spec = pl.BlockSpec((seq, D_HEAD), lambda b, h, q: (b, N_HEADS_B + h))
    return pl.pallas_call(
        _sb_prompt_kernel,
        grid=(batch, N_HEADS_B, nq),
        in_specs=[q_spec, k_spec, v_spec, pl.BlockSpec((LANES, 2 * LANES), lambda b, h, q: (0, 0))],
        out_specs=pl.BlockSpec((tq, D_HEAD), lambda b, h, q: (b * nq + q, h)),
        out_shape=jax.ShapeDtypeStruct((rows, B_WIDTH), BF16),
        scratch_shapes=[pltpu.VMEM((seq, D_HEAD), BF16), pltpu.VMEM((seq, D_HEAD), BF16),
                        pltpu.VMEM((tq, D_HEAD), F32), pltpu.VMEM((tq, LANES), F32)],
        compiler_params=_params(("arbitrary", "arbitrary", "arbitrary")),
        name="sb_prompt",
    )(qs, kv, kv, _suffix_matrix(LANES))


def _page_copies(pt_ref, batch_idx, first_page, n_pages, pools, bufs, slot, sem, rows_per_page, n_logical):
    copies = []
    for p in range(n_pages):
        phys = pt_ref[batch_idx * n_logical + first_page + p]
        for pool, buf in zip(pools, bufs):
            copies.append(pltpu.make_async_copy(
                pool.at[phys], buf.at[slot, pl.ds(p * rows_per_page, rows_per_page)], sem.at[slot]))
    return copies


def _dsa_sample_kernel(pt_ref, qi_ref, wi_ref, kin_ref, qa_ref, kan_ref, van_ref,
                       idx_pool, k_pool, v_pool, o_ref,
                       idx_buf, k_buf, v_buf, keys_ref, neg_ref, sem, *, k_top, n_pages, n_new):
    b = pl.program_id(0)
    nb = pl.num_programs(0)
    past = n_pages * PAGE_SIZE
    slot = b % 2
    copies = lambda bb, ss: _page_copies(pt_ref, bb, 0, n_pages, (idx_pool, k_pool, v_pool),
                                         (idx_buf, k_buf, v_buf), ss, sem, PAGE_SIZE, n_pages)

    @pl.when(b == 0)
    def _():
        for c in copies(0, 0):
            c.start()

    @pl.when(b + 1 < nb)
    def _():
        for c in copies(b + 1, 1 - slot):
            c.start()

    for c in copies(b, slot):
        c.wait()

    k_idx = idx_buf[slot].astype(BF16)
    k_idx_new = kin_ref[0].astype(BF16)
    w = wi_ref[0] * (N_IDX_HEADS ** -0.5)
    n_q = w.shape[0]
    score = jnp.zeros((n_q, past), F32)
    score_new = jnp.zeros((n_q, LANES), F32)
    for h in range(N_IDX_HEADS):
        qh = qi_ref[0, h]
        wh = w[:, h:h + 1]
        score = score + jnp.maximum(_dot_nt(qh, k_idx) * (D_IDX ** -0.5), 0.0) * wh
        score_new = score_new + jnp.maximum(_dot_nt(qh, k_idx_new) * (D_IDX ** -0.5), 0.0) * wh

    t_idx = lax.broadcasted_iota(jnp.int32, (n_q, LANES), 0)
    lane = lax.broadcasted_iota(jnp.int32, (n_q, LANES), 1)
    new_ok = (lane <= t_idx) & (lane < n_new)
    keys_ref[:, :past] = _sortable(score)
    keys_ref[:, past:] = _sortable(jnp.where(new_ok, score_new, NEG_INF))
    n_past_chunks = past // LANES
    _select_topk(keys_ref, neg_ref, lambda c: new_ok if c >= n_past_chunks else (lane >= 0), k_top)

    k_att = k_buf[slot].astype(BF16)
    v_att = v_buf[slot].astype(BF16)
    q = qa_ref[0]
    scale = D_HEAD ** -0.5
    neg = jnp.concatenate([neg_ref[...]] * N_HEADS_A, axis=0)
    s_past = _dot_nt(q, k_att) * scale + neg[:, :past]
    s_new = _dot_nt(q, kan_ref[0].astype(BF16)) * scale + neg[:, past:]
    m = jnp.maximum(jnp.max(s_past, axis=-1, keepdims=True), jnp.max(s_new, axis=-1, keepdims=True))
    e_past = jnp.exp(s_past - m)
    e_new = jnp.exp(s_new - m)
    l = jnp.sum(e_past, axis=-1, keepdims=True) + jnp.sum(e_new, axis=-1, keepdims=True)
    o = _dot(e_past.astype(BF16), v_att) + _dot(e_new.astype(BF16), van_ref[0].astype(BF16))
    o_ref[0] = o / l


def _dsa_sample(page_table, q_i, w_i, k_i_new, q_a, k_a_new, v_a_new, idx_pool, k_pool, v_pool):
    batch, n_pages = page_table.shape
    n_q = w_i.shape[1]
    past = n_pages * PAGE_SIZE
    width = past + LANES
    k_top = min(TOPK_MAX, (past + n_q) // 4)
    blk = lambda shape: pl.BlockSpec((1,) + shape, lambda b, pt: (b,) + (0,) * len(shape))
    hbm = pl.BlockSpec(memory_space=pl.ANY)
    grid_spec = pltpu.PrefetchScalarGridSpec(
        num_scalar_prefetch=1,
        grid=(batch,),
        in_specs=[blk((N_IDX_HEADS, n_q, D_IDX)), blk((n_q, N_IDX_HEADS)), blk((LANES, D_IDX)),
                  blk((N_HEADS_A * n_q, D_HEAD)), blk((LANES, D_HEAD)), blk((LANES, D_HEAD)),
                  hbm, hbm, hbm],
        out_specs=blk((N_HEADS_A * n_q, D_HEAD)),
        scratch_shapes=[pltpu.VMEM((2, past, D_IDX), F32), pltpu.VMEM((2, past, D_HEAD), F32),
                        pltpu.VMEM((2, past, D_HEAD), F32), pltpu.VMEM((n_q, width), jnp.int32),
                        pltpu.VMEM((n_q, width), F32), pltpu.SemaphoreType.DMA((2,))],
    )
    return pl.pallas_call(
        functools.partial(_dsa_sample_kernel, k_top=k_top, n_pages=n_pages, n_new=n_q),
        grid_spec=grid_spec,
        out_shape=jax.ShapeDtypeStruct((batch, N_HEADS_A * n_q, D_HEAD), F32),
        compiler_params=_params(("arbitrary",)),
        name="dsa_sample",
    )(page_table.reshape(-1), q_i, w_i, k_i_new, q_a, k_a_new, v_a_new, idx_pool, k_pool, v_pool)


SB_CHUNK_PAGES = 8
SB_BLOCK_KEYS = 32


def _sb_sample_kernel(pt_ref, q_ref, kn_ref, vn_ref, u_ref, k_pool, v_pool, o_ref,
                      k_buf, v_buf, acc_ref, later_ref, sem, *, n_pages, n_new):
    b = pl.program_id(0)
    j = pl.program_id(1)
    n_chunks = pl.num_programs(1)
    step = b * n_chunks + j
    n_steps = pl.num_programs(0) * n_chunks
    slot = step % 2
    heads = N_HEADS_B
    rows_per_page = PAGE_SIZE * heads
    blk_lanes = SB_BLOCK_KEYS * heads

    def copies(st, ss):
        bb = st // n_chunks
        chunk = n_chunks - 1 - st % n_chunks
        return _page_copies(pt_ref, bb, chunk * SB_CHUNK_PAGES, SB_CHUNK_PAGES, (k_pool, v_pool),
                            (k_buf, v_buf), ss, sem, rows_per_page, n_pages)

    @pl.when(step == 0)
    def _():
        for c in copies(0, 0):
            c.start()

    @pl.when(step + 1 < n_steps)
    def _():
        for c in copies(step + 1, 1 - slot):
            c.start()

    q = q_ref[0]
    n_q = q.shape[0] // heads
    u = u_ref[...]
    scale = D_HEAD ** -0.5

    def head_diag(zt):
        lane_head = lax.broadcasted_iota(jnp.int32, (n_q, zt.shape[1]), 1) % heads
        out = jnp.zeros((n_q, zt.shape[1]), F32)
        for h in range(heads):
            out = out + jnp.where(lane_head == h, zt[h * n_q:(h + 1) * n_q, :], 0.0)
        return out

    def head_spread(a):
        lane_head = lax.broadcasted_iota(jnp.int32, a.shape, 1) % heads
        return jnp.concatenate([jnp.where(lane_head == h, a, 0.0) for h in range(heads)], axis=0).astype(BF16)

    def attend(k16, v16, before, later):
        n = k16.shape[0]
        z = head_diag(_dot_nt(q, k16) * scale)
        sp = _softplus(z)
        log_beta = z - sp
        sp16 = jnp.where(before, sp, 0.0).astype(BF16)
        pieces = []
        for blk in reversed(range(n // blk_lanes)):
            sl = slice(blk * blk_lanes, (blk + 1) * blk_lanes)
            cum = _dot(sp16[:, sl], u)
            pieces.append(log_beta[:, sl] - cum[:, :blk_lanes] - later)
            later = later + cum[:, blk_lanes:]
        a = jnp.exp(jnp.concatenate(pieces[::-1], axis=1))
        a = jnp.where(before, a, 0.0)
        return _dot(head_spread(a), v16), later

    @pl.when(j == 0)
    def _():
        t_idx = lax.broadcasted_iota(jnp.int32, (n_q, blk_lanes), 0)
        key = lax.broadcasted_iota(jnp.int32, (n_q, blk_lanes), 1) // heads
        before = (key < t_idx) & (key < n_new)
        out, later = attend(kn_ref[0].astype(BF16), vn_ref[0].astype(BF16), before,
                            jnp.zeros((n_q, blk_lanes), F32))
        acc_ref[...] = out
        later_ref[...] = later

    for c in copies(step, slot):
        c.wait()

    n_rows = SB_CHUNK_PAGES * rows_per_page
    out, later = attend(k_buf[slot].astype(BF16), v_buf[slot].astype(BF16),
                        jnp.full((n_q, n_rows), True), later_ref[...])
    acc_ref[...] += out
    later_ref[...] = later

    @pl.when(j == n_chunks - 1)
    def _():
        o_ref[0] = acc_ref[...]


def _sb_sample(page_table, q, k_new, v_new, k_pool, v_pool):
    batch, n_pages = page_table.shape
    rows_q = q.shape[1]
    n_q = rows_q // N_HEADS_B
    blk_lanes = SB_BLOCK_KEYS * N_HEADS_B
    n_chunks = n_pages // SB_CHUNK_PAGES
    chunk_rows = SB_CHUNK_PAGES * PAGE_SIZE * N_HEADS_B
    blk = lambda shape: pl.BlockSpec((1,) + shape, lambda b, j, pt: (b,) + (0,) * len(shape))
    hbm = pl.BlockSpec(memory_space=pl.ANY)
    grid_spec = pltpu.PrefetchScalarGridSpec(
        num_scalar_prefetch=1,
        grid=(batch, n_chunks),
        in_specs=[blk((rows_q, D_HEAD)), blk((blk_lanes, D_HEAD)), blk((blk_lanes, D_HEAD)),
                  pl.BlockSpec((blk_lanes, 2 * blk_lanes), lambda b, j, pt: (0, 0)), hbm, hbm],
        out_specs=blk((rows_q, D_HEAD)),
        scratch_shapes=[pltpu.VMEM((2, chunk_rows, D_HEAD), F32), pltpu.VMEM((2, chunk_rows, D_HEAD), F32),
                        pltpu.VMEM((rows_q, D_HEAD), F32), pltpu.VMEM((n_q, blk_lanes), F32),
                        pltpu.SemaphoreType.DMA((2,))],
    )
    return pl.pallas_call(
        functools.partial(_sb_sample_kernel, n_pages=n_pages, n_new=n_q),
        grid_spec=grid_spec,
        out_shape=jax.ShapeDtypeStruct((batch, rows_q, D_HEAD), F32),
        compiler_params=_params(("arbitrary", "arbitrary")),
        name="sb_sample",
    )(page_table.reshape(-1), q, k_new, v_new, _suffix_matrix(SB_BLOCK_KEYS, N_HEADS_B), k_pool, v_pool)


ROW_TILES = D_MODEL // LANES


def _store_row_tiles(ref, val):
    for j in range(ROW_TILES):
        ref[:, j, :] = val[:, j * LANES:(j + 1) * LANES]


def _load_row_tiles(ref):
    return jnp.concatenate([ref[:, j, :] for j in range(ROW_TILES)], axis=1)


def _merge_kernel(x_ref, oa_ref, ob_ref, sig_ref, wua_ref, wub_ref, wo_ref, gf_ref, wr_hi_ref, wr_lo_ref, br_ref,
                  h_ref, hn_ref, lg_ref):
    sig_a = sig_ref[:, :D_MODEL].astype(F32)
    sig_b = sig_ref[:, D_MODEL:].astype(F32)
    merged = sig_a * _dot(oa_ref[...], wua_ref[...]) + sig_b * _dot(ob_ref[...], wub_ref[...])
    h = x_ref[...] + _dot(merged.astype(BF16), wo_ref[...])
    h_ref[...] = h
    hn = _rmsnorm_rows(h, gf_ref[...])
    _store_row_tiles(hn_ref, hn)
    hn_hi = hn.astype(BF16)
    hn_lo = (hn - hn_hi.astype(F32)).astype(BF16)
    lg_ref[...] = (_dot(hn_hi, wr_hi_ref[...]) + _dot(hn_lo, wr_hi_ref[...])
                   + _dot(hn_hi, wr_lo_ref[...]) + br_ref[...])


def _merge(x, o_a, o_b, sig, w_up_a, w_up_b, w_out, g_ffn, wr_hi, wr_lo, b_r, tm):
    rows = x.shape[0]
    row_spec = lambda width: pl.BlockSpec((tm, width), lambda i: (i, 0))
    const = lambda shape: pl.BlockSpec(shape, lambda i: (0, 0), pipeline_mode=pl.Buffered(1))
    return pl.pallas_call(
        _merge_kernel,
        grid=(rows // tm,),
        in_specs=[row_spec(D_MODEL), row_spec(A_WIDTH), row_spec(B_WIDTH), row_spec(2 * D_MODEL),
                  const((A_WIDTH, D_MODEL)), const((B_WIDTH, D_MODEL)), const((D_MODEL, D_MODEL)),
                  const((1, D_MODEL)), const((D_MODEL, ROUTER_LANES)), const((D_MODEL, ROUTER_LANES)),
                  const((1, ROUTER_LANES))],
        out_specs=[row_spec(D_MODEL), pl.BlockSpec((tm, ROW_TILES, LANES), lambda i: (i, 0, 0)),
                   row_spec(ROUTER_LANES)],
        out_shape=[jax.ShapeDtypeStruct((rows, D_MODEL), F32),
                   jax.ShapeDtypeStruct((rows, ROW_TILES, LANES), F32),
                   jax.ShapeDtypeStruct((rows, ROUTER_LANES), F32)],
        compiler_params=_params(("arbitrary",)),
        name="merge",
    )(x, o_a, o_b, sig, w_up_a, w_up_b, w_out, g_ffn, wr_hi, wr_lo, b_r)


def _router_kernel(lg_ref, eid_ref, gate_ref):
    lg = lg_ref[...].T
    row = lambda r: lg[r:r + 1, :]
    groups = [row(g) for g in range(N_GROUPS)]
    g_max = functools.reduce(jnp.maximum, groups)
    g_sel = jnp.full(g_max.shape, N_GROUPS - 1, jnp.int32)
    for g in reversed(range(N_GROUPS)):
        g_sel = jnp.where(groups[g] == g_max, g, g_sel)
    p_group = 1.0 / functools.reduce(lambda a, c: a + c, [jnp.exp(v - g_max) for v in groups])
    inside = []
    for e in range(EXPERTS_PER_GROUP):
        v = jnp.zeros_like(g_max)
        for g in range(N_GROUPS):
            v = jnp.where(g_sel == g, row(N_GROUPS + g * EXPERTS_PER_GROUP + e), v)
        inside.append(v)

    def top1(vals, skip):
        best = functools.reduce(jnp.maximum, [jnp.where(skip == e, NEG_INF, v) for e, v in enumerate(vals)])
        idx = jnp.full(best.shape, EXPERTS_PER_GROUP - 1, jnp.int32)
        for e in reversed(range(EXPERTS_PER_GROUP)):
            idx = jnp.where((vals[e] == best) & (skip != e), e, idx)
        return best, idx

    t1, i1 = top1(inside, jnp.full(g_max.shape, -1, jnp.int32))
    t2, i2 = top1(inside, i1)
    r = jnp.exp(t2 - t1)
    eid_ref[0:1, :] = g_sel * EXPERTS_PER_GROUP + i1
    eid_ref[1:2, :] = g_sel * EXPERTS_PER_GROUP + i2
    gates = jnp.concatenate([p_group * (1.0 / (1.0 + r)), p_group * (r / (1.0 + r)),
                             jnp.zeros((ROUTER_LANES - TOP_K_EXPERTS, r.shape[1]), F32)], axis=0)
    gate_ref[...] = gates.T


def _router(logits, tt):
    n_tok = logits.shape[0]
    return pl.pallas_call(
        _router_kernel,
        grid=(n_tok // tt,),
        in_specs=[pl.BlockSpec((tt, ROUTER_LANES), lambda i: (i, 0))],
        out_specs=[pl.BlockSpec((TOP_K_EXPERTS, tt), lambda i: (0, i)),
                   pl.BlockSpec((tt, ROUTER_LANES), lambda i: (i, 0))],
        out_shape=[jax.ShapeDtypeStruct((TOP_K_EXPERTS, n_tok), jnp.int32),
                   jax.ShapeDtypeStruct((n_tok, ROUTER_LANES), F32)],
        compiler_params=_params(("arbitrary",)),
        name="router",
    )(logits)


def _experts_kernel(blk_e_ref, blk_off_ref, blk_cnt_ref, nact_ref, order_ref,
                    hn_hbm, wg_ref, wu_ref, wd_ref, ys_hbm,
                    x_buf, y_buf, wg16, wu16, wd16, in_sem, out_sem, *, n_tok):
    i = pl.program_id(0)
    n_act = nact_ref[0]
    rows = x_buf.shape[1]
    slot = i % 2
    n_pairs = TOP_K_EXPERTS * n_tok

    def gathers(blk, ss):
        off = blk_off_ref[blk]
        copies = []
        for r in range(rows):
            p = order_ref[off + r]
            tok = jnp.where(p >= n_tok, p - n_tok, p)
            copies.append(pltpu.make_async_copy(hn_hbm.at[tok], x_buf.at[ss, r], in_sem.at[ss]))
        return copies

    def scatters(blk, ss):
        off = blk_off_ref[blk]
        cnt = blk_cnt_ref[blk]
        copies = []
        for r in range(rows):
            dst = jnp.where(r < cnt, order_ref[off + r], n_pairs + ss * rows + r)
            copies.append(pltpu.make_async_copy(y_buf.at[ss, r], ys_hbm.at[dst], out_sem.at[ss]))
        return copies

    @pl.when(i == 0)
    def _():
        for c in gathers(0, 0):
            c.start()
        y_buf[1] = jnp.zeros(y_buf.shape[1:], F32)
        for ss in range(2):
            spare = pltpu.make_async_copy(y_buf.at[1], ys_hbm.at[pl.ds(n_pairs + ss * rows, rows)], out_sem.at[1])
            spare.start()
            spare.wait()

    @pl.when((i >= 2) & (i < n_act))
    def _():
        for c in scatters(i - 2, slot):
            c.wait()

    first_of_expert = (i == 0) | (blk_e_ref[i] != blk_e_ref[jnp.maximum(i - 1, 0)])

    @pl.when((i < n_act) & first_of_expert)
    def _():
        wg16[...] = wg_ref[0].astype(BF16)
        wu16[...] = wu_ref[0].astype(BF16)
        wd16[...] = wd_ref[0].astype(BF16)

    @pl.when(i < n_act)
    def _():
        for c in gathers(i, slot):
            c.wait()
        for c in gathers(jnp.minimum(i + 1, n_act - 1), 1 - slot):
            c.start()
        xb = _load_row_tiles(x_buf.at[slot]).astype(BF16)
        g = _dot(xb, wg16[...])
        u = _dot(xb, wu16[...])
        act = (g * (1.0 / (1.0 + jnp.exp(-g))) * u).astype(BF16)
        _store_row_tiles(y_buf.at[slot], _dot(act, wd16[...]))
        for c in scatters(i, slot):
            c.start()

    @pl.when(i == n_act - 1)
    def _():
        for c in gathers(i, 1 - slot):
            c.wait()
        for c in scatters(i, slot):
            c.wait()

    @pl.when((i == n_act - 1) & (i >= 1))
    def _():
        for c in scatters(i - 1, 1 - slot):
            c.wait()


def _experts(blk_e, blk_off, blk_cnt, n_act, order, hn, w_gate, w_up, w_down):
    n_blocks = blk_e.shape[0]
    n_tok = hn.shape[0]
    rows = EXPERT_ROWS
    w_spec = lambda shape: pl.BlockSpec((1,) + shape, lambda i, be, *_: (be[i], 0, 0))
    row_buf = pltpu.VMEM((2, rows, ROW_TILES, LANES), F32)
    grid_spec = pltpu.PrefetchScalarGridSpec(
        num_scalar_prefetch=5,
        grid=(n_blocks,),
        in_specs=[pl.BlockSpec(memory_space=pl.ANY),
                  w_spec((D_MODEL, D_EXPERT)), w_spec((D_MODEL, D_EXPERT)), w_spec((D_EXPERT, D_MODEL))],
        out_specs=pl.BlockSpec(memory_space=pl.ANY),
        scratch_shapes=[row_buf, row_buf,
                        pltpu.VMEM((D_MODEL, D_EXPERT), BF16), pltpu.VMEM((D_MODEL, D_EXPERT), BF16),
                        pltpu.VMEM((D_EXPERT, D_MODEL), BF16),
                        pltpu.SemaphoreType.DMA((2,)), pltpu.SemaphoreType.DMA((2,))],
    )
    return pl.pallas_call(
        functools.partial(_experts_kernel, n_tok=n_tok),
        grid_spec=grid_spec,
        out_shape=jax.ShapeDtypeStruct((TOP_K_EXPERTS * n_tok + 2 * rows, ROW_TILES, LANES), F32),
        compiler_params=_params(("arbitrary",)),
        name="experts",
    )(blk_e, blk_off, blk_cnt, n_act, order, hn, w_gate, w_up, w_down)


def _combine_kernel(h_ref, gate_ref, gfin_ref, *rest, final_norm):
    y_ref = rest[-1]
    y = h_ref[...]
    for k, ys_ref in enumerate(rest[:-1]):
        y = y + _load_row_tiles(ys_ref) * gate_ref[:, k:k + 1]
    y_ref[...] = _rmsnorm_rows(y, gfin_ref[...]) if final_norm else y


def _combine(h, gates, g_final, ys, tt, final_norm):
    n_tok = h.shape[0]
    n_tiles = n_tok // tt
    ys_spec = lambda k: pl.BlockSpec((tt, ROW_TILES, LANES), lambda i: (k * n_tiles + i, 0, 0))
    return pl.pallas_call(
        functools.partial(_combine_kernel, final_norm=final_norm),
        grid=(n_tiles,),
        in_specs=[pl.BlockSpec((tt, D_MODEL), lambda i: (i, 0)),
                  pl.BlockSpec((tt, ROUTER_LANES), lambda i: (i, 0)),
                  pl.BlockSpec((1, D_MODEL), lambda i: (0, 0))] + [ys_spec(k) for k in range(TOP_K_EXPERTS)],
        out_specs=pl.BlockSpec((tt, D_MODEL), lambda i: (i, 0)),
        out_shape=jax.ShapeDtypeStruct((n_tok, D_MODEL), F32),
        compiler_params=_params(("arbitrary",)),
        name="combine",
    )(h, gates, g_final, *([ys] * TOP_K_EXPERTS))


def _moe(h, hn, logits, g_final, w_gate, w_up, w_down, tt, final_norm):
    n_tok = h.shape[0]
    eid, gates = _router(logits, min(n_tok, 2048))
    flat_e = eid.reshape(-1)
    nk = flat_e.shape[0]
    rows = EXPERT_ROWS
    _, order = lax.sort_key_val(flat_e, jnp.arange(nk, dtype=jnp.int32))
    experts = jnp.arange(N_EXPERTS, dtype=jnp.int32)
    counts = jnp.sum((flat_e[:, None] == experts[None, :]).astype(jnp.int32), axis=0)
    inclusive = lambda v: jnp.sum(jnp.where(experts[:, None] <= experts[None, :], v[:, None], 0), axis=0)
    start = inclusive(counts) - counts
    blocks_per_e = (counts + rows - 1) // rows
    blk_end = inclusive(blocks_per_e)
    blk_begin = blk_end - blocks_per_e
    n_blocks = -(-(nk + N_EXPERTS * (rows - 1)) // rows)
    blk = jnp.arange(n_blocks, dtype=jnp.int32)
    mine = (blk[:, None] >= blk_begin[None, :]) & (blk[:, None] < blk_end[None, :])
    pick = lambda v: jnp.sum(jnp.where(mine, v[None, :], 0), axis=1)
    first = (blk - pick(blk_begin)) * rows
    blk_cnt = jnp.clip(pick(counts) - first, 0, rows)
    blk_off = jnp.clip(pick(start) + first, 0, nk)
    order = jnp.concatenate([order, jnp.zeros((rows,), jnp.int32)])
    ys = _experts(pick(experts), blk_off, blk_cnt, blk_end[-1:], order, hn, w_gate, w_up, w_down)
    return _combine(h, gates, g_final, ys, tt, final_norm)


def _split_router(w_rg, b_rg, w_re, b_re):
    w = jnp.concatenate([w_rg, w_re], axis=1)
    w = jnp.pad(w, ((0, 0), (0, ROUTER_LANES - w.shape[1])))
    b = jnp.pad(jnp.concatenate([b_rg, b_re]), (0, ROUTER_LANES - N_GROUPS - N_EXPERTS)).reshape(1, ROUTER_LANES)
    w_hi = w.astype(BF16)
    w_lo = (w - w_hi.astype(F32)).astype(BF16)
    return w_hi, w_lo, b


def _layer_weights(l, g_mix, w_in, w_up_a, w_up_b, w_out, g_ffn, w_rg, b_rg, w_re, b_re, w_e_gate, w_e_up, w_e_down):
    n_rope = A_WIDTH + 2 * D_HEAD + IDX_WIDTH + D_IDX + N_IDX_HEADS
    w = w_in[l]
    w_a = jnp.pad(w[:, :n_rope], ((0, 0), (0, -n_rope % LANES))).astype(BF16)
    w_bg = w[:, n_rope:].astype(BF16)
    return dict(
        g_mix=g_mix[l].reshape(1, D_MODEL), w_a=w_a, w_bg=w_bg,
        w_up_a=w_up_a[l].astype(BF16), w_up_b=w_up_b[l].astype(BF16), w_out=w_out[l].astype(BF16),
        g_ffn=g_ffn[l].reshape(1, D_MODEL), router=_split_router(w_rg[l], b_rg[l], w_re[l], b_re[l]),
        w_e_gate=w_e_gate[l], w_e_up=w_e_up[l], w_e_down=w_e_down[l])


def _project(x2d, wts, pos, tm_rope, tm_plain):
    tabs = _rope_tables(pos, D_HEAD) + _rope_tables(pos, D_IDX)
    q_a, k_a, v_a, q_i, kiwi = _inproj_rope(x2d, wts["g_mix"], wts["w_a"], tabs, tm_rope, pos.shape[0] // tm_rope)
    kv, qs = _inproj_plain(x2d, wts["g_mix"], wts["w_bg"], tm_plain)
    return q_a, k_a, v_a, q_i, kiwi, kv, qs


def _pad_rows(a, n):
    return jnp.pad(a, ((0, 0), (0, n - a.shape[1]), (0, 0)))


def kernel(x_prompt, x_sample, cache_a_k, cache_a_v, cache_idx_k, cache_b_k, cache_b_v, page_table,
           g_mix, w_in, w_up_a, w_up_b, w_out, g_ffn, w_rg, b_rg, w_re, b_re,
           w_e_gate, w_e_up, w_e_down, g_final):
    depth = w_in.shape[0]
    batch, seq, _ = x_prompt.shape
    dec_batch, dec_seq, _ = x_sample.shape
    n_pages = page_table.shape[1]
    past = n_pages * PAGE_SIZE
    n_phys = cache_a_k.shape[1]
    gfin = g_final.reshape(1, D_MODEL)
    hp = x_prompt.reshape(batch * seq, D_MODEL)
    hs = x_sample.reshape(dec_batch * dec_seq, D_MODEL)
    rows_s = dec_batch * dec_seq
    new_p, new_s = [], []
    for l in range(depth):
        wts = _layer_weights(l, g_mix, w_in, w_up_a, w_up_b, w_out, g_ffn, w_rg, b_rg, w_re, b_re,
                             w_e_gate, w_e_up, w_e_down)
        last = l == depth - 1

        q_a, k_a, v_a, q_i, kiwi, kv, qs = _project(hp, wts, np.arange(seq), 512, 1024)
        per_seq = lambda a: a.reshape(batch, seq, a.shape[-1])
        o_a = _dsa_prompt(per_seq(q_i), per_seq(kiwi), per_seq(q_a), per_seq(k_a), per_seq(v_a),
                          batch, seq, 256).reshape(batch * seq, A_WIDTH)
        o_b = _sb_prompt(qs, kv, batch, seq, 512)
        h, hn, logits = _merge(hp, o_a, o_b, qs, wts["w_up_a"], wts["w_up_b"], wts["w_out"], wts["g_ffn"],
                               *wts["router"], 256)
        k_b, v_b = kv[:, :B_WIDTH], kv[:, B_WIDTH:]
        new_p.append((k_a.reshape(batch, seq, D_HEAD), v_a.reshape(batch, seq, D_HEAD),
                      kiwi[:, :D_IDX].reshape(batch, seq, D_IDX),
                      k_b.reshape(batch, seq, N_HEADS_B, D_HEAD), v_b.reshape(batch, seq, N_HEADS_B, D_HEAD)))
        hp = _moe(h, hn, logits, gfin, wts["w_e_gate"], wts["w_e_up"], wts["w_e_down"], 256, last)

        pos_s = np.tile(past + np.arange(dec_seq), dec_batch)
        q_a, k_a, v_a, q_i, kiwi, kv, qs = _project(hs, wts, pos_s, rows_s, rows_s)
        q_b, k_b, v_b = qs[:, 2 * D_MODEL:], kv[:, :B_WIDTH], kv[:, B_WIDTH:]
        per_b = lambda a: a.reshape(dec_batch, dec_seq, -1)
        heads_first = lambda a, nh, d: per_b(a).reshape(dec_batch, dec_seq, nh, d).transpose(0, 2, 1, 3)
        qi_s = heads_first(q_i, N_IDX_HEADS, D_IDX)
        wi_s = per_b(kiwi)[:, :, D_IDX:D_IDX + N_IDX_HEADS]
        kin_s = _pad_rows(per_b(kiwi)[:, :, :D_IDX], LANES)
        qa_s = heads_first(q_a, N_HEADS_A, D_HEAD).reshape(dec_batch, N_HEADS_A * dec_seq, D_HEAD)
        o_a = _dsa_sample(page_table, qi_s, wi_s, kin_s, qa_s, _pad_rows(per_b(k_a), LANES),
                          _pad_rows(per_b(v_a), LANES), cache_idx_k[l], cache_a_k[l], cache_a_v[l])
        qb_s = heads_first(q_b, N_HEADS_B, D_HEAD).reshape(dec_batch, N_HEADS_B * dec_seq, D_HEAD)
        blk_lanes = SB_BLOCK_KEYS * N_HEADS_B
        kb_new = _pad_rows(k_b.reshape(dec_batch, dec_seq * N_HEADS_B, D_HEAD), blk_lanes)
        vb_new = _pad_rows(v_b.reshape(dec_batch, dec_seq * N_HEADS_B, D_HEAD), blk_lanes)
        pool_rows = lambda c: c.reshape(n_phys, PAGE_SIZE * N_HEADS_B, D_HEAD)
        o_b = _sb_sample(page_table, qb_s, kb_new, vb_new, pool_rows(cache_b_k[l]), pool_rows(cache_b_v[l]))
        tok_major = lambda o, nh: (o.reshape(dec_batch, nh, dec_seq, D_HEAD).transpose(0, 2, 1, 3)
                                   .reshape(rows_s, nh * D_HEAD).astype(BF16))
        o_a = tok_major(o_a, N_HEADS_A)
        o_b = tok_major(o_b, N_HEADS_B)
        h, hn, logits = _merge(hs, o_a, o_b, qs, wts["w_up_a"], wts["w_up_b"], wts["w_out"], wts["g_ffn"],
                               *wts["router"], rows_s)
        new_s.append((k_a.reshape(dec_batch, dec_seq, D_HEAD), v_a.reshape(dec_batch, dec_seq, D_HEAD),
                      kiwi[:, :D_IDX].reshape(dec_batch, dec_seq, D_IDX),
                      k_b.reshape(dec_batch, dec_seq, N_HEADS_B, D_HEAD),
                      v_b.reshape(dec_batch, dec_seq, N_HEADS_B, D_HEAD)))
        hs = _moe(h, hn, logits, gfin, wts["w_e_gate"], wts["w_e_up"], wts["w_e_down"], rows_s, last)

    stack = lambda rows_, i: jnp.stack([r[i] for r in rows_])
    return (hp.reshape(batch, seq, D_MODEL), hs.reshape(dec_batch, dec_seq, D_MODEL),
            stack(new_p, 0), stack(new_p, 1), stack(new_p, 2), stack(new_p, 3), stack(new_p, 4),
            stack(new_s, 0), stack(new_s, 1), stack(new_s, 2), stack(new_s, 3), stack(new_s, 4))
```

```python
import functools
import math

import numpy as np
import jax
import jax.numpy as jnp
from jax import lax
from jax.experimental import pallas as pl
from jax.experimental.pallas import tpu as pltpu

D_MODEL = 2048
D_HEAD = 128
N_HEADS_A = 8
N_IDX_HEADS = 16
D_IDX = 64
TOPK_MAX = 256
N_HEADS_B = 8
ROPE_THETA = 10000.0
N_GROUPS = 4
EXPERTS_PER_GROUP = 8
N_EXPERTS = N_GROUPS * EXPERTS_PER_GROUP
TOP_K_EXPERTS = 2
D_EXPERT = 512
PAGE_SIZE = 128
NORM_EPS = 1e-6
A_WIDTH = N_HEADS_A * D_HEAD
B_WIDTH = N_HEADS_B * D_HEAD
IDX_WIDTH = N_IDX_HEADS * D_IDX

LANES = 128
VMEM_LIMIT = 56 * 1024 * 1024
EXPERT_ROWS = 256
MERGE_ROWS = 256
ROUTER_LANES = 128

assert TOP_K_EXPERTS == 2

F32 = jnp.float32
BF16 = jnp.bfloat16
NEG_INF = float("-inf")
INT_MIN = -2 ** 31


def _dot(a, b):
    return jnp.dot(a, b, preferred_element_type=F32)


def _dot_nt(a, b):
    return lax.dot_general(a, b, (((1,), (1,)), ((), ())), preferred_element_type=F32)


def _params(semantics):
    return pltpu.CompilerParams(dimension_semantics=semantics, vmem_limit_bytes=VMEM_LIMIT)


def _rope_tables(pos, d):
    half = d // 2
    inv_freq = np.exp(np.arange(half, dtype=np.float32) * np.float32(-2.0 * math.log(ROPE_THETA) / d))
    ang = np.asarray(pos, np.float32)[:, None] * inv_freq[None, :]
    cos, sin = np.cos(ang), np.sin(ang)
    reps = LANES // d
    cos_t = np.tile(np.concatenate([cos, cos], axis=-1), (1, reps))
    sin_t = np.tile(np.concatenate([-sin, sin], axis=-1), (1, reps))
    return jnp.asarray(cos_t, F32), jnp.asarray(sin_t, F32)


def _swap_halves(blk, d):
    if d == LANES:
        return pltpu.roll(blk, LANES // 2, 1)
    lane = lax.broadcasted_iota(jnp.int32, blk.shape, 1)
    first_half = (lane % d) < (d // 2)
    return jnp.where(first_half, pltpu.roll(blk, LANES - d // 2, 1), pltpu.roll(blk, d // 2, 1))


def _rmsnorm_rows(x, g):
    inv = lax.rsqrt(jnp.mean(x * x, axis=-1, keepdims=True) + NORM_EPS)
    return x * inv * g


def _inproj_rope_kernel(x_ref, g_ref, w_ref, cos_ref, sin_ref, cosi_ref, sini_ref,
                        qa_ref, ka_ref, va_ref, qi_ref, kiwi_ref):
    xn = _rmsnorm_rows(x_ref[...], g_ref[...]).astype(BF16)
    y = _dot(xn, w_ref[...])
    cos, sin = cos_ref[...], sin_ref[...]
    for h in range(N_HEADS_A):
        blk = y[:, h * D_HEAD:(h + 1) * D_HEAD]
        qa_ref[:, h * D_HEAD:(h + 1) * D_HEAD] = (blk * cos + _swap_halves(blk, D_HEAD) * sin).astype(BF16)
    ka = y[:, A_WIDTH:A_WIDTH + D_HEAD]
    ka_ref[...] = ka * cos + _swap_halves(ka, D_HEAD) * sin
    va_ref[...] = y[:, A_WIDTH + D_HEAD:A_WIDTH + 2 * D_HEAD]
    cosi, sini = cosi_ref[...], sini_ref[...]
    base = A_WIDTH + 2 * D_HEAD
    for j in range(IDX_WIDTH // LANES):
        blk = y[:, base + j * LANES:base + (j + 1) * LANES]
        qi_ref[:, j * LANES:(j + 1) * LANES] = (blk * cosi + _swap_halves(blk, D_IDX) * sini).astype(BF16)
    blk = y[:, base + IDX_WIDTH:base + IDX_WIDTH + LANES]
    lane = lax.broadcasted_iota(jnp.int32, blk.shape, 1)
    is_k = lane < D_IDX
    kiwi_ref[...] = blk * jnp.where(is_k, cosi, 1.0) + _swap_halves(blk, D_IDX) * jnp.where(is_k, sini, 0.0)


def _inproj_rope(x, g, w_a, tabs, tm, n_pos_blocks):
    rows = x.shape[0]
    wa_cols = w_a.shape[1]
    cos, sin, cosi, sini = tabs
    row_spec = lambda width: pl.BlockSpec((tm, width), lambda i: (i, 0))
    tab_spec = pl.BlockSpec((tm, LANES), lambda i: (i % n_pos_blocks, 0))
    return pl.pallas_call(
        _inproj_rope_kernel,
        grid=(rows // tm,),
        in_specs=[row_spec(D_MODEL),
                  pl.BlockSpec((1, D_MODEL), lambda i: (0, 0)),
                  pl.BlockSpec((D_MODEL, wa_cols), lambda i: (0, 0)),
                  tab_spec, tab_spec, tab_spec, tab_spec],
        out_specs=[row_spec(A_WIDTH), row_spec(D_HEAD), row_spec(D_HEAD), row_spec(IDX_WIDTH), row_spec(LANES)],
        out_shape=[jax.ShapeDtypeStruct((rows, A_WIDTH), BF16),
                   jax.ShapeDtypeStruct((rows, D_HEAD), F32),
                   jax.ShapeDtypeStruct((rows, D_HEAD), F32),
                   jax.ShapeDtypeStruct((rows, IDX_WIDTH), BF16),
                   jax.ShapeDtypeStruct((rows, LANES), F32)],
        compiler_params=_params(("arbitrary",)),
        name="inproj_rope",
    )(x, g, w_a, cos, sin, cosi, sini)


N_GATE_TILES = 2 * D_MODEL // B_WIDTH


def _inproj_plain_kernel(x_ref, g_ref, w_ref, kb_ref, vb_ref, qs_ref, xn_ref):
    j = pl.program_id(1)

    @pl.when(j == 0)
    def _():
        xn_ref[...] = _rmsnorm_rows(x_ref[...], g_ref[...]).astype(BF16)

    product = lambda: _dot(xn_ref[...], w_ref[...])

    @pl.when(j == 0)
    def _():
        qs_ref[...] = product().astype(BF16)

    @pl.when(j == 1)
    def _():
        kb_ref[...] = product()

    @pl.when(j == 2)
    def _():
        vb_ref[...] = product()

    @pl.when(j >= 3)
    def _():
        qs_ref[...] = (1.0 / (1.0 + jnp.exp(-product()))).astype(BF16)


def _inproj_plain(x, g, w_bg, tm):
    rows = x.shape[0]
    tn = B_WIDTH
    n_col = w_bg.shape[1] // tn
    assert n_col == 3 + N_GATE_TILES
    fixed = pl.BlockSpec((tm, tn), lambda i, j: (i, 0))
    qs_map = lambda i, j: (i, jnp.where(j < 3, N_GATE_TILES, j - 3))
    return pl.pallas_call(
        _inproj_plain_kernel,
        grid=(rows // tm, n_col),
        in_specs=[pl.BlockSpec((tm, D_MODEL), lambda i, j: (i, 0)),
                  pl.BlockSpec((1, D_MODEL), lambda i, j: (0, 0)),
                  pl.BlockSpec((D_MODEL, tn), lambda i, j: (0, j))],
        out_specs=[fixed, fixed, pl.BlockSpec((tm, tn), qs_map)],
        out_shape=[jax.ShapeDtypeStruct((rows, tn), F32), jax.ShapeDtypeStruct((rows, tn), F32),
                   jax.ShapeDtypeStruct((rows, (N_GATE_TILES + 1) * tn), BF16)],
        scratch_shapes=[pltpu.VMEM((tm, D_MODEL), BF16)],
        compiler_params=_params(("arbitrary", "arbitrary")),
        name="inproj_plain",
    )(x, g, w_bg)


def _sortable(x):
    b = lax.bitcast_convert_type(x, jnp.int32)
    return b ^ ((b >> 31) & jnp.int32(0x7FFFFFFF))


def _select_topk(keys_ref, neg_ref, valid_fn, k_top):
    rows, width = keys_ref.shape
    n_chunks = width // LANES
    ones = jnp.ones((LANES, LANES), BF16)
    lane = lax.broadcasted_iota(jnp.int32, (rows, LANES), 1)

    def lane_total(acc):
        return _dot(acc.astype(BF16), ones)

    def count(pred):
        acc = jnp.zeros((rows, LANES), F32)
        for c in range(n_chunks):
            acc = acc + jnp.where(pred(keys_ref[:, c * LANES:(c + 1) * LANES], c), 1.0, 0.0)
        return lane_total(acc)

    def value_step(i, ans):
        cand_u = ans | lax.shift_left(jnp.int32(1), 31 - i)
        cand = cand_u ^ jnp.int32(INT_MIN)
        cnt = count(lambda k, c: k >= cand)
        return jnp.where(cnt >= k_top, cand_u, ans)

    thr = lax.fori_loop(0, 32, value_step, jnp.zeros((rows, LANES), jnp.int32)) ^ jnp.int32(INT_MIN)
    n_gt = count(lambda k, c: k > thr)
    n_ge = count(lambda k, c: k >= thr)
    need = k_top - n_gt

    n_bits = (width - 1).bit_length()

    def index_step(i, cst):
        cand = cst | lax.shift_left(jnp.int32(1), n_bits - 1 - i)
        cnt = count(lambda k, c: (k == thr) & (lane + c * LANES < cand))
        return jnp.where(cnt < need, cand, cst)

    cut = lax.cond(jnp.max(n_ge) > k_top,
                   lambda: lax.fori_loop(0, n_bits, index_step, jnp.zeros((rows, LANES), jnp.int32)),
                   lambda: jnp.full((rows, LANES), width, jnp.int32))

    for c in range(n_chunks):
        k = keys_ref[:, c * LANES:(c + 1) * LANES]
        tie_keep = jnp.where(lane + c * LANES <= cut, 0.0, NEG_INF)
        keep = jnp.where(k > thr, 0.0, jnp.where(k == thr, tie_keep, NEG_INF))
        neg_ref[:, c * LANES:(c + 1) * LANES] = jnp.where(valid_fn(c), keep, NEG_INF)


def _dsa_prompt_kernel(qi_ref, wq_ref, kk_ref, qa_ref, ka_ref, va_ref, *rest, k_top, q_start):
    _, o_ref, keys_ref, neg_ref = rest
    tq, width = keys_ref.shape

    w = wq_ref[0, :, D_IDX:D_IDX + N_IDX_HEADS] * ((N_IDX_HEADS * D_IDX) ** -0.5)
    k_idx = kk_ref[0, :, :D_IDX].astype(BF16)
    score = jnp.zeros((tq, width), F32)
    for h in range(N_IDX_HEADS):
        rel = jnp.maximum(_dot_nt(qi_ref[0, :, h * D_IDX:(h + 1) * D_IDX], k_idx), 0.0)
        score = score + rel * w[:, h:h + 1]

    t_pos = q_start + lax.broadcasted_iota(jnp.int32, (tq, LANES), 0)
    lane = lax.broadcasted_iota(jnp.int32, (tq, LANES), 1)
    causal = lambda c: lane + c * LANES <= t_pos
    s_pos = lax.broadcasted_iota(jnp.int32, (tq, width), 1)
    q_pos = q_start + lax.broadcasted_iota(jnp.int32, (tq, width), 0)
    keys_ref[...] = _sortable(jnp.where(s_pos <= q_pos, score, NEG_INF))
    _select_topk(keys_ref, neg_ref, causal, k_top)

    k_att = ka_ref[0].astype(BF16)
    v_att = va_ref[0].astype(BF16)
    exp2_scale = (D_HEAD ** -0.5) * math.log2(math.e)
    for h in range(N_HEADS_A):
        s = _dot_nt(qa_ref[0, :, h * D_HEAD:(h + 1) * D_HEAD], k_att) + neg_ref[...]
        m = jnp.max(s, axis=-1, keepdims=True)
        e = jnp.exp2((s - m) * exp2_scale)
        l = jnp.sum(e, axis=-1, keepdims=True)
        o = _dot(e.astype(BF16), v_att) / l
        o_ref[0, :, h * D_HEAD:(h + 1) * D_HEAD] = o.astype(BF16)


def _dsa_prompt(q_i, kiwi, q_a, k_a, v_a, batch, seq, tq):
    k_top = min(TOPK_MAX, seq // 4)
    out = jnp.zeros((batch, seq, A_WIDTH), BF16)
    for qb in range(seq // tq):
        width = (qb + 1) * tq
        q_spec = lambda w_: pl.BlockSpec((1, tq, w_), lambda b: (b, qb, 0))
        kv_spec = lambda w_: pl.BlockSpec((1, width, w_), lambda b: (b, 0, 0))
        out = pl.pallas_call(
            functools.partial(_dsa_prompt_kernel, k_top=k_top, q_start=qb * tq),
            grid=(batch,),
            in_specs=[q_spec(IDX_WIDTH), q_spec(LANES), kv_spec(LANES), q_spec(A_WIDTH), kv_spec(D_HEAD),
                      kv_spec(D_HEAD), pl.BlockSpec(memory_space=pl.ANY)],
            out_specs=q_spec(A_WIDTH),
            out_shape=jax.ShapeDtypeStruct((batch, seq, A_WIDTH), BF16),
            scratch_shapes=[pltpu.VMEM((tq, width), jnp.int32), pltpu.VMEM((tq, width), F32)],
            input_output_aliases={6: 0},
            compiler_params=_params(("arbitrary",)),
            name=f"dsa_prompt_q{qb}",
        )(q_i, kiwi, kiwi, q_a, k_a, v_a, out)
    return out


def _suffix_matrix(n, reps=1):
    j = lax.broadcasted_iota(jnp.int32, (n * reps, n * reps), 0)
    s = lax.broadcasted_iota(jnp.int32, (n * reps, n * reps), 1)
    same = (j % reps) == (s % reps)
    later = jnp.where(same & (j // reps > s // reps), 1.0, 0.0)
    total = jnp.where(same, 1.0, 0.0)
    return jnp.concatenate([later, total], axis=1).astype(BF16)


def _softplus(z):
    return jnp.maximum(z, 0.0) + jnp.log(1.0 + jnp.exp(-jnp.abs(z)))


def _sb_prompt_kernel(q_ref, k_ref, v_ref, u_ref, o_ref, k16, v16, acc_ref, later_ref):
    qb = pl.program_id(2)
    tq = q_ref.shape[0]
    sub = LANES
    n_sub = tq // sub

    @pl.when(qb == 0)
    def _():
        k16[...] = k_ref[...].astype(BF16)
        v16[...] = v_ref[...].astype(BF16)

    q = q_ref[...]
    u = u_ref[...]

    def block(kb, diagonal):
        start = pl.multiple_of(kb * tq, tq)
        z = _dot_nt(q, k16[pl.ds(start, tq), :]) * (D_HEAD ** -0.5)
        sp = _softplus(z)
        log_beta = z - sp
        if diagonal:
            before = (lax.broadcasted_iota(jnp.int32, (tq, tq), 1)
                      < lax.broadcasted_iota(jnp.int32, (tq, tq), 0))
            sp = jnp.where(before, sp, 0.0)
        sp16 = sp.astype(BF16)
        later = later_ref[...]
        args = [None] * n_sub
        for j in reversed(range(n_sub)):
            cum = _dot(sp16[:, j * sub:(j + 1) * sub], u)
            args[j] = log_beta[:, j * sub:(j + 1) * sub] - cum[:, :sub] - later
            later = later + cum[:, sub:]
        a = jnp.exp(jnp.concatenate(args, axis=1))
        if diagonal:
            a = jnp.where(before, a, 0.0)
        acc_ref[...] += _dot(a.astype(BF16), v16[pl.ds(start, tq), :])
        later_ref[...] = later

    acc_ref[...] = jnp.zeros_like(acc_ref)
    later_ref[...] = jnp.zeros_like(later_ref)
    block(qb, True)

    def body(it, carry):
        block(qb - 1 - it, False)
        return carry

    lax.fori_loop(0, qb, body, 0)
    o_ref[...] = acc_ref[...].astype(BF16)


def _sb_prompt(qs, k_b, v_b, batch, seq, tq):
    rows = batch * seq
    nq = seq // tq
    q_col = qs.shape[1] // D_HEAD - N_HEADS_B
    q_spec = pl.BlockSpec((tq, D_HEAD), lambda b, h, q: (b * nq + q, q_col + h))
    kv_spec = pl.BlockSpec((seq, D_HEAD), lambda b, h, q: (b, h))
    return pl.pallas_call(
        _sb_prompt_kernel,
        grid=(batch, N_HEADS_B, nq),
        in_specs=[q_spec, kv_spec, kv_spec, pl.BlockSpec((LANES, 2 * LANES), lambda b, h, q: (0, 0))],
        out_specs=pl.BlockSpec((tq, D_HEAD), lambda b, h, q: (b * nq + q, h)),
        out_shape=jax.ShapeDtypeStruct((rows, B_WIDTH), BF16),
        scratch_shapes=[pltpu.VMEM((seq, D_HEAD), BF16), pltpu.VMEM((seq, D_HEAD), BF16),
                        pltpu.VMEM((tq, D_HEAD), F32), pltpu.VMEM((tq, LANES), F32)],
        compiler_params=_params(("arbitrary", "arbitrary", "arbitrary")),
        name="sb_prompt",
    )(qs, k_b, v_b, _suffix_matrix(LANES))


def _page_copies(pt_ref, batch_idx, first_page, n_pages, pools, bufs, slot, sem, rows_per_page, n_logical):
    copies = []
    for p in range(n_pages):
        phys = pt_ref[batch_idx * n_logical + first_page + p]
        for pool, buf in zip(pools, bufs):
            copies.append(pltpu.make_async_copy(
                pool.at[phys], buf.at[slot, pl.ds(p * rows_per_page, rows_per_page)], sem.at[slot]))
    return copies


def _paged_double_buffer(pt_ref, pools, bufs, sem, n_pages, rows_per_page):
    b = pl.program_id(0)
    slot = b % 2
    copies = lambda bb, ss: _page_copies(pt_ref, bb, 0, n_pages, pools, bufs, ss, sem, rows_per_page, n_pages)

    @pl.when(b == 0)
    def _():
        for c in copies(0, 0):
            c.start()

    @pl.when(b + 1 < pl.num_programs(0))
    def _():
        for c in copies(b + 1, 1 - slot):
            c.start()

    for c in copies(b, slot):
        c.wait()
    return slot


def _new_key_visible(n_q, n_new, rows):
    t_idx = lax.broadcasted_iota(jnp.int32, (rows, LANES), 0) % n_q
    lane = lax.broadcasted_iota(jnp.int32, (rows, LANES), 1)
    return (lane <= t_idx) & (lane < n_new)


def _idx_scores_sample_kernel(pt_ref, qi_ref, w_ref, kin_ref, idx_pool, s_ref, idx_buf, sem, *, n_pages, n_new):
    slot = _paged_double_buffer(pt_ref, (idx_pool,), (idx_buf,), sem, n_pages, PAGE_SIZE)
    past = n_pages * PAGE_SIZE
    q = qi_ref[0]
    w = w_ref[0]
    n_q = q.shape[0] // N_IDX_HEADS

    def head_sum(keys16):
        rel = jnp.maximum(_dot_nt(q, keys16), 0.0) * w
        return functools.reduce(lambda a, c: a + c, [rel[h * n_q:(h + 1) * n_q] for h in range(N_IDX_HEADS)])

    s_ref[0, :, :past] = head_sum(idx_buf[slot].astype(BF16))
    s_new = head_sum(kin_ref[0].astype(BF16))
    s_ref[0, :, past:] = jnp.where(_new_key_visible(n_q, n_new, n_q), s_new, NEG_INF)


def _topk_mask_kernel(s_ref, neg_ref, keys_ref, *, k_top, n_q, n_new):
    rows, width = keys_ref.shape
    keys_ref[...] = _sortable(s_ref[...])
    visible = _new_key_visible(n_q, n_new, rows)
    last = width // LANES - 1
    _select_topk(keys_ref, neg_ref, lambda c: visible if c == last else jnp.full((rows, LANES), True), k_top)


def _dsa_sample_attend_kernel(pt_ref, qa_ref, neg_ref, kan_ref, van_ref, k_pool, v_pool, o_ref,
                              k_buf, v_buf, sem, *, n_pages):
    slot = _paged_double_buffer(pt_ref, (k_pool, v_pool), (k_buf, v_buf), sem, n_pages, PAGE_SIZE)
    past = n_pages * PAGE_SIZE
    k_att = k_buf[slot].astype(BF16)
    v_att = v_buf[slot].astype(BF16)
    q = qa_ref[0]
    neg = jnp.concatenate([neg_ref[0]] * N_HEADS_A, axis=0)
    s_past = _dot_nt(q, k_att) + neg[:, :past]
    s_new = _dot_nt(q, kan_ref[0].astype(BF16)) + neg[:, past:]
    m = jnp.maximum(jnp.max(s_past, axis=-1, keepdims=True), jnp.max(s_new, axis=-1, keepdims=True))
    exp2_scale = (D_HEAD ** -0.5) * math.log2(math.e)
    e_past = jnp.exp2((s_past - m) * exp2_scale)
    e_new = jnp.exp2((s_new - m) * exp2_scale)
    l = jnp.sum(e_past, axis=-1, keepdims=True) + jnp.sum(e_new, axis=-1, keepdims=True)
    o = _dot(e_past.astype(BF16), v_att) + _dot(e_new.astype(BF16), van_ref[0].astype(BF16))
    o_ref[0] = o / l


def _dsa_sample(page_table, q_i, w_i, k_i_new, q_a, k_a_new, v_a_new, idx_pool, k_pool, v_pool):
    batch, n_pages = page_table.shape
    n_q = q_a.shape[1] // N_HEADS_A
    past = n_pages * PAGE_SIZE
    width = past + LANES
    k_top = min(TOPK_MAX, (past + n_q) // 4)
    pt = page_table.reshape(-1)
    blk = lambda shape: pl.BlockSpec((1,) + shape, lambda b, pt_: (b,) + (0,) * len(shape))
    hbm = pl.BlockSpec(memory_space=pl.ANY)

    scores = pl.pallas_call(
        functools.partial(_idx_scores_sample_kernel, n_pages=n_pages, n_new=n_q),
        grid_spec=pltpu.PrefetchScalarGridSpec(
            num_scalar_prefetch=1, grid=(batch,),
            in_specs=[blk((N_IDX_HEADS * n_q, D_IDX)), blk((N_IDX_HEADS * n_q, 1)), blk((LANES, D_IDX)), hbm],
            out_specs=blk((n_q, width)),
            scratch_shapes=[pltpu.VMEM((2, past, D_IDX), F32), pltpu.SemaphoreType.DMA((2,))]),
        out_shape=jax.ShapeDtypeStruct((batch, n_q, width), F32),
        compiler_params=_params(("arbitrary",)),
        name="idx_scores_sample",
    )(pt, q_i, w_i, k_i_new, idx_pool)

    rows = batch * n_q
    tr = min(rows, LANES)
    neg = pl.pallas_call(
        functools.partial(_topk_mask_kernel, k_top=k_top, n_q=n_q, n_new=n_q),
        grid=(rows // tr,),
        in_specs=[pl.BlockSpec((tr, width), lambda i: (i, 0))],
        out_specs=pl.BlockSpec((tr, width), lambda i: (i, 0)),
        out_shape=jax.ShapeDtypeStruct((rows, width), F32),
        scratch_shapes=[pltpu.VMEM((tr, width), jnp.int32)],
        compiler_params=_params(("arbitrary",)),
        name="topk_mask_sample",
    )(scores.reshape(rows, width))

    return pl.pallas_call(
        functools.partial(_dsa_sample_attend_kernel, n_pages=n_pages),
        grid_spec=pltpu.PrefetchScalarGridSpec(
            num_scalar_prefetch=1, grid=(batch,),
            in_specs=[blk((N_HEADS_A * n_q, D_HEAD)), blk((n_q, width)), blk((LANES, D_HEAD)), blk((LANES, D_HEAD)),
                      hbm, hbm],
            out_specs=blk((N_HEADS_A * n_q, D_HEAD)),
            scratch_shapes=[pltpu.VMEM((2, past, D_HEAD), F32), pltpu.VMEM((2, past, D_HEAD), F32),
                            pltpu.SemaphoreType.DMA((2,))]),
        out_shape=jax.ShapeDtypeStruct((batch, N_HEADS_A * n_q, D_HEAD), F32),
        compiler_params=_params(("arbitrary",)),
        name="dsa_sample_attend",
    )(pt, q_a, neg.reshape(batch, n_q, width), k_a_new, v_a_new, k_pool, v_pool)


SB_CHUNK_PAGES = 8
SB_BLOCK_KEYS = 32


def _sb_sample_kernel(pt_ref, q_ref, kn_ref, vn_ref, u_ref, k_pool, v_pool, o_ref,
                      k_buf, v_buf, acc_ref, later_ref, sem, *, n_pages, n_new):
    b = pl.program_id(0)
    j = pl.program_id(1)
    n_chunks = pl.num_programs(1)
    step = b * n_chunks + j
    n_steps = pl.num_programs(0) * n_chunks
    slot = step % 2
    heads = N_HEADS_B
    rows_per_page = PAGE_SIZE * heads
    blk_lanes = SB_BLOCK_KEYS * heads

    def copies(st, ss):
        bb = st // n_chunks
        chunk = n_chunks - 1 - st % n_chunks
        return _page_copies(pt_ref, bb, chunk * SB_CHUNK_PAGES, SB_CHUNK_PAGES, (k_pool, v_pool),
                            (k_buf, v_buf), ss, sem, rows_per_page, n_pages)

    @pl.when(step == 0)
    def _():
        for c in copies(0, 0):
            c.start()

    @pl.when(step + 1 < n_steps)
    def _():
        for c in copies(step + 1, 1 - slot):
            c.start()

    q = q_ref[0]
    n_q = q.shape[0] // heads
    u = u_ref[...]
    scale = D_HEAD ** -0.5

    def head_diag(zt):
        lane_head = lax.broadcasted_iota(jnp.int32, (n_q, zt.shape[1]), 1) % heads
        out = jnp.zeros((n_q, zt.shape[1]), F32)
        for h in range(heads):
            out = out + jnp.where(lane_head == h, zt[h * n_q:(h + 1) * n_q, :], 0.0)
        return out

    def head_spread(a):
        lane_head = lax.broadcasted_iota(jnp.int32, a.shape, 1) % heads
        return jnp.concatenate([jnp.where(lane_head == h, a, 0.0) for h in range(heads)], axis=0).astype(BF16)

    def attend(k16, v16, before, later):
        n = k16.shape[0]
        z = head_diag(_dot_nt(q, k16) * scale)
        sp = _softplus(z)
        log_beta = z - sp
        sp16 = jnp.where(before, sp, 0.0).astype(BF16)
        pieces = []
        for blk in reversed(range(n // blk_lanes)):
            sl = slice(blk * blk_lanes, (blk + 1) * blk_lanes)
            cum = _dot(sp16[:, sl], u)
            pieces.append(log_beta[:, sl] - cum[:, :blk_lanes] - later)
            later = later + cum[:, blk_lanes:]
        a = jnp.exp(jnp.concatenate(pieces[::-1], axis=1))
        a = jnp.where(before, a, 0.0)
        return _dot(head_spread(a), v16), later

    @pl.when(j == 0)
    def _():
        t_idx = lax.broadcasted_iota(jnp.int32, (n_q, blk_lanes), 0)
        key = lax.broadcasted_iota(jnp.int32, (n_q, blk_lanes), 1) // heads
        before = (key < t_idx) & (key < n_new)
        out, later = attend(kn_ref[0].astype(BF16), vn_ref[0].astype(BF16), before,
                            jnp.zeros((n_q, blk_lanes), F32))
        acc_ref[...] = out
        later_ref[...] = later

    for c in copies(step, slot):
        c.wait()

    n_rows = SB_CHUNK_PAGES * rows_per_page
    out, later = attend(k_buf[slot].astype(BF16), v_buf[slot].astype(BF16),
                        jnp.full((n_q, n_rows), True), later_ref[...])
    acc_ref[...] += out
    later_ref[...] = later

    @pl.when(j == n_chunks - 1)
    def _():
        o_ref[0] = acc_ref[...]


def _sb_sample(page_table, q, k_new, v_new, k_pool, v_pool):
    batch, n_pages = page_table.shape
    rows_q = q.shape[1]
    n_q = rows_q // N_HEADS_B
    blk_lanes = SB_BLOCK_KEYS * N_HEADS_B
    n_chunks = n_pages // SB_CHUNK_PAGES
    chunk_rows = SB_CHUNK_PAGES * PAGE_SIZE * N_HEADS_B
    blk = lambda shape: pl.BlockSpec((1,) + shape, lambda b, j, pt: (b,) + (0,) * len(shape))
    hbm = pl.BlockSpec(memory_space=pl.ANY)
    grid_spec = pltpu.PrefetchScalarGridSpec(
        num_scalar_prefetch=1,
        grid=(batch, n_chunks),
        in_specs=[blk((rows_q, D_HEAD)), blk((blk_lanes, D_HEAD)), blk((blk_lanes, D_HEAD)),
                  pl.BlockSpec((blk_lanes, 2 * blk_lanes), lambda b, j, pt: (0, 0)), hbm, hbm],
        out_specs=blk((rows_q, D_HEAD)),
        scratch_shapes=[pltpu.VMEM((2, chunk_rows, D_HEAD), F32), pltpu.VMEM((2, chunk_rows, D_HEAD), F32),
                        pltpu.VMEM((rows_q, D_HEAD), F32), pltpu.VMEM((n_q, blk_lanes), F32),
                        pltpu.SemaphoreType.DMA((2,))],
    )
    return pl.pallas_call(
        functools.partial(_sb_sample_kernel, n_pages=n_pages, n_new=n_q),
        grid_spec=grid_spec,
        out_shape=jax.ShapeDtypeStruct((batch, rows_q, D_HEAD), F32),
        compiler_params=_params(("arbitrary", "arbitrary")),
        name="sb_sample",
    )(page_table.reshape(-1), q, k_new, v_new, _suffix_matrix(SB_BLOCK_KEYS, N_HEADS_B), k_pool, v_pool)


ROW_TILES = D_MODEL // LANES


def _store_row_tiles(ref, val):
    for j in range(ROW_TILES):
        ref[:, j, :] = val[:, j * LANES:(j + 1) * LANES]


def _load_row_tiles(ref):
    return jnp.concatenate([ref[:, j, :] for j in range(ROW_TILES)], axis=1)


def _merge_kernel(*refs, n_prompt_tiles):
    prompt, sample = refs[0:4], refs[4:8]
    wua_ref, wub_ref, wo_ref, gf_ref, wr_hi_ref, wr_lo_ref, br_ref, h_ref, hn_ref, lg_ref = refs[8:]

    def tile(x_ref, oa_ref, ob_ref, sig_ref):
        sig_a = sig_ref[:, :D_MODEL].astype(F32)
        sig_b = sig_ref[:, D_MODEL:].astype(F32)
        merged = sig_a * _dot(oa_ref[...], wua_ref[...]) + sig_b * _dot(ob_ref[...], wub_ref[...])
        h = x_ref[...] + _dot(merged.astype(BF16), wo_ref[...])
        h_ref[...] = h
        hn = _rmsnorm_rows(h, gf_ref[...])
        _store_row_tiles(hn_ref, hn)
        hn_hi = hn.astype(BF16)
        hn_lo = (hn - hn_hi.astype(F32)).astype(BF16)
        lg_ref[...] = (_dot(hn_hi, wr_hi_ref[...]) + _dot(hn_lo, wr_hi_ref[...])
                       + _dot(hn_hi, wr_lo_ref[...]) + br_ref[...])

    i = pl.program_id(0)

    @pl.when(i < n_prompt_tiles)
    def _():
        tile(*prompt)

    @pl.when(i >= n_prompt_tiles)
    def _():
        tile(*sample)


def _merge(prompt, sample, w_up_a, w_up_b, w_out, g_ffn, wr_hi, wr_lo, b_r, tm):
    n_p, n_s = prompt[0].shape[0] // tm, sample[0].shape[0] // tm
    assert prompt[0].shape[0] == n_p * tm and sample[0].shape[0] == n_s * tm
    rows = (n_p + n_s) * tm
    widths = (D_MODEL, A_WIDTH, B_WIDTH, 2 * D_MODEL)
    p_spec = lambda width: pl.BlockSpec((tm, width), lambda i: (jnp.minimum(i, n_p - 1), 0))
    s_spec = lambda width: pl.BlockSpec((tm, width), lambda i: (jnp.maximum(i - n_p, 0), 0))
    row_spec = lambda width: pl.BlockSpec((tm, width), lambda i: (i, 0))
    const = lambda shape: pl.BlockSpec(shape, lambda i: (0, 0), pipeline_mode=pl.Buffered(1))
    return pl.pallas_call(
        functools.partial(_merge_kernel, n_prompt_tiles=n_p),
        grid=(n_p + n_s,),
        in_specs=[p_spec(w_) for w_ in widths] + [s_spec(w_) for w_ in widths] + [
                  const((A_WIDTH, D_MODEL)), const((B_WIDTH, D_MODEL)), const((D_MODEL, D_MODEL)),
                  const((1, D_MODEL)), const((D_MODEL, ROUTER_LANES)), const((D_MODEL, ROUTER_LANES)),
                  const((1, ROUTER_LANES))],
        out_specs=[row_spec(D_MODEL), pl.BlockSpec((tm, ROW_TILES, LANES), lambda i: (i, 0, 0)),
                   row_spec(ROUTER_LANES)],
        out_shape=[jax.ShapeDtypeStruct((rows, D_MODEL), F32),
                   jax.ShapeDtypeStruct((rows, ROW_TILES, LANES), F32),
                   jax.ShapeDtypeStruct((rows, ROUTER_LANES), F32)],
        compiler_params=_params(("arbitrary",)),
        name="merge",
    )(*prompt, *sample, w_up_a, w_up_b, w_out, g_ffn, wr_hi, wr_lo, b_r)


def _router_kernel(lg_ref, eid_ref, gate_ref, *, chunk):
    for c0 in range(0, lg_ref.shape[0], chunk):
        _route_chunk(lg_ref, eid_ref, gate_ref, c0, chunk)


def _route_chunk(lg_ref, eid_ref, gate_ref, c0, chunk):
    lg = lg_ref[c0:c0 + chunk, :].T
    row = lambda r: lg[r:r + 1, :]
    groups = [row(g) for g in range(N_GROUPS)]
    g_max = functools.reduce(jnp.maximum, groups)
    g_sel = jnp.full(g_max.shape, N_GROUPS - 1, jnp.int32)
    for g in reversed(range(N_GROUPS)):
        g_sel = jnp.where(groups[g] == g_max, g, g_sel)
    p_group = 1.0 / functools.reduce(lambda a, c: a + c, [jnp.exp(v - g_max) for v in groups])
    inside = []
    for e in range(EXPERTS_PER_GROUP):
        v = jnp.zeros_like(g_max)
        for g in range(N_GROUPS):
            v = jnp.where(g_sel == g, row(N_GROUPS + g * EXPERTS_PER_GROUP + e), v)
        inside.append(v)

    def top1(vals, skip):
        best = functools.reduce(jnp.maximum, [jnp.where(skip == e, NEG_INF, v) for e, v in enumerate(vals)])
        idx = jnp.full(best.shape, EXPERTS_PER_GROUP - 1, jnp.int32)
        for e in reversed(range(EXPERTS_PER_GROUP)):
            idx = jnp.where((vals[e] == best) & (skip != e), e, idx)
        return best, idx

    t1, i1 = top1(inside, jnp.full(g_max.shape, -1, jnp.int32))
    t2, i2 = top1(inside, i1)
    r = jnp.exp(t2 - t1)
    for k, ids in enumerate((g_sel * EXPERTS_PER_GROUP + i1, g_sel * EXPERTS_PER_GROUP + i2)):
        for c in range(chunk // LANES):
            row0 = c0 // LANES + c
            eid_ref[k, row0:row0 + 1, :] = ids[:, c * LANES:(c + 1) * LANES]
    gates = jnp.concatenate([p_group * (1.0 / (1.0 + r)), p_group * (r / (1.0 + r)),
                             jnp.zeros((ROUTER_LANES - TOP_K_EXPERTS, chunk), F32)], axis=0)
    gate_ref[c0:c0 + chunk, :] = gates.T


def _router(logits):
    n_tok = logits.shape[0]
    assert n_tok % LANES == 0
    chunk = max(c for c in range(LANES, 2048 + 1, LANES) if n_tok % c == 0)
    return pl.pallas_call(
        functools.partial(_router_kernel, chunk=chunk),
        out_shape=[jax.ShapeDtypeStruct((TOP_K_EXPERTS, n_tok // LANES, LANES), jnp.int32),
                   jax.ShapeDtypeStruct((n_tok, ROUTER_LANES), F32)],
        compiler_params=pltpu.CompilerParams(vmem_limit_bytes=VMEM_LIMIT),
        name="router",
    )(logits)


def _experts_kernel(blk_e_ref, blk_off_ref, blk_cnt_ref, nact_ref, order_ref,
                    hn_hbm, wg_ref, wu_ref, wd_ref, ys_hbm,
                    x_buf, y_buf, wg16, wu16, wd16, in_sem, out_sem, *, n_tok):
    i = pl.program_id(0)
    n_act = nact_ref[0]
    rows = x_buf.shape[1]
    slot = i % 2
    n_pairs = TOP_K_EXPERTS * n_tok

    def gathers(blk, ss):
        off = blk_off_ref[blk]
        copies = []
        for r in range(rows):
            p = order_ref[off + r]
            tok = jnp.where(p >= n_tok, p - n_tok, p)
            copies.append(pltpu.make_async_copy(hn_hbm.at[tok], x_buf.at[ss, r], in_sem.at[ss]))
        return copies

    def scatters(blk, ss):
        off = blk_off_ref[blk]
        cnt = blk_cnt_ref[blk]
        copies = []
        for r in range(rows):
            dst = jnp.where(r < cnt, order_ref[off + r], n_pairs + ss * rows + r)
            copies.append(pltpu.make_async_copy(y_buf.at[ss, r], ys_hbm.at[dst], out_sem.at[ss]))
        return copies

    @pl.when(i == 0)
    def _():
        for c in gathers(0, 0):
            c.start()
        y_buf[1] = jnp.zeros(y_buf.shape[1:], F32)
        for ss in range(2):
            spare = pltpu.make_async_copy(y_buf.at[1], ys_hbm.at[pl.ds(n_pairs + ss * rows, rows)], out_sem.at[1])
            spare.start()
            spare.wait()

    @pl.when((i >= 2) & (i < n_act))
    def _():
        for c in scatters(i - 2, slot):
            c.wait()

    first_of_expert = (i == 0) | (blk_e_ref[i] != blk_e_ref[jnp.maximum(i - 1, 0)])

    @pl.when((i < n_act) & first_of_expert)
    def _():
        wg16[...] = wg_ref[0].astype(BF16)
        wu16[...] = wu_ref[0].astype(BF16)
        wd16[...] = wd_ref[0].astype(BF16)

    @pl.when(i < n_act)
    def _():
        for c in gathers(i, slot):
            c.wait()
        for c in gathers(jnp.minimum(i + 1, n_act - 1), 1 - slot):
            c.start()
        xb = _load_row_tiles(x_buf.at[slot]).astype(BF16)
        g = _dot(xb, wg16[...])
        u = _dot(xb, wu16[...])
        act = (g * (1.0 / (1.0 + jnp.exp(-g))) * u).astype(BF16)
        _store_row_tiles(y_buf.at[slot], _dot(act, wd16[...]))
        for c in scatters(i, slot):
            c.start()

    @pl.when(i == n_act - 1)
    def _():
        for c in gathers(i, 1 - slot):
            c.wait()
        for c in scatters(i, slot):
            c.wait()

    @pl.when((i == n_act - 1) & (i >= 1))
    def _():
        for c in scatters(i - 1, 1 - slot):
            c.wait()


def _experts(blk_e, blk_off, blk_cnt, n_act, order, hn, w_gate, w_up, w_down):
    n_blocks = blk_e.shape[0]
    n_tok = hn.shape[0]
    rows = EXPERT_ROWS
    w_spec = lambda shape: pl.BlockSpec((1,) + shape, lambda i, be, *_: (be[i], 0, 0))
    row_buf = pltpu.VMEM((2, rows, ROW_TILES, LANES), F32)
    grid_spec = pltpu.PrefetchScalarGridSpec(
        num_scalar_prefetch=5,
        grid=(n_blocks,),
        in_specs=[pl.BlockSpec(memory_space=pl.ANY),
                  w_spec((D_MODEL, D_EXPERT)), w_spec((D_MODEL, D_EXPERT)), w_spec((D_EXPERT, D_MODEL))],
        out_specs=pl.BlockSpec(memory_space=pl.ANY),
        scratch_shapes=[row_buf, row_buf,
                        pltpu.VMEM((D_MODEL, D_EXPERT), BF16), pltpu.VMEM((D_MODEL, D_EXPERT), BF16),
                        pltpu.VMEM((D_EXPERT, D_MODEL), BF16),
                        pltpu.SemaphoreType.DMA((2,)), pltpu.SemaphoreType.DMA((2,))],
    )
    return pl.pallas_call(
        functools.partial(_experts_kernel, n_tok=n_tok),
        grid_spec=grid_spec,
        out_shape=jax.ShapeDtypeStruct((TOP_K_EXPERTS * n_tok + 2 * rows, ROW_TILES, LANES), F32),
        compiler_params=_params(("arbitrary",)),
        name="experts",
    )(blk_e, blk_off, blk_cnt, n_act, order, hn, w_gate, w_up, w_down)


def _combine_kernel(h_ref, gate_ref, gfin_ref, *rest, final_norm, n_prompt_tiles):
    yp_ref, ys_ref = rest[-2:]
    y = h_ref[...]
    for k, e_ref in enumerate(rest[:-2]):
        y = y + _load_row_tiles(e_ref) * gate_ref[:, k:k + 1]
    y = _rmsnorm_rows(y, gfin_ref[...]) if final_norm else y
    i = pl.program_id(0)

    @pl.when(i < n_prompt_tiles)
    def _():
        yp_ref[...] = y

    @pl.when(i >= n_prompt_tiles)
    def _():
        ys_ref[...] = y


def _combine(h, gates, g_final, ys, tt, n_prompt, final_norm):
    n_tok = h.shape[0]
    n_tiles, n_p = n_tok // tt, n_prompt // tt
    assert n_tok == n_tiles * tt and n_prompt == n_p * tt and n_tiles > n_p
    e_spec = lambda k: pl.BlockSpec((tt, ROW_TILES, LANES), lambda i: (k * n_tiles + i, 0, 0))
    return pl.pallas_call(
        functools.partial(_combine_kernel, final_norm=final_norm, n_prompt_tiles=n_p),
        grid=(n_tiles,),
        in_specs=[pl.BlockSpec((tt, D_MODEL), lambda i: (i, 0)),
                  pl.BlockSpec((tt, ROUTER_LANES), lambda i: (i, 0)),
                  pl.BlockSpec((1, D_MODEL), lambda i: (0, 0))] + [e_spec(k) for k in range(TOP_K_EXPERTS)],
        out_specs=[pl.BlockSpec((tt, D_MODEL), lambda i: (jnp.minimum(i, n_p - 1), 0)),
                   pl.BlockSpec((tt, D_MODEL), lambda i: (jnp.maximum(i - n_p, 0), 0))],
        out_shape=[jax.ShapeDtypeStruct((n_prompt, D_MODEL), F32),
                   jax.ShapeDtypeStruct((n_tok - n_prompt, D_MODEL), F32)],
        compiler_params=_params(("arbitrary",)),
        name="combine",
    )(h, gates, g_final, *([ys] * TOP_K_EXPERTS))


def _moe(h, hn, logits, g_final, w_gate, w_up, w_down, tt, n_prompt, final_norm):
    n_tok = h.shape[0]
    eid, gates = _router(logits)
    flat_e = eid.reshape(-1)
    nk = flat_e.shape[0]
    rows = EXPERT_ROWS
    _, order = lax.sort_key_val(flat_e, jnp.arange(nk, dtype=jnp.int32))
    experts = jnp.arange(N_EXPERTS, dtype=jnp.int32)
    counts = jnp.sum((flat_e[:, None] == experts[None, :]).astype(jnp.int32), axis=0)
    inclusive = lambda v: jnp.sum(jnp.where(experts[:, None] <= experts[None, :], v[:, None], 0), axis=0)
    start = inclusive(counts) - counts
    blocks_per_e = (counts + rows - 1) // rows
    blk_end = inclusive(blocks_per_e)
    blk_begin = blk_end - blocks_per_e
    n_blocks = -(-(nk + N_EXPERTS * (rows - 1)) // rows)
    blk = jnp.arange(n_blocks, dtype=jnp.int32)
    mine = (blk[:, None] >= blk_begin[None, :]) & (blk[:, None] < blk_end[None, :])
    pick = lambda v: jnp.sum(jnp.where(mine, v[None, :], 0), axis=1)
    first = (blk - pick(blk_begin)) * rows
    blk_cnt = jnp.clip(pick(counts) - first, 0, rows)
    blk_off = jnp.clip(pick(start) + first, 0, nk)
    order = jnp.concatenate([order, jnp.zeros((rows,), jnp.int32)])
    ys = _experts(pick(experts), blk_off, blk_cnt, blk_end[-1:], order, hn, w_gate, w_up, w_down)
    return _combine(h, gates, g_final, ys, tt, n_prompt, final_norm)


def _split_router(w_rg, b_rg, w_re, b_re):
    w = jnp.concatenate([w_rg, w_re], axis=1)
    w = jnp.pad(w, ((0, 0), (0, ROUTER_LANES - w.shape[1])))
    b = jnp.pad(jnp.concatenate([b_rg, b_re]), (0, ROUTER_LANES - N_GROUPS - N_EXPERTS)).reshape(1, ROUTER_LANES)
    w_hi = w.astype(BF16)
    w_lo = (w - w_hi.astype(F32)).astype(BF16)
    return w_hi, w_lo, b


def _layer_weights(l, g_mix, w_in, w_up_a, w_up_b, w_out, g_ffn, w_rg, b_rg, w_re, b_re, w_e_gate, w_e_up, w_e_down):
    n_rope = A_WIDTH + 2 * D_HEAD + IDX_WIDTH + D_IDX + N_IDX_HEADS
    w = w_in[l]
    w_a = jnp.pad(w[:, :n_rope], ((0, 0), (0, -n_rope % LANES))).astype(BF16)
    w_bg = w[:, n_rope:].astype(BF16)
    return dict(
        g_mix=g_mix[l].reshape(1, D_MODEL), w_a=w_a, w_bg=w_bg,
        w_up_a=w_up_a[l].astype(BF16), w_up_b=w_up_b[l].astype(BF16), w_out=w_out[l].astype(BF16),
        g_ffn=g_ffn[l].reshape(1, D_MODEL), router=_split_router(w_rg[l], b_rg[l], w_re[l], b_re[l]),
        w_e_gate=w_e_gate[l], w_e_up=w_e_up[l], w_e_down=w_e_down[l])


def _project(x2d, wts, pos, tm_rope, tm_plain):
    tabs = _rope_tables(pos, D_HEAD) + _rope_tables(pos, D_IDX)
    q_a, k_a, v_a, q_i, kiwi = _inproj_rope(x2d, wts["g_mix"], wts["w_a"], tabs, tm_rope, pos.shape[0] // tm_rope)
    k_b, v_b, qs = _inproj_plain(x2d, wts["g_mix"], wts["w_bg"], tm_plain)
    return q_a, k_a, v_a, q_i, kiwi, k_b, v_b, qs


def _pad_rows(a, n):
    return jnp.pad(a, ((0, 0), (0, n - a.shape[1]), (0, 0)))


def kernel(x_prompt, x_sample, cache_a_k, cache_a_v, cache_idx_k, cache_b_k, cache_b_v, page_table,
           g_mix, w_in, w_up_a, w_up_b, w_out, g_ffn, w_rg, b_rg, w_re, b_re,
           w_e_gate, w_e_up, w_e_down, g_final):
    depth = w_in.shape[0]
    batch, seq, _ = x_prompt.shape
    dec_batch, dec_seq, _ = x_sample.shape
    n_pages = page_table.shape[1]
    past = n_pages * PAGE_SIZE
    n_phys = cache_a_k.shape[1]
    gfin = g_final.reshape(1, D_MODEL)
    hp = x_prompt.reshape(batch * seq, D_MODEL)
    hs = x_sample.reshape(dec_batch * dec_seq, D_MODEL)
    rows_s = dec_batch * dec_seq
    new_p, new_s = [], []
    for l in range(depth):
        wts = _layer_weights(l, g_mix, w_in, w_up_a, w_up_b, w_out, g_ffn, w_rg, b_rg, w_re, b_re,
                             w_e_gate, w_e_up, w_e_down)
        last = l == depth - 1

        q_a, k_a, v_a, q_i, kiwi, k_b, v_b, qs = _project(hp, wts, np.arange(seq), 512, 1024)
        per_seq = lambda a: a.reshape(batch, seq, a.shape[-1])
        o_a = _dsa_prompt(per_seq(q_i), per_seq(kiwi), per_seq(q_a), per_seq(k_a), per_seq(v_a),
                          batch, seq, 256).reshape(batch * seq, A_WIDTH)
        o_b = _sb_prompt(qs, k_b, v_b, batch, seq, 512)
        prompt_rows = (hp, o_a, o_b, qs)
        new_p.append((k_a.reshape(batch, seq, D_HEAD), v_a.reshape(batch, seq, D_HEAD),
                      kiwi[:, :D_IDX].reshape(batch, seq, D_IDX),
                      k_b.reshape(batch, seq, N_HEADS_B, D_HEAD), v_b.reshape(batch, seq, N_HEADS_B, D_HEAD)))

        pos_s = np.tile(past + np.arange(dec_seq), dec_batch)
        q_a, k_a, v_a, q_i, kiwi, k_b, v_b, qs = _project(hs, wts, pos_s, rows_s, rows_s)
        q_b = qs[:, 2 * D_MODEL:]
        per_b = lambda a: a.reshape(dec_batch, dec_seq, -1)
        heads_first = lambda a, nh, d: (per_b(a).reshape(dec_batch, dec_seq, nh, d).transpose(0, 2, 1, 3)
                                        .reshape(dec_batch, nh * dec_seq, d))
        w_idx = per_b(kiwi)[:, :, D_IDX:D_IDX + N_IDX_HEADS] * ((N_IDX_HEADS * D_IDX) ** -0.5)
        wi_s = w_idx.transpose(0, 2, 1).reshape(dec_batch, N_IDX_HEADS * dec_seq, 1)
        kin_s = _pad_rows(per_b(kiwi)[:, :, :D_IDX], LANES)
        o_a = _dsa_sample(page_table, heads_first(q_i, N_IDX_HEADS, D_IDX), wi_s, kin_s,
                          heads_first(q_a, N_HEADS_A, D_HEAD), _pad_rows(per_b(k_a), LANES),
                          _pad_rows(per_b(v_a), LANES), cache_idx_k[l], cache_a_k[l], cache_a_v[l])
        blk_lanes = SB_BLOCK_KEYS * N_HEADS_B
        kb_new = _pad_rows(k_b.reshape(dec_batch, dec_seq * N_HEADS_B, D_HEAD), blk_lanes)
        vb_new = _pad_rows(v_b.reshape(dec_batch, dec_seq * N_HEADS_B, D_HEAD), blk_lanes)
        pool_rows = lambda c: c.reshape(n_phys, PAGE_SIZE * N_HEADS_B, D_HEAD)
        o_b = _sb_sample(page_table, heads_first(q_b, N_HEADS_B, D_HEAD), kb_new, vb_new,
                         pool_rows(cache_b_k[l]), pool_rows(cache_b_v[l]))
        tok_major = lambda o, nh: (o.reshape(dec_batch, nh, dec_seq, D_HEAD).transpose(0, 2, 1, 3)
                                   .reshape(rows_s, nh * D_HEAD).astype(BF16))
        sample_rows = (hs, tok_major(o_a, N_HEADS_A), tok_major(o_b, N_HEADS_B), qs)
        new_s.append((k_a.reshape(dec_batch, dec_seq, D_HEAD), v_a.reshape(dec_batch, dec_seq, D_HEAD),
                      kiwi[:, :D_IDX].reshape(dec_batch, dec_seq, D_IDX),
                      k_b.reshape(dec_batch, dec_seq, N_HEADS_B, D_HEAD),
                      v_b.reshape(dec_batch, dec_seq, N_HEADS_B, D_HEAD)))

        h, hn, logits = _merge(prompt_rows, sample_rows, wts["w_up_a"], wts["w_up_b"], wts["w_out"], wts["g_ffn"],
                               *wts["router"], MERGE_ROWS)
        hp, hs = _moe(h, hn, logits, gfin, wts["w_e_gate"], wts["w_e_up"], wts["w_e_down"], MERGE_ROWS,
                      batch * seq, last)

    stack = lambda rows_, i: jnp.stack([r[i] for r in rows_])
    return (hp.reshape(batch, seq, D_MODEL), hs.reshape(dec_batch, dec_seq, D_MODEL),
            stack(new_p, 0), stack(new_p, 1), stack(new_p, 2), stack(new_p, 3), stack(new_p, 4),
            stack(new_s, 0), stack(new_s, 1), stack(new_s, 2), stack(new_s, 3), stack(new_s, 4))
```

```python
import functools
import math

import numpy as np
import jax
import jax.numpy as jnp
from jax import lax
from jax.experimental import pallas as pl
from jax.experimental.pallas import tpu as pltpu

D_MODEL = 2048
D_HEAD = 128
N_HEADS_A = 8
N_IDX_HEADS = 16
D_IDX = 64
TOPK_MAX = 256
N_HEADS_B = 8
ROPE_THETA = 10000.0
N_GROUPS = 4
EXPERTS_PER_GROUP = 8
N_EXPERTS = N_GROUPS * EXPERTS_PER_GROUP
TOP_K_EXPERTS = 2
D_EXPERT = 512
PAGE_SIZE = 128
NORM_EPS = 1e-6
A_WIDTH = N_HEADS_A * D_HEAD
B_WIDTH = N_HEADS_B * D_HEAD
IDX_WIDTH = N_IDX_HEADS * D_IDX

LANES = 128
VMEM_LIMIT = 56 * 1024 * 1024
EXPERT_ROWS = 256
SB_HEADS_PER_STEP = 2
MERGE_ROWS = 256
ROUTER_LANES = 128

assert TOP_K_EXPERTS == 2

F32 = jnp.float32
BF16 = jnp.bfloat16
NEG_INF = float("-inf")
INT_MIN = -2 ** 31


def _dot(a, b):
    return jnp.dot(a, b, preferred_element_type=F32)


def _dot_nt(a, b):
    return lax.dot_general(a, b, (((1,), (1,)), ((), ())), preferred_element_type=F32)


def _params(semantics):
    return pltpu.CompilerParams(dimension_semantics=semantics, vmem_limit_bytes=VMEM_LIMIT)


def _rope_tables(pos, d):
    half = d // 2
    inv_freq = np.exp(np.arange(half, dtype=np.float32) * np.float32(-2.0 * math.log(ROPE_THETA) / d))
    ang = np.asarray(pos, np.float32)[:, None] * inv_freq[None, :]
    cos, sin = np.cos(ang), np.sin(ang)
    reps = LANES // d
    cos_t = np.tile(np.concatenate([cos, cos], axis=-1), (1, reps))
    sin_t = np.tile(np.concatenate([-sin, sin], axis=-1), (1, reps))
    return jnp.asarray(cos_t, F32), jnp.asarray(sin_t, F32)


def _swap_halves(blk, d):
    if d == LANES:
        return pltpu.roll(blk, LANES // 2, 1)
    lane = lax.broadcasted_iota(jnp.int32, blk.shape, 1)
    first_half = (lane % d) < (d // 2)
    return jnp.where(first_half, pltpu.roll(blk, LANES - d // 2, 1), pltpu.roll(blk, d // 2, 1))


def _rmsnorm_rows(x, g):
    inv = lax.rsqrt(jnp.mean(x * x, axis=-1, keepdims=True) + NORM_EPS)
    return x * inv * g


def _inproj_rope_kernel(x_ref, g_ref, w_ref, cos_ref, sin_ref, cosi_ref, sini_ref,
                        qa_ref, ka_ref, va_ref, qi_ref, kiwi_ref, kit_ref=None):
    xn = _rmsnorm_rows(x_ref[...], g_ref[...]).astype(BF16)
    y = _dot(xn, w_ref[...])
    cos, sin = cos_ref[...], sin_ref[...]
    for h in range(N_HEADS_A):
        blk = y[:, h * D_HEAD:(h + 1) * D_HEAD]
        qa_ref[:, h * D_HEAD:(h + 1) * D_HEAD] = (blk * cos + _swap_halves(blk, D_HEAD) * sin).astype(BF16)
    ka = y[:, A_WIDTH:A_WIDTH + D_HEAD]
    ka_ref[...] = ka * cos + _swap_halves(ka, D_HEAD) * sin
    va_ref[...] = y[:, A_WIDTH + D_HEAD:A_WIDTH + 2 * D_HEAD]
    cosi, sini = cosi_ref[...], sini_ref[...]
    base = A_WIDTH + 2 * D_HEAD
    for j in range(IDX_WIDTH // LANES):
        blk = y[:, base + j * LANES:base + (j + 1) * LANES]
        qi_ref[:, j * LANES:(j + 1) * LANES] = (blk * cosi + _swap_halves(blk, D_IDX) * sini).astype(BF16)
    blk = y[:, base + IDX_WIDTH:base + IDX_WIDTH + LANES]
    lane = lax.broadcasted_iota(jnp.int32, blk.shape, 1)
    is_k = lane < D_IDX
    kiwi = blk * jnp.where(is_k, cosi, 1.0) + _swap_halves(blk, D_IDX) * jnp.where(is_k, sini, 0.0)
    kiwi_ref[...] = kiwi
    if kit_ref is not None:
        kit_ref[0] = kiwi.T[:D_IDX, :]


def _inproj_rope(x, g, w_a, tabs, tm, n_pos_blocks, transposed_k_idx):
    rows = x.shape[0]
    wa_cols = w_a.shape[1]
    cos, sin, cosi, sini = tabs
    row_spec = lambda width: pl.BlockSpec((tm, width), lambda i: (i, 0))
    tab_spec = pl.BlockSpec((tm, LANES), lambda i: (i % n_pos_blocks, 0))
    out_specs = [row_spec(A_WIDTH), row_spec(D_HEAD), row_spec(D_HEAD), row_spec(IDX_WIDTH), row_spec(LANES)]
    out_shape = [jax.ShapeDtypeStruct((rows, A_WIDTH), BF16),
                 jax.ShapeDtypeStruct((rows, D_HEAD), F32),
                 jax.ShapeDtypeStruct((rows, D_HEAD), F32),
                 jax.ShapeDtypeStruct((rows, IDX_WIDTH), BF16),
                 jax.ShapeDtypeStruct((rows, LANES), F32)]
    if transposed_k_idx:
        out_specs.append(pl.BlockSpec((1, D_IDX, tm), lambda i: (i // n_pos_blocks, 0, i % n_pos_blocks)))
        out_shape.append(jax.ShapeDtypeStruct((rows // (n_pos_blocks * tm), D_IDX, n_pos_blocks * tm), F32))
    return pl.pallas_call(
        _inproj_rope_kernel,
        grid=(rows // tm,),
        in_specs=[row_spec(D_MODEL),
                  pl.BlockSpec((1, D_MODEL), lambda i: (0, 0)),
                  pl.BlockSpec((D_MODEL, wa_cols), lambda i: (0, 0)),
                  tab_spec, tab_spec, tab_spec, tab_spec],
        out_specs=out_specs,
        out_shape=out_shape,
        compiler_params=_params(("arbitrary",)),
        name="inproj_rope",
    )(x, g, w_a, cos, sin, cosi, sini)


N_GATE_TILES = 2 * D_MODEL // B_WIDTH


def _inproj_plain_kernel(x_ref, g_ref, w_ref, kb_ref, vb_ref, qs_ref, xn_ref):
    j = pl.program_id(1)

    @pl.when(j == 0)
    def _():
        xn_ref[...] = _rmsnorm_rows(x_ref[...], g_ref[...]).astype(BF16)

    product = lambda: _dot(xn_ref[...], w_ref[...])

    @pl.when(j == 0)
    def _():
        qs_ref[...] = product().astype(BF16)

    @pl.when(j == 1)
    def _():
        kb_ref[...] = product()

    @pl.when(j == 2)
    def _():
        vb_ref[...] = product()

    @pl.when(j >= 3)
    def _():
        qs_ref[...] = (1.0 / (1.0 + jnp.exp(-product()))).astype(BF16)


def _inproj_plain(x, g, w_bg, tm):
    rows = x.shape[0]
    tn = B_WIDTH
    n_col = w_bg.shape[1] // tn
    assert n_col == 3 + N_GATE_TILES
    fixed = pl.BlockSpec((tm, tn), lambda i, j: (i, 0))
    qs_map = lambda i, j: (i, jnp.where(j < 3, N_GATE_TILES, j - 3))
    return pl.pallas_call(
        _inproj_plain_kernel,
        grid=(rows // tm, n_col),
        in_specs=[pl.BlockSpec((tm, D_MODEL), lambda i, j: (i, 0)),
                  pl.BlockSpec((1, D_MODEL), lambda i, j: (0, 0)),
                  pl.BlockSpec((D_MODEL, tn), lambda i, j: (0, j))],
        out_specs=[fixed, fixed, pl.BlockSpec((tm, tn), qs_map)],
        out_shape=[jax.ShapeDtypeStruct((rows, tn), F32), jax.ShapeDtypeStruct((rows, tn), F32),
                   jax.ShapeDtypeStruct((rows, (N_GATE_TILES + 1) * tn), BF16)],
        scratch_shapes=[pltpu.VMEM((tm, D_MODEL), BF16)],
        compiler_params=_params(("arbitrary", "arbitrary")),
        name="inproj_plain",
    )(x, g, w_bg)


def _sortable(x):
    b = lax.bitcast_convert_type(x, jnp.int32)
    return b ^ ((b >> 31) & jnp.int32(0x7FFFFFFF))


def _select_topk(keys_ref, neg_ref, valid_fn, k_top):
    rows, width = keys_ref.shape
    n_chunks = width // LANES
    ones = jnp.ones((LANES, LANES), BF16)
    lane = lax.broadcasted_iota(jnp.int32, (rows, LANES), 1)

    def lane_total(acc):
        return _dot(acc.astype(BF16), ones)

    def count(pred):
        acc = jnp.zeros((rows, LANES), F32)
        for c in range(n_chunks):
            acc = acc + jnp.where(pred(keys_ref[:, c * LANES:(c + 1) * LANES], c), 1.0, 0.0)
        return lane_total(acc)

    def value_step(i, ans):
        cand_u = ans | lax.shift_left(jnp.int32(1), 31 - i)
        cand = cand_u ^ jnp.int32(INT_MIN)
        cnt = count(lambda k, c: k >= cand)
        return jnp.where(cnt >= k_top, cand_u, ans)

    thr = lax.fori_loop(0, 32, value_step, jnp.zeros((rows, LANES), jnp.int32)) ^ jnp.int32(INT_MIN)
    n_gt = count(lambda k, c: k > thr)
    n_ge = count(lambda k, c: k >= thr)
    need = k_top - n_gt

    n_bits = (width - 1).bit_length()

    def index_step(i, cst):
        cand = cst | lax.shift_left(jnp.int32(1), n_bits - 1 - i)
        cnt = count(lambda k, c: (k == thr) & (lane + c * LANES < cand))
        return jnp.where(cnt < need, cand, cst)

    cut = lax.cond(jnp.max(n_ge) > k_top,
                   lambda: lax.fori_loop(0, n_bits, index_step, jnp.zeros((rows, LANES), jnp.int32)),
                   lambda: jnp.full((rows, LANES), width, jnp.int32))

    for c in range(n_chunks):
        k = keys_ref[:, c * LANES:(c + 1) * LANES]
        tie_keep = jnp.where(lane + c * LANES <= cut, 0.0, NEG_INF)
        keep = jnp.where(k > thr, 0.0, jnp.where(k == thr, tie_keep, NEG_INF))
        neg_ref[:, c * LANES:(c + 1) * LANES] = jnp.where(valid_fn(c), keep, NEG_INF)


def _dsa_prompt_kernel(qi_ref, wq_ref, kk_ref, qa_ref, ka_ref, va_ref, *rest, k_top, q_start):
    _, o_ref, keys_ref, neg_ref = rest
    tq, width = keys_ref.shape

    w = wq_ref[0, :, D_IDX:D_IDX + N_IDX_HEADS] * ((N_IDX_HEADS * D_IDX) ** -0.5)
    k_idx = kk_ref[0, :, :D_IDX].astype(BF16)
    score = jnp.zeros((tq, width), F32)
    for h in range(N_IDX_HEADS):
        rel = jnp.maximum(_dot_nt(qi_ref[0, :, h * D_IDX:(h + 1) * D_IDX], k_idx), 0.0)
        score = score + rel * w[:, h:h + 1]

    t_pos = q_start + lax.broadcasted_iota(jnp.int32, (tq, LANES), 0)
    lane = lax.broadcasted_iota(jnp.int32, (tq, LANES), 1)
    causal = lambda c: lane + c * LANES <= t_pos
    s_pos = lax.broadcasted_iota(jnp.int32, (tq, width), 1)
    q_pos = q_start + lax.broadcasted_iota(jnp.int32, (tq, width), 0)
    keys_ref[...] = _sortable(jnp.where(s_pos <= q_pos, score, NEG_INF))
    _select_topk(keys_ref, neg_ref, causal, k_top)

    k_att = ka_ref[0].astype(BF16)
    v_att = va_ref[0].astype(BF16)
    exp2_scale = (D_HEAD ** -0.5) * math.log2(math.e)
    for h in range(N_HEADS_A):
        s = _dot_nt(qa_ref[0, :, h * D_HEAD:(h + 1) * D_HEAD], k_att) + neg_ref[...]
        m = jnp.max(s, axis=-1, keepdims=True)
        e = jnp.exp2((s - m) * exp2_scale)
        l = jnp.sum(e, axis=-1, keepdims=True)
        o = _dot(e.astype(BF16), v_att) / l
        o_ref[0, :, h * D_HEAD:(h + 1) * D_HEAD] = o.astype(BF16)


def _dsa_prompt(q_i, kiwi, q_a, k_a, v_a, batch, seq, tq):
    k_top = min(TOPK_MAX, seq // 4)
    out = jnp.zeros((batch, seq, A_WIDTH), BF16)
    for qb in range(seq // tq):
        width = (qb + 1) * tq
        q_spec = lambda w_: pl.BlockSpec((1, tq, w_), lambda b: (b, qb, 0))
        kv_spec = lambda w_: pl.BlockSpec((1, width, w_), lambda b: (b, 0, 0))
        out = pl.pallas_call(
            functools.partial(_dsa_prompt_kernel, k_top=k_top, q_start=qb * tq),
            grid=(batch,),
            in_specs=[q_spec(IDX_WIDTH), q_spec(LANES), kv_spec(LANES), q_spec(A_WIDTH), kv_spec(D_HEAD),
                      kv_spec(D_HEAD), pl.BlockSpec(memory_space=pl.ANY)],
            out_specs=q_spec(A_WIDTH),
            out_shape=jax.ShapeDtypeStruct((batch, seq, A_WIDTH), BF16),
            scratch_shapes=[pltpu.VMEM((tq, width), jnp.int32), pltpu.VMEM((tq, width), F32)],
            input_output_aliases={6: 0},
            compiler_params=_params(("arbitrary",)),
            name=f"dsa_prompt_q{qb}",
        )(q_i, kiwi, kiwi, q_a, k_a, v_a, out)
    return out


def _suffix_matrix(n, reps=1):
    j = lax.broadcasted_iota(jnp.int32, (n * reps, n * reps), 0)
    s = lax.broadcasted_iota(jnp.int32, (n * reps, n * reps), 1)
    same = (j % reps) == (s % reps)
    later = jnp.where(same & (j // reps > s // reps), 1.0, 0.0)
    total = jnp.where(same, 1.0, 0.0)
    return jnp.concatenate([later, total], axis=1).astype(BF16)


def _softplus(z):
    return jnp.maximum(z, 0.0) + jnp.log(1.0 + jnp.exp(-jnp.abs(z)))


def _sb_prompt_kernel(q_ref, k_ref, v_ref, u_ref, o_ref, k16, v16, acc_ref, later_ref):
    qb = pl.program_id(2)
    tq = q_ref.shape[0]
    sub = LANES
    n_sub = tq // sub

    @pl.when(qb == 0)
    def _():
        k16[...] = k_ref[...].astype(BF16)
        v16[...] = v_ref[...].astype(BF16)

    u = u_ref[...]
    n_heads = q_ref.shape[1] // D_HEAD

    def block(kb, diagonal):
        start = pl.multiple_of(kb * tq, tq)
        for hh in range(n_heads):
            cols = slice(hh * D_HEAD, (hh + 1) * D_HEAD)
            z = _dot_nt(q_ref[:, cols], k16[pl.ds(start, tq), cols]) * (D_HEAD ** -0.5)
            sp = _softplus(z)
            log_beta = z - sp
            if diagonal:
                before = (lax.broadcasted_iota(jnp.int32, (tq, tq), 1)
                          < lax.broadcasted_iota(jnp.int32, (tq, tq), 0))
                sp = jnp.where(before, sp, 0.0)
            sp16 = sp.astype(BF16)
            later = later_ref[:, cols]
            args = [None] * n_sub
            for j in reversed(range(n_sub)):
                cum = _dot(sp16[:, j * sub:(j + 1) * sub], u)
                args[j] = log_beta[:, j * sub:(j + 1) * sub] - cum[:, :sub] - later
                later = later + cum[:, sub:]
            a = jnp.exp(jnp.concatenate(args, axis=1))
            if diagonal:
                a = jnp.where(before, a, 0.0)
            acc_ref[:, cols] += _dot(a.astype(BF16), v16[pl.ds(start, tq), cols])
            later_ref[:, cols] = later

    acc_ref[...] = jnp.zeros_like(acc_ref)
    later_ref[...] = jnp.zeros_like(later_ref)
    block(qb, True)

    def body(it, carry):
        block(qb - 1 - it, False)
        return carry

    lax.fori_loop(0, qb, body, 0)
    o_ref[...] = acc_ref[...].astype(BF16)


def _sb_prompt(qs, k_b, v_b, batch, seq, tq):
    rows = batch * seq
    nq = seq // tq
    hw = SB_HEADS_PER_STEP * D_HEAD
    q_col = (qs.shape[1] - B_WIDTH) // hw
    q_spec = pl.BlockSpec((tq, hw), lambda b, h, q: (b * nq + q, q_col + h))
    kv_spec = pl.BlockSpec((seq, hw), lambda b, h, q: (b, h))
    return pl.pallas_call(
        _sb_prompt_kernel,
        grid=(batch, N_HEADS_B // SB_HEADS_PER_STEP, nq),
        in_specs=[q_spec, kv_spec, kv_spec, pl.BlockSpec((LANES, 2 * LANES), lambda b, h, q: (0, 0))],
        out_specs=pl.BlockSpec((tq, hw), lambda b, h, q: (b * nq + q, h)),
        out_shape=jax.ShapeDtypeStruct((rows, B_WIDTH), BF16),
        scratch_shapes=[pltpu.VMEM((seq, hw), BF16), pltpu.VMEM((seq, hw), BF16),
                        pltpu.VMEM((tq, hw), F32), pltpu.VMEM((tq, hw), F32)],
        compiler_params=_params(("arbitrary", "arbitrary", "arbitrary")),
        name="sb_prompt",
    )(qs, k_b, v_b, _suffix_matrix(LANES))


def _page_copies(pt_ref, batch_idx, first_page, n_pages, pools, bufs, slot, sem, rows_per_page, n_logical):
    copies = []
    for p in range(n_pages):
        phys = pt_ref[batch_idx * n_logical + first_page + p]
        for pool, buf in zip(pools, bufs):
            copies.append(pltpu.make_async_copy(
                pool.at[phys], buf.at[slot, pl.ds(p * rows_per_page, rows_per_page)], sem.at[slot]))
    return copies


def _paged_double_buffer(pt_ref, pools, bufs, sem, n_pages, rows_per_page):
    b = pl.program_id(0)
    slot = b % 2
    copies = lambda bb, ss: _page_copies(pt_ref, bb, 0, n_pages, pools, bufs, ss, sem, rows_per_page, n_pages)

    @pl.when(b == 0)
    def _():
        for c in copies(0, 0):
            c.start()

    @pl.when(b + 1 < pl.num_programs(0))
    def _():
        for c in copies(b + 1, 1 - slot):
            c.start()

    for c in copies(b, slot):
        c.wait()
    return slot


def _new_key_visible(n_q, n_new, rows):
    t_idx = lax.broadcasted_iota(jnp.int32, (rows, LANES), 0) % n_q
    lane = lax.broadcasted_iota(jnp.int32, (rows, LANES), 1)
    return (lane <= t_idx) & (lane < n_new)


def _idx_scores_sample_kernel(pt_ref, qi_ref, w_ref, kin_ref, idx_pool, s_ref, idx_buf, sem, *, n_pages, n_new):
    slot = _paged_double_buffer(pt_ref, (idx_pool,), (idx_buf,), sem, n_pages, PAGE_SIZE)
    past = n_pages * PAGE_SIZE
    q = qi_ref[0]
    w = w_ref[0]
    n_q = q.shape[0] // N_IDX_HEADS

    def head_sum(keys16):
        rel = jnp.maximum(_dot_nt(q, keys16), 0.0) * w
        return functools.reduce(lambda a, c: a + c, [rel[h * n_q:(h + 1) * n_q] for h in range(N_IDX_HEADS)])

    s_ref[0, :, :past] = head_sum(idx_buf[slot].astype(BF16))
    s_new = head_sum(kin_ref[0].astype(BF16))
    s_ref[0, :, past:] = jnp.where(_new_key_visible(n_q, n_new, n_q), s_new, NEG_INF)


def _topk_mask_kernel(s_ref, neg_ref, keys_ref, *, k_top, n_q, n_new):
    rows, width = keys_ref.shape
    keys_ref[...] = _sortable(s_ref[...])
    visible = _new_key_visible(n_q, n_new, rows)
    last = width // LANES - 1
    _select_topk(keys_ref, neg_ref, lambda c: visible if c == last else jnp.full((rows, LANES), True), k_top)


def _dsa_sample_attend_kernel(pt_ref, qa_ref, neg_ref, kan_ref, van_ref, k_pool, v_pool, o_ref,
                              k_buf, v_buf, sem, *, n_pages):
    slot = _paged_double_buffer(pt_ref, (k_pool, v_pool), (k_buf, v_buf), sem, n_pages, PAGE_SIZE)
    past = n_pages * PAGE_SIZE
    k_att = k_buf[slot].astype(BF16)
    v_att = v_buf[slot].astype(BF16)
    q = qa_ref[0]
    neg = jnp.concatenate([neg_ref[0]] * N_HEADS_A, axis=0)
    s_past = _dot_nt(q, k_att) + neg[:, :past]
    s_new = _dot_nt(q, kan_ref[0].astype(BF16)) + neg[:, past:]
    m = jnp.maximum(jnp.max(s_past, axis=-1, keepdims=True), jnp.max(s_new, axis=-1, keepdims=True))
    exp2_scale = (D_HEAD ** -0.5) * math.log2(math.e)
    e_past = jnp.exp2((s_past - m) * exp2_scale)
    e_new = jnp.exp2((s_new - m) * exp2_scale)
    l = jnp.sum(e_past, axis=-1, keepdims=True) + jnp.sum(e_new, axis=-1, keepdims=True)
    o = _dot(e_past.astype(BF16), v_att) + _dot(e_new.astype(BF16), van_ref[0].astype(BF16))
    o_ref[0] = o / l


def _dsa_sample(page_table, q_i, w_i, k_i_new, q_a, k_a_new, v_a_new, idx_pool, k_pool, v_pool):
    batch, n_pages = page_table.shape
    n_q = q_a.shape[1] // N_HEADS_A
    past = n_pages * PAGE_SIZE
    width = past + LANES
    k_top = min(TOPK_MAX, (past + n_q) // 4)
    pt = page_table.reshape(-1)
    blk = lambda shape: pl.BlockSpec((1,) + shape, lambda b, pt_: (b,) + (0,) * len(shape))
    hbm = pl.BlockSpec(memory_space=pl.ANY)

    scores = pl.pallas_call(
        functools.partial(_idx_scores_sample_kernel, n_pages=n_pages, n_new=n_q),
        grid_spec=pltpu.PrefetchScalarGridSpec(
            num_scalar_prefetch=1, grid=(batch,),
            in_specs=[blk((N_IDX_HEADS * n_q, D_IDX)), blk((N_IDX_HEADS * n_q, 1)), blk((LANES, D_IDX)), hbm],
            out_specs=blk((n_q, width)),
            scratch_shapes=[pltpu.VMEM((2, past, D_IDX), F32), pltpu.SemaphoreType.DMA((2,))]),
        out_shape=jax.ShapeDtypeStruct((batch, n_q, width), F32),
        compiler_params=_params(("arbitrary",)),
        name="idx_scores_sample",
    )(pt, q_i, w_i, k_i_new, idx_pool)

    rows = batch * n_q
    tr = min(rows, LANES)
    neg = pl.pallas_call(
        functools.partial(_topk_mask_kernel, k_top=k_top, n_q=n_q, n_new=n_q),
        grid=(rows // tr,),
        in_specs=[pl.BlockSpec((tr, width), lambda i: (i, 0))],
        out_specs=pl.BlockSpec((tr, width), lambda i: (i, 0)),
        out_shape=jax.ShapeDtypeStruct((rows, width), F32),
        scratch_shapes=[pltpu.VMEM((tr, width), jnp.int32)],
        compiler_params=_params(("arbitrary",)),
        name="topk_mask_sample",
    )(scores.reshape(rows, width))

    return pl.pallas_call(
        functools.partial(_dsa_sample_attend_kernel, n_pages=n_pages),
        grid_spec=pltpu.PrefetchScalarGridSpec(
            num_scalar_prefetch=1, grid=(batch,),
            in_specs=[blk((N_HEADS_A * n_q, D_HEAD)), blk((n_q, width)), blk((LANES, D_HEAD)), blk((LANES, D_HEAD)),
                      hbm, hbm],
            out_specs=blk((N_HEADS_A * n_q, D_HEAD)),
            scratch_shapes=[pltpu.VMEM((2, past, D_HEAD), F32), pltpu.VMEM((2, past, D_HEAD), F32),
                            pltpu.SemaphoreType.DMA((2,))]),
        out_shape=jax.ShapeDtypeStruct((batch, N_HEADS_A * n_q, D_HEAD), F32),
        compiler_params=_params(("arbitrary",)),
        name="dsa_sample_attend",
    )(pt, q_a, neg.reshape(batch, n_q, width), k_a_new, v_a_new, k_pool, v_pool)


SB_CHUNK_PAGES = 8
SB_SLOTS = 3
SB_BLOCK_KEYS = 32


def _sb_sample_kernel(pt_ref, q_ref, kn_ref, vn_ref, u_ref, k_pool, v_pool, o_ref,
                      k_buf, v_buf, acc_ref, later_ref, sem, *, n_pages, n_new, total_steps):
    b = pl.program_id(0)
    j = pl.program_id(1)
    n_chunks = pl.num_programs(1)
    step = b * n_chunks + j
    n_steps = total_steps
    n_slots = k_buf.shape[0]
    slot = step % n_slots
    heads = N_HEADS_B
    rows_per_page = PAGE_SIZE * heads
    blk_lanes = SB_BLOCK_KEYS * heads

    def copies(st, ss):
        bb = st // n_chunks
        chunk = n_chunks - 1 - st % n_chunks
        return _page_copies(pt_ref, bb, chunk * SB_CHUNK_PAGES, SB_CHUNK_PAGES, (k_pool, v_pool),
                            (k_buf, v_buf), ss, sem, rows_per_page, n_pages)

    @pl.when(step == 0)
    def _():
        for ahead in range(min(n_slots - 1, total_steps)):
            for c in copies(ahead, ahead):
                c.start()

    @pl.when(step + n_slots - 1 < n_steps)
    def _():
        for c in copies(step + n_slots - 1, (step + n_slots - 1) % n_slots):
            c.start()

    q = q_ref[0]
    n_q = q.shape[0] // heads
    u = u_ref[...]
    scale = D_HEAD ** -0.5

    def head_diag(zt):
        lane_head = lax.broadcasted_iota(jnp.int32, (n_q, zt.shape[1]), 1) % heads
        out = jnp.zeros((n_q, zt.shape[1]), F32)
        for h in range(heads):
            out = out + jnp.where(lane_head == h, zt[h * n_q:(h + 1) * n_q, :], 0.0)
        return out

    def head_spread(a):
        lane_head = lax.broadcasted_iota(jnp.int32, a.shape, 1) % heads
        return jnp.concatenate([jnp.where(lane_head == h, a, 0.0) for h in range(heads)], axis=0).astype(BF16)

    def attend(k16, v16, before, later):
        n = k16.shape[0]
        z = head_diag(_dot_nt(q, k16) * scale)
        sp = _softplus(z)
        log_beta = z - sp
        sp16 = jnp.where(before, sp, 0.0).astype(BF16)
        pieces = []
        for blk in reversed(range(n // blk_lanes)):
            sl = slice(blk * blk_lanes, (blk + 1) * blk_lanes)
            cum = _dot(sp16[:, sl], u)
            pieces.append(log_beta[:, sl] - cum[:, :blk_lanes] - later)
            later = later + cum[:, blk_lanes:]
        a = jnp.exp(jnp.concatenate(pieces[::-1], axis=1))
        a = jnp.where(before, a, 0.0)
        return _dot(head_spread(a), v16), later

    @pl.when(j == 0)
    def _():
        t_idx = lax.broadcasted_iota(jnp.int32, (n_q, blk_lanes), 0)
        key = lax.broadcasted_iota(jnp.int32, (n_q, blk_lanes), 1) // heads
        before = (key < t_idx) & (key < n_new)
        out, later = attend(kn_ref[0].astype(BF16), vn_ref[0].astype(BF16), before,
                            jnp.zeros((n_q, blk_lanes), F32))
        acc_ref[...] = out
        later_ref[...] = later

    for c in copies(step, slot):
        c.wait()

    n_rows = SB_CHUNK_PAGES * rows_per_page
    out, later = attend(k_buf[slot].astype(BF16), v_buf[slot].astype(BF16),
                        jnp.full((n_q, n_rows), True), later_ref[...])
    acc_ref[...] += out
    later_ref[...] = later

    @pl.when(j == n_chunks - 1)
    def _():
        o_ref[0] = acc_ref[...]


def _sb_sample(page_table, q, k_new, v_new, k_pool, v_pool):
    batch, n_pages = page_table.shape
    rows_q = q.shape[1]
    n_q = rows_q // N_HEADS_B
    blk_lanes = SB_BLOCK_KEYS * N_HEADS_B
    n_chunks = n_pages // SB_CHUNK_PAGES
    chunk_rows = SB_CHUNK_PAGES * PAGE_SIZE * N_HEADS_B
    blk = lambda shape: pl.BlockSpec((1,) + shape, lambda b, j, pt: (b,) + (0,) * len(shape))
    hbm = pl.BlockSpec(memory_space=pl.ANY)
    grid_spec = pltpu.PrefetchScalarGridSpec(
        num_scalar_prefetch=1,
        grid=(batch, n_chunks),
        in_specs=[blk((rows_q, D_HEAD)), blk((blk_lanes, D_HEAD)), blk((blk_lanes, D_HEAD)),
                  pl.BlockSpec((blk_lanes, 2 * blk_lanes), lambda b, j, pt: (0, 0)), hbm, hbm],
        out_specs=blk((rows_q, D_HEAD)),
        scratch_shapes=[pltpu.VMEM((SB_SLOTS, chunk_rows, D_HEAD), F32), pltpu.VMEM((SB_SLOTS, chunk_rows, D_HEAD), F32),
                        pltpu.VMEM((rows_q, D_HEAD), F32), pltpu.VMEM((n_q, blk_lanes), F32),
                        pltpu.SemaphoreType.DMA((SB_SLOTS,))],
    )
    return pl.pallas_call(
        functools.partial(_sb_sample_kernel, n_pages=n_pages, n_new=n_q, total_steps=batch * n_chunks),
        grid_spec=grid_spec,
        out_shape=jax.ShapeDtypeStruct((batch, rows_q, D_HEAD), F32),
        compiler_params=_params(("arbitrary", "arbitrary")),
        name="sb_sample",
    )(page_table.reshape(-1), q, k_new, v_new, _suffix_matrix(SB_BLOCK_KEYS, N_HEADS_B), k_pool, v_pool)


ROW_TILES = D_MODEL // LANES


def _store_row_tiles(ref, val):
    for j in range(ROW_TILES):
        ref[:, j, :] = val[:, j * LANES:(j + 1) * LANES]


def _load_row_tiles(ref):
    return jnp.concatenate([ref[:, j, :] for j in range(ROW_TILES)], axis=1)


def _merge_kernel(*refs, n_prompt_tiles):
    prompt, sample = refs[0:4], refs[4:8]
    wua_ref, wub_ref, wo_ref, gf_ref, wr_hi_ref, wr_lo_ref, br_ref, h_ref, hn_ref, lg_ref = refs[8:]

    def tile(x_ref, oa_ref, ob_ref, sig_ref):
        sig_a = sig_ref[:, :D_MODEL].astype(F32)
        sig_b = sig_ref[:, D_MODEL:].astype(F32)
        merged = sig_a * _dot(oa_ref[...], wua_ref[...]) + sig_b * _dot(ob_ref[...], wub_ref[...])
        h = x_ref[...] + _dot(merged.astype(BF16), wo_ref[...])
        h_ref[...] = h
        hn = _rmsnorm_rows(h, gf_ref[...])
        _store_row_tiles(hn_ref, hn)
        hn_hi = hn.astype(BF16)
        hn_lo = (hn - hn_hi.astype(F32)).astype(BF16)
        lg_ref[...] = (_dot(hn_hi, wr_hi_ref[...]) + _dot(hn_lo, wr_hi_ref[...])
                       + _dot(hn_hi, wr_lo_ref[...]) + br_ref[...])

    i = pl.program_id(0)

    @pl.when(i < n_prompt_tiles)
    def _():
        tile(*prompt)

    @pl.when(i >= n_prompt_tiles)
    def _():
        tile(*sample)


def _merge(prompt, sample, w_up_a, w_up_b, w_out, g_ffn, wr_hi, wr_lo, b_r, tm):
    n_p, n_s = prompt[0].shape[0] // tm, sample[0].shape[0] // tm
    assert prompt[0].shape[0] == n_p * tm and sample[0].shape[0] == n_s * tm
    rows = (n_p + n_s) * tm
    widths = (D_MODEL, A_WIDTH, B_WIDTH, 2 * D_MODEL)
    p_spec = lambda width: pl.BlockSpec((tm, width), lambda i: (jnp.minimum(i, n_p - 1), 0))
    s_spec = lambda width: pl.BlockSpec((tm, width), lambda i: (jnp.maximum(i - n_p, 0), 0))
    row_spec = lambda width: pl.BlockSpec((tm, width), lambda i: (i, 0))
    const = lambda shape: pl.BlockSpec(shape, lambda i: (0, 0), pipeline_mode=pl.Buffered(1))
    return pl.pallas_call(
        functools.partial(_merge_kernel, n_prompt_tiles=n_p),
        grid=(n_p + n_s,),
        in_specs=[p_spec(w_) for w_ in widths] + [s_spec(w_) for w_ in widths] + [
                  const((A_WIDTH, D_MODEL)), const((B_WIDTH, D_MODEL)), const((D_MODEL, D_MODEL)),
                  const((1, D_MODEL)), const((D_MODEL, ROUTER_LANES)), const((D_MODEL, ROUTER_LANES)),
                  const((1, ROUTER_LANES))],
        out_specs=[row_spec(D_MODEL), pl.BlockSpec((tm, ROW_TILES, LANES), lambda i: (i, 0, 0)),
                   row_spec(ROUTER_LANES)],
        out_shape=[jax.ShapeDtypeStruct((rows, D_MODEL), F32),
                   jax.ShapeDtypeStruct((rows, ROW_TILES, LANES), F32),
                   jax.ShapeDtypeStruct((rows, ROUTER_LANES), F32)],
        compiler_params=_params(("arbitrary",)),
        name="merge",
    )(*prompt, *sample, w_up_a, w_up_b, w_out, g_ffn, wr_hi, wr_lo, b_r)


def _router_kernel(lg_ref, eid_ref, gate_ref, *, chunk):
    for c0 in range(0, lg_ref.shape[0], chunk):
        _route_chunk(lg_ref, eid_ref, gate_ref, c0, chunk)


def _route_chunk(lg_ref, eid_ref, gate_ref, c0, chunk):
    lg = lg_ref[c0:c0 + chunk, :].T
    row = lambda r: lg[r:r + 1, :]
    groups = [row(g) for g in range(N_GROUPS)]
    g_max = functools.reduce(jnp.maximum, groups)
    g_sel = jnp.full(g_max.shape, N_GROUPS - 1, jnp.int32)
    for g in reversed(range(N_GROUPS)):
        g_sel = jnp.where(groups[g] == g_max, g, g_sel)
    p_group = 1.0 / functools.reduce(lambda a, c: a + c, [jnp.exp(v - g_max) for v in groups])
    inside = []
    for e in range(EXPERTS_PER_GROUP):
        v = jnp.zeros_like(g_max)
        for g in range(N_GROUPS):
            v = jnp.where(g_sel == g, row(N_GROUPS + g * EXPERTS_PER_GROUP + e), v)
        inside.append(v)

    def top1(vals, skip):
        best = functools.reduce(jnp.maximum, [jnp.where(skip == e, NEG_INF, v) for e, v in enumerate(vals)])
        idx = jnp.full(best.shape, EXPERTS_PER_GROUP - 1, jnp.int32)
        for e in reversed(range(EXPERTS_PER_GROUP)):
            idx = jnp.where((vals[e] == best) & (skip != e), e, idx)
        return best, idx

    t1, i1 = top1(inside, jnp.full(g_max.shape, -1, jnp.int32))
    t2, i2 = top1(inside, i1)
    r = jnp.exp(t2 - t1)
    for k, ids in enumerate((g_sel * EXPERTS_PER_GROUP + i1, g_sel * EXPERTS_PER_GROUP + i2)):
        for c in range(chunk // LANES):
            row0 = c0 // LANES + c
            eid_ref[k, row0:row0 + 1, :] = ids[:, c * LANES:(c + 1) * LANES]
    gates = jnp.concatenate([p_group * (1.0 / (1.0 + r)), p_group * (r / (1.0 + r)),
                             jnp.zeros((ROUTER_LANES - TOP_K_EXPERTS, chunk), F32)], axis=0)
    gate_ref[c0:c0 + chunk, :] = gates.T


def _router(logits):
    n_tok = logits.shape[0]
    assert n_tok % LANES == 0
    chunk = max(c for c in range(LANES, 2048 + 1, LANES) if n_tok % c == 0)
    return pl.pallas_call(
        functools.partial(_router_kernel, chunk=chunk),
        out_shape=[jax.ShapeDtypeStruct((TOP_K_EXPERTS, n_tok // LANES, LANES), jnp.int32),
                   jax.ShapeDtypeStruct((n_tok, ROUTER_LANES), F32)],
        compiler_params=pltpu.CompilerParams(vmem_limit_bytes=VMEM_LIMIT),
        name="router",
    )(logits)


def _experts_kernel(blk_e_ref, blk_off_ref, blk_cnt_ref, nact_ref, order_ref,
                    hn_hbm, wg_ref, wu_ref, wd_ref, ys_hbm,
                    x_buf, y_buf, wg16, wu16, wd16, in_sem, out_sem, *, n_tok):
    i = pl.program_id(0)
    n_act = nact_ref[0]
    rows = x_buf.shape[1]
    slot = i % 2
    n_pairs = TOP_K_EXPERTS * n_tok

    def gathers(blk, ss):
        off = blk_off_ref[blk]
        copies = []
        for r in range(rows):
            p = order_ref[off + r]
            tok = jnp.where(p >= n_tok, p - n_tok, p)
            copies.append(pltpu.make_async_copy(hn_hbm.at[tok], x_buf.at[ss, r], in_sem.at[ss]))
        return copies

    def scatters(blk, ss):
        off = blk_off_ref[blk]
        cnt = blk_cnt_ref[blk]
        copies = []
        for r in range(rows):
            dst = jnp.where(r < cnt, order_ref[off + r], n_pairs + ss * rows + r)
            copies.append(pltpu.make_async_copy(y_buf.at[ss, r], ys_hbm.at[dst], out_sem.at[ss]))
        return copies

    @pl.when(i == 0)
    def _():
        for c in gathers(0, 0):
            c.start()
        y_buf[1] = jnp.zeros(y_buf.shape[1:], F32)
        for ss in range(2):
            spare = pltpu.make_async_copy(y_buf.at[1], ys_hbm.at[pl.ds(n_pairs + ss * rows, rows)], out_sem.at[1])
            spare.start()
            spare.wait()

    @pl.when((i >= 2) & (i < n_act))
    def _():
        for c in scatters(i - 2, slot):
            c.wait()

    first_of_expert = (i == 0) | (blk_e_ref[i] != blk_e_ref[jnp.maximum(i - 1, 0)])

    @pl.when((i < n_act) & first_of_expert)
    def _():
        wg16[...] = wg_ref[0].astype(BF16)
        wu16[...] = wu_ref[0].astype(BF16)
        wd16[...] = wd_ref[0].astype(BF16)

    @pl.when(i < n_act)
    def _():
        for c in gathers(i, slot):
            c.wait()
        for r, c in enumerate(gathers(jnp.minimum(i + 1, n_act - 1), 1 - slot)):
            c.start(priority=r % 2)
        xb = _load_row_tiles(x_buf.at[slot]).astype(BF16)
        g = _dot(xb, wg16[...])
        u = _dot(xb, wu16[...])
        act = (g * (1.0 / (1.0 + jnp.exp(-g))) * u).astype(BF16)
        _store_row_tiles(y_buf.at[slot], _dot(act, wd16[...]))
        for r, c in enumerate(scatters(i, slot)):
            c.start(priority=r % 2)

    @pl.when(i == n_act - 1)
    def _():
        for c in gathers(i, 1 - slot):
            c.wait()
        for c in scatters(i, slot):
            c.wait()

    @pl.when((i == n_act - 1) & (i >= 1))
    def _():
        for c in scatters(i - 1, 1 - slot):
            c.wait()


def _experts(blk_e, blk_off, blk_cnt, n_act, order, hn, w_gate, w_up, w_down):
    n_blocks = blk_e.shape[0]
    n_tok = hn.shape[0]
    rows = EXPERT_ROWS
    w_spec = lambda shape: pl.BlockSpec((1,) + shape, lambda i, be, *_: (be[i], 0, 0))
    row_buf = pltpu.VMEM((2, rows, ROW_TILES, LANES), F32)
    grid_spec = pltpu.PrefetchScalarGridSpec(
        num_scalar_prefetch=5,
        grid=(n_blocks,),
        in_specs=[pl.BlockSpec(memory_space=pl.ANY),
                  w_spec((D_MODEL, D_EXPERT)), w_spec((D_MODEL, D_EXPERT)), w_spec((D_EXPERT, D_MODEL))],
        out_specs=pl.BlockSpec(memory_space=pl.ANY),
        scratch_shapes=[row_buf, row_buf,
                        pltpu.VMEM((D_MODEL, D_EXPERT), BF16), pltpu.VMEM((D_MODEL, D_EXPERT), BF16),
                        pltpu.VMEM((D_EXPERT, D_MODEL), BF16),
                        pltpu.SemaphoreType.DMA((2,)), pltpu.SemaphoreType.DMA((2,))],
    )
    return pl.pallas_call(
        functools.partial(_experts_kernel, n_tok=n_tok),
        grid_spec=grid_spec,
        out_shape=jax.ShapeDtypeStruct((TOP_K_EXPERTS * n_tok + 2 * rows, ROW_TILES, LANES), F32),
        compiler_params=_params(("arbitrary",)),
        name="experts",
    )(blk_e, blk_off, blk_cnt, n_act, order, hn, w_gate, w_up, w_down)


def _combine_kernel(h_ref, gate_ref, gfin_ref, *rest, final_norm, n_prompt_tiles):
    yp_ref, ys_ref = rest[-2:]
    y = h_ref[...]
    for k, e_ref in enumerate(rest[:-2]):
        y = y + _load_row_tiles(e_ref) * gate_ref[:, k:k + 1]
    y = _rmsnorm_rows(y, gfin_ref[...]) if final_norm else y
    i = pl.program_id(0)

    @pl.when(i < n_prompt_tiles)
    def _():
        yp_ref[...] = y

    @pl.when(i >= n_prompt_tiles)
    def _():
        ys_ref[...] = y


def _combine(h, gates, g_final, ys, tt, n_prompt, final_norm):
    n_tok = h.shape[0]
    n_tiles, n_p = n_tok // tt, n_prompt // tt
    assert n_tok == n_tiles * tt and n_prompt == n_p * tt and n_tiles > n_p
    e_spec = lambda k: pl.BlockSpec((tt, ROW_TILES, LANES), lambda i: (k * n_tiles + i, 0, 0))
    return pl.pallas_call(
        functools.partial(_combine_kernel, final_norm=final_norm, n_prompt_tiles=n_p),
        grid=(n_tiles,),
        in_specs=[pl.BlockSpec((tt, D_MODEL), lambda i: (i, 0)),
                  pl.BlockSpec((tt, ROUTER_LANES), lambda i: (i, 0)),
                  pl.BlockSpec((1, D_MODEL), lambda i: (0, 0))] + [e_spec(k) for k in range(TOP_K_EXPERTS)],
        out_specs=[pl.BlockSpec((tt, D_MODEL), lambda i: (jnp.minimum(i, n_p - 1), 0)),
                   pl.BlockSpec((tt, D_MODEL), lambda i: (jnp.maximum(i - n_p, 0), 0))],
        out_shape=[jax.ShapeDtypeStruct((n_prompt, D_MODEL), F32),
                   jax.ShapeDtypeStruct((n_tok - n_prompt, D_MODEL), F32)],
        compiler_params=_params(("arbitrary",)),
        name="combine",
    )(h, gates, g_final, *([ys] * TOP_K_EXPERTS))


def _moe(h, hn, logits, g_final, w_gate, w_up, w_down, tt, n_prompt, final_norm):
    n_tok = h.shape[0]
    eid, gates = _router(logits)
    flat_e = eid.reshape(-1)
    nk = flat_e.shape[0]
    rows = EXPERT_ROWS
    _, order = lax.sort_key_val(flat_e, jnp.arange(nk, dtype=jnp.int32))
    experts = jnp.arange(N_EXPERTS, dtype=jnp.int32)
    counts = jnp.sum((flat_e[:, None] == experts[None, :]).astype(jnp.int32), axis=0)
    inclusive = lambda v: jnp.sum(jnp.where(experts[:, None] <= experts[None, :], v[:, None], 0), axis=0)
    start = inclusive(counts) - counts
    blocks_per_e = (counts + rows - 1) // rows
    blk_end = inclusive(blocks_per_e)
    blk_begin = blk_end - blocks_per_e
    n_blocks = -(-(nk + N_EXPERTS * (rows - 1)) // rows)
    blk = jnp.arange(n_blocks, dtype=jnp.int32)
    mine = (blk[:, None] >= blk_begin[None, :]) & (blk[:, None] < blk_end[None, :])
    pick = lambda v: jnp.sum(jnp.where(mine, v[None, :], 0), axis=1)
    first = (blk - pick(blk_begin)) * rows
    blk_cnt = jnp.clip(pick(counts) - first, 0, rows)
    blk_off = jnp.clip(pick(start) + first, 0, nk)
    order = jnp.concatenate([order, jnp.zeros((rows,), jnp.int32)])
    ys = _experts(pick(experts), blk_off, blk_cnt, blk_end[-1:], order, hn, w_gate, w_up, w_down)
    return _combine(h, gates, g_final, ys, tt, n_prompt, final_norm)


def _split_router(w_rg, b_rg, w_re, b_re):
    w = jnp.concatenate([w_rg, w_re], axis=1)
    w = jnp.pad(w, ((0, 0), (0, ROUTER_LANES - w.shape[1])))
    b = jnp.pad(jnp.concatenate([b_rg, b_re]), (0, ROUTER_LANES - N_GROUPS - N_EXPERTS)).reshape(1, ROUTER_LANES)
    w_hi = w.astype(BF16)
    w_lo = (w - w_hi.astype(F32)).astype(BF16)
    return w_hi, w_lo, b


def _layer_weights(l, g_mix, w_in, w_up_a, w_up_b, w_out, g_ffn, w_rg, b_rg, w_re, b_re, w_e_gate, w_e_up, w_e_down):
    n_rope = A_WIDTH + 2 * D_HEAD + IDX_WIDTH + D_IDX + N_IDX_HEADS
    w = w_in[l]
    w_a = jnp.pad(w[:, :n_rope], ((0, 0), (0, -n_rope % LANES))).astype(BF16)
    w_bg = w[:, n_rope:].astype(BF16)
    return dict(
        g_mix=g_mix[l].reshape(1, D_MODEL), w_a=w_a, w_bg=w_bg,
        w_up_a=w_up_a[l].astype(BF16), w_up_b=w_up_b[l].astype(BF16), w_out=w_out[l].astype(BF16),
        g_ffn=g_ffn[l].reshape(1, D_MODEL), router=_split_router(w_rg[l], b_rg[l], w_re[l], b_re[l]),
        w_e_gate=w_e_gate[l], w_e_up=w_e_up[l], w_e_down=w_e_down[l])


def _project(x2d, wts, pos, tm_rope, tm_plain, transposed_k_idx=False):
    tabs = _rope_tables(pos, D_HEAD) + _rope_tables(pos, D_IDX)
    roped = _inproj_rope(x2d, wts["g_mix"], wts["w_a"], tabs, tm_rope, pos.shape[0] // tm_rope, transposed_k_idx)
    return (tuple(roped),) + tuple(_inproj_plain(x2d, wts["g_mix"], wts["w_bg"], tm_plain))


def _pad_rows(a, n):
    return jnp.pad(a, ((0, 0), (0, n - a.shape[1]), (0, 0)))


def kernel(x_prompt, x_sample, cache_a_k, cache_a_v, cache_idx_k, cache_b_k, cache_b_v, page_table,
           g_mix, w_in, w_up_a, w_up_b, w_out, g_ffn, w_rg, b_rg, w_re, b_re,
           w_e_gate, w_e_up, w_e_down, g_final):
    depth = w_in.shape[0]
    batch, seq, _ = x_prompt.shape
    dec_batch, dec_seq, _ = x_sample.shape
    n_pages = page_table.shape[1]
    past = n_pages * PAGE_SIZE
    n_phys = cache_a_k.shape[1]
    gfin = g_final.reshape(1, D_MODEL)
    hp = x_prompt.reshape(batch * seq, D_MODEL)
    hs = x_sample.reshape(dec_batch * dec_seq, D_MODEL)
    rows_s = dec_batch * dec_seq
    new_p, new_s = [], []
    for l in range(depth):
        wts = _layer_weights(l, g_mix, w_in, w_up_a, w_up_b, w_out, g_ffn, w_rg, b_rg, w_re, b_re,
                             w_e_gate, w_e_up, w_e_down)
        last = l == depth - 1

        (q_a, k_a, v_a, q_i, kiwi, k_idx_t), k_b, v_b, qs = _project(hp, wts, np.arange(seq), 512, 1024, True)
        per_seq = lambda a: a.reshape(batch, seq, a.shape[-1])
        o_a = _dsa_prompt(per_seq(q_i), per_seq(kiwi), per_seq(q_a), per_seq(k_a), per_seq(v_a),
                          batch, seq, 256).reshape(batch * seq, A_WIDTH)
        o_b = _sb_prompt(qs, k_b, v_b, batch, seq, 512)
        prompt_rows = (hp, o_a, o_b, qs)
        new_p.append((k_a.reshape(batch, seq, D_HEAD), v_a.reshape(batch, seq, D_HEAD),
                      jnp.swapaxes(k_idx_t, 1, 2),
                      k_b.reshape(batch, seq, N_HEADS_B, D_HEAD), v_b.reshape(batch, seq, N_HEADS_B, D_HEAD)))

        pos_s = np.tile(past + np.arange(dec_seq), dec_batch)
        (q_a, k_a, v_a, q_i, kiwi), k_b, v_b, qs = _project(hs, wts, pos_s, rows_s, rows_s)
        q_b = qs[:, 2 * D_MODEL:]
        per_b = lambda a: a.reshape(dec_batch, dec_seq, -1)
        heads_first = lambda a, nh, d: (per_b(a).reshape(dec_batch, dec_seq, nh, d).transpose(0, 2, 1, 3)
                                        .reshape(dec_batch, nh * dec_seq, d))
        w_idx = per_b(kiwi)[:, :, D_IDX:D_IDX + N_IDX_HEADS] * ((N_IDX_HEADS * D_IDX) ** -0.5)
        wi_s = w_idx.transpose(0, 2, 1).reshape(dec_batch, N_IDX_HEADS * dec_seq, 1)
        kin_s = _pad_rows(per_b(kiwi)[:, :, :D_IDX], LANES)
        o_a = _dsa_sample(page_table, heads_first(q_i, N_IDX_HEADS, D_IDX), wi_s, kin_s,
                          heads_first(q_a, N_HEADS_A, D_HEAD), _pad_rows(per_b(k_a), LANES),
                          _pad_rows(per_b(v_a), LANES), cache_idx_k[l], cache_a_k[l], cache_a_v[l])
        blk_lanes = SB_BLOCK_KEYS * N_HEADS_B
        kb_new = _pad_rows(k_b.reshape(dec_batch, dec_seq * N_HEADS_B, D_HEAD), blk_lanes)
        vb_new = _pad_rows(v_b.reshape(dec_batch, dec_seq * N_HEADS_B, D_HEAD), blk_lanes)
        pool_rows = lambda c: c.reshape(n_phys, PAGE_SIZE * N_HEADS_B, D_HEAD)
        o_b = _sb_sample(page_table, heads_first(q_b, N_HEADS_B, D_HEAD), kb_new, vb_new,
                         pool_rows(cache_b_k[l]), pool_rows(cache_b_v[l]))
        tok_major = lambda o, nh: (o.reshape(dec_batch, nh, dec_seq, D_HEAD).transpose(0, 2, 1, 3)
                                   .reshape(rows_s, nh * D_HEAD).astype(BF16))
        sample_rows = (hs, tok_major(o_a, N_HEADS_A), tok_major(o_b, N_HEADS_B), qs)
        new_s.append((k_a.reshape(dec_batch, dec_seq, D_HEAD), v_a.reshape(dec_batch, dec_seq, D_HEAD),
                      kiwi[:, :D_IDX].reshape(dec_batch, dec_seq, D_IDX),
                      k_b.reshape(dec_batch, dec_seq, N_HEADS_B, D_HEAD),
                      v_b.reshape(dec_batch, dec_seq, N_HEADS_B, D_HEAD)))

        h, hn, logits = _merge(prompt_rows, sample_rows, wts["w_up_a"], wts["w_up_b"], wts["w_out"], wts["g_ffn"],
                               *wts["router"], MERGE_ROWS)
        hp, hs = _moe(h, hn, logits, gfin, wts["w_e_gate"], wts["w_e_up"], wts["w_e_down"], MERGE_ROWS,
                      batch * seq, last)

    stack = lambda rows_, i: jnp.stack([r[i] for r in rows_])
    return (hp.reshape(batch, seq, D_MODEL), hs.reshape(dec_batch, dec_seq, D_MODEL),
            stack(new_p, 0), stack(new_p, 1), stack(new_p, 2), stack(new_p, 3), stack(new_p, 4),
            stack(new_s, 0), stack(new_s, 1), stack(new_s, 2), stack(new_s, 3), stack(new_s, 4))
```

```python
import functools
import math

import numpy as np
import jax
import jax.numpy as jnp
from jax import lax
from jax.experimental import pallas as pl
from jax.experimental.pallas import tpu as pltpu

D_MODEL = 2048
D_HEAD = 128
N_HEADS_A = 8
N_IDX_HEADS = 16
D_IDX = 64
TOPK_MAX = 256
N_HEADS_B = 8
ROPE_THETA = 10000.0
N_GROUPS = 4
EXPERTS_PER_GROUP = 8
N_EXPERTS = N_GROUPS * EXPERTS_PER_GROUP
TOP_K_EXPERTS = 2
D_EXPERT = 512
PAGE_SIZE = 128
NORM_EPS = 1e-6
A_WIDTH = N_HEADS_A * D_HEAD
B_WIDTH = N_HEADS_B * D_HEAD
IDX_WIDTH = N_IDX_HEADS * D_IDX

LANES = 128
VMEM_LIMIT = 56 * 1024 * 1024
EXPERT_ROWS = 256
EXPERT_K_CHUNK = 256
SB_HEADS_PER_STEP = 2
MERGE_ROWS = 256
ROUTER_LANES = 128

assert TOP_K_EXPERTS == 2

F32 = jnp.float32
BF16 = jnp.bfloat16
NEG_INF = float("-inf")
INT_MIN = -2 ** 31


def _dot(a, b):
    return jnp.dot(a, b, preferred_element_type=F32)


def _dot_nt(a, b):
    return lax.dot_general(a, b, (((1,), (1,)), ((), ())), preferred_element_type=F32)


def _params(semantics):
    return pltpu.CompilerParams(dimension_semantics=semantics, vmem_limit_bytes=VMEM_LIMIT)


def _rope_tables(pos, d):
    half = d // 2
    inv_freq = np.exp(np.arange(half, dtype=np.float32) * np.float32(-2.0 * math.log(ROPE_THETA) / d))
    ang = np.asarray(pos, np.float32)[:, None] * inv_freq[None, :]
    cos, sin = np.cos(ang), np.sin(ang)
    reps = LANES // d
    cos_t = np.tile(np.concatenate([cos, cos], axis=-1), (1, reps))
    sin_t = np.tile(np.concatenate([-sin, sin], axis=-1), (1, reps))
    return jnp.asarray(cos_t, F32), jnp.asarray(sin_t, F32)


def _swap_halves(blk, d):
    if d == LANES:
        return pltpu.roll(blk, LANES // 2, 1)
    lane = lax.broadcasted_iota(jnp.int32, blk.shape, 1)
    first_half = (lane % d) < (d // 2)
    return jnp.where(first_half, pltpu.roll(blk, LANES - d // 2, 1), pltpu.roll(blk, d // 2, 1))


def _rmsnorm_rows(x, g):
    inv = lax.rsqrt(jnp.mean(x * x, axis=-1, keepdims=True) + NORM_EPS)
    return x * inv * g


def _inproj_rope_kernel(x_ref, g_ref, w_ref, cos_ref, sin_ref, cosi_ref, sini_ref,
                        qa_ref, ka_ref, va_ref, qi_ref, kiwi_ref, kit_ref=None):
    xn = _rmsnorm_rows(x_ref[...], g_ref[...]).astype(BF16)
    y = _dot(xn, w_ref[...])
    cos, sin = cos_ref[...], sin_ref[...]
    for h in range(N_HEADS_A):
        blk = y[:, h * D_HEAD:(h + 1) * D_HEAD]
        qa_ref[:, h * D_HEAD:(h + 1) * D_HEAD] = (blk * cos + _swap_halves(blk, D_HEAD) * sin).astype(BF16)
    ka = y[:, A_WIDTH:A_WIDTH + D_HEAD]
    ka_ref[...] = ka * cos + _swap_halves(ka, D_HEAD) * sin
    va_ref[...] = y[:, A_WIDTH + D_HEAD:A_WIDTH + 2 * D_HEAD]
    cosi, sini = cosi_ref[...], sini_ref[...]
    base = A_WIDTH + 2 * D_HEAD
    for j in range(IDX_WIDTH // LANES):
        blk = y[:, base + j * LANES:base + (j + 1) * LANES]
        qi_ref[:, j * LANES:(j + 1) * LANES] = (blk * cosi + _swap_halves(blk, D_IDX) * sini).astype(BF16)
    blk = y[:, base + IDX_WIDTH:base + IDX_WIDTH + LANES]
    lane = lax.broadcasted_iota(jnp.int32, blk.shape, 1)
    is_k = lane < D_IDX
    kiwi = blk * jnp.where(is_k, cosi, 1.0) + _swap_halves(blk, D_IDX) * jnp.where(is_k, sini, 0.0)
    kiwi_ref[...] = kiwi
    if kit_ref is not None:
        kit_ref[0] = kiwi.T[:D_IDX, :]


def _inproj_rope(x, g, w_a, tabs, tm, n_pos_blocks, transposed_k_idx):
    rows = x.shape[0]
    wa_cols = w_a.shape[1]
    cos, sin, cosi, sini = tabs
    row_spec = lambda width: pl.BlockSpec((tm, width), lambda i: (i, 0))
    tab_spec = pl.BlockSpec((tm, LANES), lambda i: (i % n_pos_blocks, 0))
    out_specs = [row_spec(A_WIDTH), row_spec(D_HEAD), row_spec(D_HEAD), row_spec(IDX_WIDTH), row_spec(LANES)]
    out_shape = [jax.ShapeDtypeStruct((rows, A_WIDTH), BF16),
                 jax.ShapeDtypeStruct((rows, D_HEAD), F32),
                 jax.ShapeDtypeStruct((rows, D_HEAD), F32),
                 jax.ShapeDtypeStruct((rows, IDX_WIDTH), BF16),
                 jax.ShapeDtypeStruct((rows, LANES), F32)]
    if transposed_k_idx:
        out_specs.append(pl.BlockSpec((1, D_IDX, tm), lambda i: (i // n_pos_blocks, 0, i % n_pos_blocks)))
        out_shape.append(jax.ShapeDtypeStruct((rows // (n_pos_blocks * tm), D_IDX, n_pos_blocks * tm), F32))
    return pl.pallas_call(
        _inproj_rope_kernel,
        grid=(rows // tm,),
        in_specs=[row_spec(D_MODEL),
                  pl.BlockSpec((1, D_MODEL), lambda i: (0, 0)),
                  pl.BlockSpec((D_MODEL, wa_cols), lambda i: (0, 0)),
                  tab_spec, tab_spec, tab_spec, tab_spec],
        out_specs=out_specs,
        out_shape=out_shape,
        compiler_params=_params(("arbitrary",)),
        name="inproj_rope",
    )(x, g, w_a, cos, sin, cosi, sini)


N_GATE_TILES = 2 * D_MODEL // B_WIDTH


def _inproj_plain_kernel(x_ref, g_ref, w_ref, kb_ref, vb_ref, qs_ref, xn_ref):
    j = pl.program_id(1)

    @pl.when(j == 0)
    def _():
        xn_ref[...] = _rmsnorm_rows(x_ref[...], g_ref[...]).astype(BF16)

    product = lambda: _dot(xn_ref[...], w_ref[...])

    @pl.when(j == 0)
    def _():
        qs_ref[...] = product().astype(BF16)

    @pl.when(j == 1)
    def _():
        kb_ref[...] = product()

    @pl.when(j == 2)
    def _():
        vb_ref[...] = product()

    @pl.when(j >= 3)
    def _():
        qs_ref[...] = (1.0 / (1.0 + jnp.exp(-product()))).astype(BF16)


def _inproj_plain(x, g, w_bg, tm):
    rows = x.shape[0]
    tn = B_WIDTH
    n_col = w_bg.shape[1] // tn
    assert n_col == 3 + N_GATE_TILES
    fixed = pl.BlockSpec((tm, tn), lambda i, j: (i, 0))
    qs_map = lambda i, j: (i, jnp.where(j < 3, N_GATE_TILES, j - 3))
    return pl.pallas_call(
        _inproj_plain_kernel,
        grid=(rows // tm, n_col),
        in_specs=[pl.BlockSpec((tm, D_MODEL), lambda i, j: (i, 0)),
                  pl.BlockSpec((1, D_MODEL), lambda i, j: (0, 0)),
                  pl.BlockSpec((D_MODEL, tn), lambda i, j: (0, j))],
        out_specs=[fixed, fixed, pl.BlockSpec((tm, tn), qs_map)],
        out_shape=[jax.ShapeDtypeStruct((rows, tn), F32), jax.ShapeDtypeStruct((rows, tn), F32),
                   jax.ShapeDtypeStruct((rows, (N_GATE_TILES + 1) * tn), BF16)],
        scratch_shapes=[pltpu.VMEM((tm, D_MODEL), BF16)],
        compiler_params=_params(("arbitrary", "arbitrary")),
        name="inproj_plain",
    )(x, g, w_bg)


def _sortable(x):
    b = lax.bitcast_convert_type(x, jnp.int32)
    return b ^ ((b >> 31) & jnp.int32(0x7FFFFFFF))


def _select_topk(keys_ref, neg_ref, valid_fn, k_top):
    rows, width = keys_ref.shape
    n_chunks = width // LANES
    ones = jnp.ones((LANES, LANES), BF16)
    lane = lax.broadcasted_iota(jnp.int32, (rows, LANES), 1)

    def lane_total(acc):
        return _dot(acc.astype(BF16), ones)

    def count(pred):
        acc = jnp.zeros((rows, LANES), F32)
        for c in range(n_chunks):
            acc = acc + jnp.where(pred(keys_ref[:, c * LANES:(c + 1) * LANES], c), 1.0, 0.0)
        return lane_total(acc)

    def value_step(i, ans):
        cand_u = ans | lax.shift_left(jnp.int32(1), 31 - i)
        cand = cand_u ^ jnp.int32(INT_MIN)
        cnt = count(lambda k, c: k >= cand)
        return jnp.where(cnt >= k_top, cand_u, ans)

    thr = lax.fori_loop(0, 32, value_step, jnp.zeros((rows, LANES), jnp.int32)) ^ jnp.int32(INT_MIN)
    n_gt = count(lambda k, c: k > thr)
    n_ge = count(lambda k, c: k >= thr)
    need = k_top - n_gt

    n_bits = (width - 1).bit_length()

    def index_step(i, cst):
        cand = cst | lax.shift_left(jnp.int32(1), n_bits - 1 - i)
        cnt = count(lambda k, c: (k == thr) & (lane + c * LANES < cand))
        return jnp.where(cnt < need, cand, cst)

    cut = lax.cond(jnp.max(n_ge) > k_top,
                   lambda: lax.fori_loop(0, n_bits, index_step, jnp.zeros((rows, LANES), jnp.int32)),
                   lambda: jnp.full((rows, LANES), width, jnp.int32))

    for c in range(n_chunks):
        k = keys_ref[:, c * LANES:(c + 1) * LANES]
        tie_keep = jnp.where(lane + c * LANES <= cut, 0.0, NEG_INF)
        keep = jnp.where(k > thr, 0.0, jnp.where(k == thr, tie_keep, NEG_INF))
        neg_ref[:, c * LANES:(c + 1) * LANES] = jnp.where(valid_fn(c), keep, NEG_INF)


def _dsa_prompt_kernel(qi_ref, wq_ref, kk_ref, qa_ref, ka_ref, va_ref, *rest, k_top, q_start):
    _, o_ref, keys_ref, neg_ref = rest
    tq, width = keys_ref.shape

    w = wq_ref[0, :, D_IDX:D_IDX + N_IDX_HEADS] * ((N_IDX_HEADS * D_IDX) ** -0.5)
    k_idx = kk_ref[0, :, :D_IDX].astype(BF16)
    score = jnp.zeros((tq, width), F32)
    for h in range(N_IDX_HEADS):
        rel = jnp.maximum(_dot_nt(qi_ref[0, :, h * D_IDX:(h + 1) * D_IDX], k_idx), 0.0)
        score = score + rel * w[:, h:h + 1]

    t_pos = q_start + lax.broadcasted_iota(jnp.int32, (tq, LANES), 0)
    lane = lax.broadcasted_iota(jnp.int32, (tq, LANES), 1)
    causal = lambda c: lane + c * LANES <= t_pos
    s_pos = lax.broadcasted_iota(jnp.int32, (tq, width), 1)
    q_pos = q_start + lax.broadcasted_iota(jnp.int32, (tq, width), 0)
    keys_ref[...] = _sortable(jnp.where(s_pos <= q_pos, score, NEG_INF))
    _select_topk(keys_ref, neg_ref, causal, k_top)

    k_att = ka_ref[0].astype(BF16)
    v_att = va_ref[0].astype(BF16)
    exp2_scale = (D_HEAD ** -0.5) * math.log2(math.e)
    for h in range(N_HEADS_A):
        s = _dot_nt(qa_ref[0, :, h * D_HEAD:(h + 1) * D_HEAD], k_att) + neg_ref[...]
        m = jnp.max(s, axis=-1, keepdims=True)
        e = jnp.exp2((s - m) * exp2_scale)
        l = jnp.sum(e, axis=-1, keepdims=True)
        o = _dot(e.astype(BF16), v_att) / l
        o_ref[0, :, h * D_HEAD:(h + 1) * D_HEAD] = o.astype(BF16)


def _dsa_prompt(q_i, kiwi, q_a, k_a, v_a, batch, seq, tq):
    k_top = min(TOPK_MAX, seq // 4)
    out = jnp.zeros((batch, seq, A_WIDTH), BF16)
    for qb in range(seq // tq):
        width = (qb + 1) * tq
        q_spec = lambda w_: pl.BlockSpec((1, tq, w_), lambda b: (b, qb, 0))
        kv_spec = lambda w_: pl.BlockSpec((1, width, w_), lambda b: (b, 0, 0))
        out = pl.pallas_call(
            functools.partial(_dsa_prompt_kernel, k_top=k_top, q_start=qb * tq),
            grid=(batch,),
            in_specs=[q_spec(IDX_WIDTH), q_spec(LANES), kv_spec(LANES), q_spec(A_WIDTH), kv_spec(D_HEAD),
                      kv_spec(D_HEAD), pl.BlockSpec(memory_space=pl.ANY)],
            out_specs=q_spec(A_WIDTH),
            out_shape=jax.ShapeDtypeStruct((batch, seq, A_WIDTH), BF16),
            scratch_shapes=[pltpu.VMEM((tq, width), jnp.int32), pltpu.VMEM((tq, width), F32)],
            input_output_aliases={6: 0},
            compiler_params=_params(("arbitrary",)),
            name=f"dsa_prompt_q{qb}",
        )(q_i, kiwi, kiwi, q_a, k_a, v_a, out)
    return out


def _suffix_matrix(n, reps=1):
    j = lax.broadcasted_iota(jnp.int32, (n * reps, n * reps), 0)
    s = lax.broadcasted_iota(jnp.int32, (n * reps, n * reps), 1)
    same = (j % reps) == (s % reps)
    later = jnp.where(same & (j // reps > s // reps), 1.0, 0.0)
    total = jnp.where(same, 1.0, 0.0)
    return jnp.concatenate([later, total], axis=1).astype(BF16)


def _softplus(z):
    return jnp.maximum(z, 0.0) + jnp.log(1.0 + jnp.exp(-jnp.abs(z)))


def _sb_prompt_kernel(q_ref, k_ref, v_ref, u_ref, o_ref, k16, v16, acc_ref, later_ref):
    qb = pl.program_id(2)
    tq = q_ref.shape[0]
    sub = LANES
    n_sub = tq // sub

    @pl.when(qb == 0)
    def _():
        k16[...] = k_ref[...].astype(BF16)
        v16[...] = v_ref[...].astype(BF16)

    u = u_ref[...]
    n_heads = q_ref.shape[1] // D_HEAD

    def block(kb, diagonal):
        start = pl.multiple_of(kb * tq, tq)
        for hh in range(n_heads):
            cols = slice(hh * D_HEAD, (hh + 1) * D_HEAD)
            z = _dot_nt(q_ref[:, cols], k16[pl.ds(start, tq), cols]) * (D_HEAD ** -0.5)
            sp = _softplus(z)
            log_beta = z - sp
            if diagonal:
                before = (lax.broadcasted_iota(jnp.int32, (tq, tq), 1)
                          < lax.broadcasted_iota(jnp.int32, (tq, tq), 0))
                sp = jnp.where(before, sp, 0.0)
            sp16 = sp.astype(BF16)
            later = later_ref[:, cols]
            args = [None] * n_sub
            for j in reversed(range(n_sub)):
                cum = _dot(sp16[:, j * sub:(j + 1) * sub], u)
                args[j] = log_beta[:, j * sub:(j + 1) * sub] - cum[:, :sub] - later
                later = later + cum[:, sub:]
            a = jnp.exp(jnp.concatenate(args, axis=1))
            if diagonal:
                a = jnp.where(before, a, 0.0)
            acc_ref[:, cols] += _dot(a.astype(BF16), v16[pl.ds(start, tq), cols])
            later_ref[:, cols] = later

    acc_ref[...] = jnp.zeros_like(acc_ref)
    later_ref[...] = jnp.zeros_like(later_ref)
    block(qb, True)

    def body(it, carry):
        block(qb - 1 - it, False)
        return carry

    lax.fori_loop(0, qb, body, 0)
    o_ref[...] = acc_ref[...].astype(BF16)


def _sb_prompt(qs, k_b, v_b, batch, seq, tq):
    rows = batch * seq
    nq = seq // tq
    hw = SB_HEADS_PER_STEP * D_HEAD
    q_col = (qs.shape[1] - B_WIDTH) // hw
    q_spec = pl.BlockSpec((tq, hw), lambda b, h, q: (b * nq + q, q_col + h))
    kv_spec = pl.BlockSpec((seq, hw), lambda b, h, q: (b, h))
    return pl.pallas_call(
        _sb_prompt_kernel,
        grid=(batch, N_HEADS_B // SB_HEADS_PER_STEP, nq),
        in_specs=[q_spec, kv_spec, kv_spec, pl.BlockSpec((LANES, 2 * LANES), lambda b, h, q: (0, 0))],
        out_specs=pl.BlockSpec((tq, hw), lambda b, h, q: (b * nq + q, h)),
        out_shape=jax.ShapeDtypeStruct((rows, B_WIDTH), BF16),
        scratch_shapes=[pltpu.VMEM((seq, hw), BF16), pltpu.VMEM((seq, hw), BF16),
                        pltpu.VMEM((tq, hw), F32), pltpu.VMEM((tq, hw), F32)],
        compiler_params=_params(("arbitrary", "arbitrary", "arbitrary")),
        name="sb_prompt",
    )(qs, k_b, v_b, _suffix_matrix(LANES))


def _page_copies(pt_ref, batch_idx, first_page, n_pages, pools, bufs, slot, sem, rows_per_page, n_logical):
    copies = []
    for p in range(n_pages):
        phys = pt_ref[batch_idx * n_logical + first_page + p]
        for pool, buf in zip(pools, bufs):
            copies.append(pltpu.make_async_copy(
                pool.at[phys], buf.at[slot, pl.ds(p * rows_per_page, rows_per_page)], sem.at[slot]))
    return copies


def _paged_double_buffer(pt_ref, pools, bufs, sem, n_pages, rows_per_page):
    b = pl.program_id(0)
    slot = b % 2
    copies = lambda bb, ss: _page_copies(pt_ref, bb, 0, n_pages, pools, bufs, ss, sem, rows_per_page, n_pages)

    @pl.when(b == 0)
    def _():
        for c in copies(0, 0):
            c.start()

    @pl.when(b + 1 < pl.num_programs(0))
    def _():
        for c in copies(b + 1, 1 - slot):
            c.start()

    for c in copies(b, slot):
        c.wait()
    return slot


def _new_key_visible(n_q, n_new, rows):
    t_idx = lax.broadcasted_iota(jnp.int32, (rows, LANES), 0) % n_q
    lane = lax.broadcasted_iota(jnp.int32, (rows, LANES), 1)
    return (lane <= t_idx) & (lane < n_new)


def _idx_scores_sample_kernel(pt_ref, qi_ref, w_ref, kin_ref, idx_pool, s_ref, idx_buf, sem, *, n_pages, n_new):
    b = pl.program_id(0)
    slot = b % 2
    past = n_pages * PAGE_SIZE

    def copies(bb, ss):
        return [pltpu.make_async_copy(idx_pool.at[pt_ref[bb * n_pages + p]],
                                      idx_buf.at[ss, :, pl.ds(p * PAGE_SIZE, PAGE_SIZE)], sem.at[ss])
                for p in range(n_pages)]

    @pl.when(b == 0)
    def _():
        for c in copies(0, 0):
            c.start()

    @pl.when(b + 1 < pl.num_programs(0))
    def _():
        for c in copies(b + 1, 1 - slot):
            c.start()

    for c in copies(b, slot):
        c.wait()

    q = qi_ref[0]
    w = w_ref[0]
    n_q = q.shape[0] // N_IDX_HEADS

    def head_sum(qk):
        rel = jnp.maximum(qk, 0.0) * w
        return functools.reduce(lambda a, c: a + c, [rel[h * n_q:(h + 1) * n_q] for h in range(N_IDX_HEADS)])

    s_ref[0, :, :past] = head_sum(_dot(q, idx_buf[slot].astype(BF16)))
    s_new = head_sum(_dot_nt(q, kin_ref[0].astype(BF16)))
    s_ref[0, :, past:] = jnp.where(_new_key_visible(n_q, n_new, n_q), s_new, NEG_INF)


def _topk_mask_kernel(s_ref, neg_ref, keys_ref, *, k_top, n_q, n_new):
    rows, width = keys_ref.shape
    keys_ref[...] = _sortable(s_ref[...])
    visible = _new_key_visible(n_q, n_new, rows)
    last = width // LANES - 1
    _select_topk(keys_ref, neg_ref, lambda c: visible if c == last else jnp.full((rows, LANES), True), k_top)


def _dsa_sample_attend_kernel(pt_ref, qa_ref, neg_ref, kan_ref, van_ref, k_pool, v_pool, o_ref,
                              k_buf, v_buf, sem, *, n_pages):
    slot = _paged_double_buffer(pt_ref, (k_pool, v_pool), (k_buf, v_buf), sem, n_pages, PAGE_SIZE)
    past = n_pages * PAGE_SIZE
    k_att = k_buf[slot].astype(BF16)
    v_att = v_buf[slot].astype(BF16)
    q = qa_ref[0]
    neg = jnp.concatenate([neg_ref[0]] * N_HEADS_A, axis=0)
    s_past = _dot_nt(q, k_att) + neg[:, :past]
    s_new = _dot_nt(q, kan_ref[0].astype(BF16)) + neg[:, past:]
    m = jnp.maximum(jnp.max(s_past, axis=-1, keepdims=True), jnp.max(s_new, axis=-1, keepdims=True))
    exp2_scale = (D_HEAD ** -0.5) * math.log2(math.e)
    e_past = jnp.exp2((s_past - m) * exp2_scale)
    e_new = jnp.exp2((s_new - m) * exp2_scale)
    l = jnp.sum(e_past, axis=-1, keepdims=True) + jnp.sum(e_new, axis=-1, keepdims=True)
    o = _dot(e_past.astype(BF16), v_att) + _dot(e_new.astype(BF16), van_ref[0].astype(BF16))
    o_ref[0] = o / l


def _dsa_sample(page_table, q_i, w_i, k_i_new, q_a, k_a_new, v_a_new, idx_pool, k_pool, v_pool):
    batch, n_pages = page_table.shape
    n_q = q_a.shape[1] // N_HEADS_A
    past = n_pages * PAGE_SIZE
    width = past + LANES
    k_top = min(TOPK_MAX, (past + n_q) // 4)
    pt = page_table.reshape(-1)
    blk = lambda shape: pl.BlockSpec((1,) + shape, lambda b, pt_: (b,) + (0,) * len(shape))
    hbm = pl.BlockSpec(memory_space=pl.ANY)

    scores = pl.pallas_call(
        functools.partial(_idx_scores_sample_kernel, n_pages=n_pages, n_new=n_q),
        grid_spec=pltpu.PrefetchScalarGridSpec(
            num_scalar_prefetch=1, grid=(batch,),
            in_specs=[blk((N_IDX_HEADS * n_q, D_IDX)), blk((N_IDX_HEADS * n_q, 1)), blk((LANES, D_IDX)), hbm],
            out_specs=blk((n_q, width)),
            scratch_shapes=[pltpu.VMEM((2, D_IDX, past), F32), pltpu.SemaphoreType.DMA((2,))]),
        out_shape=jax.ShapeDtypeStruct((batch, n_q, width), F32),
        compiler_params=_params(("arbitrary",)),
        name="idx_scores_sample",
    )(pt, q_i, w_i, k_i_new, idx_pool)

    rows = batch * n_q
    tr = min(rows, LANES)
    neg = pl.pallas_call(
        functools.partial(_topk_mask_kernel, k_top=k_top, n_q=n_q, n_new=n_q),
        grid=(rows // tr,),
        in_specs=[pl.BlockSpec((tr, width), lambda i: (i, 0))],
        out_specs=pl.BlockSpec((tr, width), lambda i: (i, 0)),
        out_shape=jax.ShapeDtypeStruct((rows, width), F32),
        scratch_shapes=[pltpu.VMEM((tr, width), jnp.int32)],
        compiler_params=_params(("arbitrary",)),
        name="topk_mask_sample",
    )(scores.reshape(rows, width))

    return pl.pallas_call(
        functools.partial(_dsa_sample_attend_kernel, n_pages=n_pages),
        grid_spec=pltpu.PrefetchScalarGridSpec(
            num_scalar_prefetch=1, grid=(batch,),
            in_specs=[blk((N_HEADS_A * n_q, D_HEAD)), blk((n_q, width)), blk((LANES, D_HEAD)), blk((LANES, D_HEAD)),
                      hbm, hbm],
            out_specs=blk((N_HEADS_A * n_q, D_HEAD)),
            scratch_shapes=[pltpu.VMEM((2, past, D_HEAD), F32), pltpu.VMEM((2, past, D_HEAD), F32),
                            pltpu.SemaphoreType.DMA((2,))]),
        out_shape=jax.ShapeDtypeStruct((batch, N_HEADS_A * n_q, D_HEAD), F32),
        compiler_params=_params(("arbitrary",)),
        name="dsa_sample_attend",
    )(pt, q_a, neg.reshape(batch, n_q, width), k_a_new, v_a_new, k_pool, v_pool)


SB_CHUNK_PAGES = 8
SB_SLOTS = 3
SB_BLOCK_KEYS = 32


def _sb_sample_kernel(pt_ref, q_ref, kn_ref, vn_ref, u_ref, k_pool, v_pool, o_ref,
                      k_buf, v_buf, acc_ref, later_ref, sem, *, n_pages, n_new, total_steps):
    b = pl.program_id(0)
    j = pl.program_id(1)
    n_chunks = pl.num_programs(1)
    step = b * n_chunks + j
    n_steps = total_steps
    n_slots = k_buf.shape[0]
    slot = step % n_slots
    heads = N_HEADS_B
    rows_per_page = PAGE_SIZE * heads
    blk_lanes = SB_BLOCK_KEYS * heads

    def copies(st, ss):
        bb = st // n_chunks
        chunk = n_chunks - 1 - st % n_chunks
        return _page_copies(pt_ref, bb, chunk * SB_CHUNK_PAGES, SB_CHUNK_PAGES, (k_pool, v_pool),
                            (k_buf, v_buf), ss, sem, rows_per_page, n_pages)

    @pl.when(step == 0)
    def _():
        for ahead in range(min(n_slots - 1, total_steps)):
            for c in copies(ahead, ahead):
                c.start()

    @pl.when(step + n_slots - 1 < n_steps)
    def _():
        for c in copies(step + n_slots - 1, (step + n_slots - 1) % n_slots):
            c.start()

    q = q_ref[0]
    n_q = q.shape[0] // heads
    u = u_ref[...]
    scale = D_HEAD ** -0.5

    def head_diag(zt):
        lane_head = lax.broadcasted_iota(jnp.int32, (n_q, zt.shape[1]), 1) % heads
        out = jnp.zeros((n_q, zt.shape[1]), F32)
        for h in range(heads):
            out = out + jnp.where(lane_head == h, zt[h * n_q:(h + 1) * n_q, :], 0.0)
        return out

    def head_spread(a):
        lane_head = lax.broadcasted_iota(jnp.int32, a.shape, 1) % heads
        return jnp.concatenate([jnp.where(lane_head == h, a, 0.0) for h in range(heads)], axis=0).astype(BF16)

    def attend(k16, v16, before, later):
        n = k16.shape[0]
        z = head_diag(_dot_nt(q, k16) * scale)
        sp = _softplus(z)
        log_beta = z - sp
        sp16 = jnp.where(before, sp, 0.0).astype(BF16)
        pieces = []
        for blk in reversed(range(n // blk_lanes)):
            sl = slice(blk * blk_lanes, (blk + 1) * blk_lanes)
            cum = _dot(sp16[:, sl], u)
            pieces.append(log_beta[:, sl] - cum[:, :blk_lanes] - later)
            later = later + cum[:, blk_lanes:]
        a = jnp.exp(jnp.concatenate(pieces[::-1], axis=1))
        a = jnp.where(before, a, 0.0)
        return _dot(head_spread(a), v16), later

    @pl.when(j == 0)
    def _():
        t_idx = lax.broadcasted_iota(jnp.int32, (n_q, blk_lanes), 0)
        key = lax.broadcasted_iota(jnp.int32, (n_q, blk_lanes), 1) // heads
        before = (key < t_idx) & (key < n_new)
        out, later = attend(kn_ref[0].astype(BF16), vn_ref[0].astype(BF16), before,
                            jnp.zeros((n_q, blk_lanes), F32))
        acc_ref[...] = out
        later_ref[...] = later

    for c in copies(step, slot):
        c.wait()

    n_rows = SB_CHUNK_PAGES * rows_per_page
    out, later = attend(k_buf[slot].astype(BF16), v_buf[slot].astype(BF16),
                        jnp.full((n_q, n_rows), True), later_ref[...])
    acc_ref[...] += out
    later_ref[...] = later

    @pl.when(j == n_chunks - 1)
    def _():
        o_ref[0] = acc_ref[...]


def _sb_sample(page_table, q, k_new, v_new, k_pool, v_pool):
    batch, n_pages = page_table.shape
    rows_q = q.shape[1]
    n_q = rows_q // N_HEADS_B
    blk_lanes = SB_BLOCK_KEYS * N_HEADS_B
    n_chunks = n_pages // SB_CHUNK_PAGES
    chunk_rows = SB_CHUNK_PAGES * PAGE_SIZE * N_HEADS_B
    blk = lambda shape: pl.BlockSpec((1,) + shape, lambda b, j, pt: (b,) + (0,) * len(shape))
    hbm = pl.BlockSpec(memory_space=pl.ANY)
    grid_spec = pltpu.PrefetchScalarGridSpec(
        num_scalar_prefetch=1,
        grid=(batch, n_chunks),
        in_specs=[blk((rows_q, D_HEAD)), blk((blk_lanes, D_HEAD)), blk((blk_lanes, D_HEAD)),
                  pl.BlockSpec((blk_lanes, 2 * blk_lanes), lambda b, j, pt: (0, 0)), hbm, hbm],
        out_specs=blk((rows_q, D_HEAD)),
        scratch_shapes=[pltpu.VMEM((SB_SLOTS, chunk_rows, D_HEAD), F32), pltpu.VMEM((SB_SLOTS, chunk_rows, D_HEAD), F32),
                        pltpu.VMEM((rows_q, D_HEAD), F32), pltpu.VMEM((n_q, blk_lanes), F32),
                        pltpu.SemaphoreType.DMA((SB_SLOTS,))],
    )
    return pl.pallas_call(
        functools.partial(_sb_sample_kernel, n_pages=n_pages, n_new=n_q, total_steps=batch * n_chunks),
        grid_spec=grid_spec,
        out_shape=jax.ShapeDtypeStruct((batch, rows_q, D_HEAD), F32),
        compiler_params=_params(("arbitrary", "arbitrary")),
        name="sb_sample",
    )(page_table.reshape(-1), q, k_new, v_new, _suffix_matrix(SB_BLOCK_KEYS, N_HEADS_B), k_pool, v_pool)


def _merge_kernel(*refs, n_prompt_tiles):
    prompt, sample = refs[0:4], refs[4:8]
    wua_ref, wub_ref, wo_ref, gf_ref, wr_hi_ref, wr_lo_ref, br_ref, h_ref, hn_ref, lg_ref = refs[8:]

    def tile(x_ref, oa_ref, ob_ref, sig_ref):
        sig_a = sig_ref[:, :D_MODEL].astype(F32)
        sig_b = sig_ref[:, D_MODEL:].astype(F32)
        merged = sig_a * _dot(oa_ref[...], wua_ref[...]) + sig_b * _dot(ob_ref[...], wub_ref[...])
        h = x_ref[...] + _dot(merged.astype(BF16), wo_ref[...])
        h_ref[...] = h
        hn = _rmsnorm_rows(h, gf_ref[...])
        hn_ref[...] = hn
        hn_hi = hn.astype(BF16)
        hn_lo = (hn - hn_hi.astype(F32)).astype(BF16)
        lg_ref[...] = (_dot(hn_hi, wr_hi_ref[...]) + _dot(hn_lo, wr_hi_ref[...])
                       + _dot(hn_hi, wr_lo_ref[...]) + br_ref[...])

    i = pl.program_id(0)

    @pl.when(i < n_prompt_tiles)
    def _():
        tile(*prompt)

    @pl.when(i >= n_prompt_tiles)
    def _():
        tile(*sample)


def _merge(prompt, sample, w_up_a, w_up_b, w_out, g_ffn, wr_hi, wr_lo, b_r, tm):
    n_p, n_s = prompt[0].shape[0] // tm, sample[0].shape[0] // tm
    assert prompt[0].shape[0] == n_p * tm and sample[0].shape[0] == n_s * tm
    rows = (n_p + n_s) * tm
    widths = (D_MODEL, A_WIDTH, B_WIDTH, 2 * D_MODEL)
    p_spec = lambda width: pl.BlockSpec((tm, width), lambda i: (jnp.minimum(i, n_p - 1), 0))
    s_spec = lambda width: pl.BlockSpec((tm, width), lambda i: (jnp.maximum(i - n_p, 0), 0))
    row_spec = lambda width: pl.BlockSpec((tm, width), lambda i: (i, 0))
    const = lambda shape: pl.BlockSpec(shape, lambda i: (0, 0), pipeline_mode=pl.Buffered(1))
    return pl.pallas_call(
        functools.partial(_merge_kernel, n_prompt_tiles=n_p),
        grid=(n_p + n_s,),
        in_specs=[p_spec(w_) for w_ in widths] + [s_spec(w_) for w_ in widths] + [
                  const((A_WIDTH, D_MODEL)), const((B_WIDTH, D_MODEL)), const((D_MODEL, D_MODEL)),
                  const((1, D_MODEL)), const((D_MODEL, ROUTER_LANES)), const((D_MODEL, ROUTER_LANES)),
                  const((1, ROUTER_LANES))],
        out_specs=[row_spec(D_MODEL), row_spec(D_MODEL), row_spec(ROUTER_LANES)],
        out_shape=[jax.ShapeDtypeStruct((rows, D_MODEL), F32),
                   jax.ShapeDtypeStruct((rows, D_MODEL), F32),
                   jax.ShapeDtypeStruct((rows, ROUTER_LANES), F32)],
        compiler_params=_params(("arbitrary",)),
        name="merge",
    )(*prompt, *sample, w_up_a, w_up_b, w_out, g_ffn, wr_hi, wr_lo, b_r)


def _router_kernel(lg_ref, eid_ref, gate_ref, *, chunk):
    for c0 in range(0, lg_ref.shape[0], chunk):
        _route_chunk(lg_ref, eid_ref, gate_ref, c0, chunk)


def _route_chunk(lg_ref, eid_ref, gate_ref, c0, chunk):
    lg = lg_ref[c0:c0 + chunk, :].T
    row = lambda r: lg[r:r + 1, :]
    groups = [row(g) for g in range(N_GROUPS)]
    g_max = functools.reduce(jnp.maximum, groups)
    g_sel = jnp.full(g_max.shape, N_GROUPS - 1, jnp.int32)
    for g in reversed(range(N_GROUPS)):
        g_sel = jnp.where(groups[g] == g_max, g, g_sel)
    p_group = 1.0 / functools.reduce(lambda a, c: a + c, [jnp.exp(v - g_max) for v in groups])
    inside = []
    for e in range(EXPERTS_PER_GROUP):
        v = jnp.zeros_like(g_max)
        for g in range(N_GROUPS):
            v = jnp.where(g_sel == g, row(N_GROUPS + g * EXPERTS_PER_GROUP + e), v)
        inside.append(v)

    def top1(vals, skip):
        best = functools.reduce(jnp.maximum, [jnp.where(skip == e, NEG_INF, v) for e, v in enumerate(vals)])
        idx = jnp.full(best.shape, EXPERTS_PER_GROUP - 1, jnp.int32)
        for e in reversed(range(EXPERTS_PER_GROUP)):
            idx = jnp.where((vals[e] == best) & (skip != e), e, idx)
        return best, idx

    t1, i1 = top1(inside, jnp.full(g_max.shape, -1, jnp.int32))
    t2, i2 = top1(inside, i1)
    r = jnp.exp(t2 - t1)
    for k, ids in enumerate((g_sel * EXPERTS_PER_GROUP + i1, g_sel * EXPERTS_PER_GROUP + i2)):
        for c in range(chunk // LANES):
            row0 = c0 // LANES + c
            eid_ref[k, row0:row0 + 1, :] = ids[:, c * LANES:(c + 1) * LANES]
    gates = jnp.concatenate([p_group * (1.0 / (1.0 + r)), p_group * (r / (1.0 + r)),
                             jnp.zeros((ROUTER_LANES - TOP_K_EXPERTS, chunk), F32)], axis=0)
    gate_ref[c0:c0 + chunk, :] = gates.T


def _router(logits):
    n_tok = logits.shape[0]
    assert n_tok % LANES == 0
    chunk = max(c for c in range(LANES, 2048 + 1, LANES) if n_tok % c == 0)
    return pl.pallas_call(
        functools.partial(_router_kernel, chunk=chunk),
        out_shape=[jax.ShapeDtypeStruct((TOP_K_EXPERTS, n_tok // LANES, LANES), jnp.int32),
                   jax.ShapeDtypeStruct((n_tok, ROUTER_LANES), F32)],
        compiler_params=pltpu.CompilerParams(vmem_limit_bytes=VMEM_LIMIT),
        name="router",
    )(logits)


def _experts_kernel(blk_e_ref, blk_off_ref, blk_cnt_ref, nact_ref, order_ref,
                    hn_hbm, wg_ref, wu_ref, wd_ref, ys_hbm,
                    x_buf, y_buf, wg16, wu16, wd16, in_sem, out_sem, *, n_tok):
    i = pl.program_id(0)
    n_act = nact_ref[0]
    rows = x_buf.shape[1]
    slot = i % 2
    n_pairs = TOP_K_EXPERTS * n_tok

    def gathers(blk, ss):
        off = blk_off_ref[blk]
        copies = []
        for r in range(rows):
            p = order_ref[off + r]
            tok = jnp.where(p >= n_tok, p - n_tok, p)
            copies.append(pltpu.make_async_copy(hn_hbm.at[pl.ds(tok, 1)], x_buf.at[ss, pl.ds(r, 1)], in_sem.at[ss]))
        return copies

    def scatters(blk, ss, real):
        off = blk_off_ref[blk]
        cnt = jnp.where(real, blk_cnt_ref[blk], 0)
        copies = []
        for r in range(rows):
            dst = jnp.where(r < cnt, order_ref[off + r], n_pairs + ss * rows + r)
            copies.append(pltpu.make_async_copy(y_buf.at[ss, pl.ds(r, 1)], ys_hbm.at[pl.ds(dst, 1)], out_sem.at[ss]))
        return copies

    @pl.when(i == 0)
    def _():
        for c in gathers(0, 0):
            c.start()
        y_buf[1] = jnp.zeros(y_buf.shape[1:], F32)
        for ss in range(2):
            spare = pltpu.make_async_copy(y_buf.at[1], ys_hbm.at[pl.ds(n_pairs + ss * rows, rows)], out_sem.at[1])
            spare.start()
            spare.wait()

    @pl.when((i >= 1) & (i < n_act))
    def _():
        for c in scatters(jnp.maximum(i - 2, 0), slot, i >= 2):
            c.wait()

    first_of_expert = (i == 0) | (blk_e_ref[i] != blk_e_ref[jnp.maximum(i - 1, 0)])

    @pl.when((i < n_act) & first_of_expert)
    def _():
        wg16[...] = wg_ref[0].astype(BF16)
        wu16[...] = wu_ref[0].astype(BF16)
        wd16[...] = wd_ref[0].astype(BF16)

    @pl.when(i < n_act)
    def _():
        for c in gathers(i, slot):
            c.wait()
        starts = (gathers(jnp.minimum(i + 1, n_act - 1), 1 - slot)
                  + scatters(jnp.maximum(i - 1, 0), 1 - slot, i >= 1))
        n_chunks = 2 * D_MODEL // EXPERT_K_CHUNK
        per_chunk = len(starts) // n_chunks
        chunk_no = 0

        def issue_some():
            nonlocal chunk_no
            for c in starts[chunk_no * per_chunk:(chunk_no + 1) * per_chunk]:
                c.start()
            chunk_no += 1

        g = jnp.zeros((rows, D_EXPERT), F32)
        u = jnp.zeros((rows, D_EXPERT), F32)
        for k0 in range(0, D_MODEL, EXPERT_K_CHUNK):
            xc = x_buf[slot, :, k0:k0 + EXPERT_K_CHUNK].astype(BF16)
            g = g + _dot(xc, wg16[k0:k0 + EXPERT_K_CHUNK, :])
            u = u + _dot(xc, wu16[k0:k0 + EXPERT_K_CHUNK, :])
            issue_some()
        act = (g * (1.0 / (1.0 + jnp.exp(-g))) * u).astype(BF16)
        for n0 in range(0, D_MODEL, EXPERT_K_CHUNK):
            y_buf[slot, :, n0:n0 + EXPERT_K_CHUNK] = _dot(act, wd16[:, n0:n0 + EXPERT_K_CHUNK])
            issue_some()

    @pl.when(i == n_act - 1)
    def _():
        for c in scatters(i, slot, True):
            c.start()
        for c in gathers(i, 1 - slot):
            c.wait()
        for c in scatters(jnp.maximum(i - 1, 0), 1 - slot, i >= 1):
            c.wait()
        for c in scatters(i, slot, True):
            c.wait()


def _experts(blk_e, blk_off, blk_cnt, n_act, order, hn, w_gate, w_up, w_down):
    n_blocks = blk_e.shape[0]
    n_tok = hn.shape[0]
    rows = EXPERT_ROWS
    w_spec = lambda shape: pl.BlockSpec((1,) + shape, lambda i, be, *_: (be[i], 0, 0))
    row_buf = pltpu.VMEM((2, rows, D_MODEL), F32)
    grid_spec = pltpu.PrefetchScalarGridSpec(
        num_scalar_prefetch=5,
        grid=(n_blocks,),
        in_specs=[pl.BlockSpec(memory_space=pl.ANY),
                  w_spec((D_MODEL, D_EXPERT)), w_spec((D_MODEL, D_EXPERT)), w_spec((D_EXPERT, D_MODEL))],
        out_specs=pl.BlockSpec(memory_space=pl.ANY),
        scratch_shapes=[row_buf, row_buf,
                        pltpu.VMEM((D_MODEL, D_EXPERT), BF16), pltpu.VMEM((D_MODEL, D_EXPERT), BF16),
                        pltpu.VMEM((D_EXPERT, D_MODEL), BF16),
                        pltpu.SemaphoreType.DMA((2,)), pltpu.SemaphoreType.DMA((2,))],
    )
    return pl.pallas_call(
        functools.partial(_experts_kernel, n_tok=n_tok),
        grid_spec=grid_spec,
        out_shape=jax.ShapeDtypeStruct((TOP_K_EXPERTS * n_tok + 2 * rows, D_MODEL), F32),
        compiler_params=_params(("arbitrary",)),
        name="experts",
    )(blk_e, blk_off, blk_cnt, n_act, order, hn, w_gate, w_up, w_down)


def _combine_kernel(h_ref, gate_ref, gfin_ref, *rest, final_norm, n_prompt_tiles):
    yp_ref, ys_ref = rest[-2:]
    y = h_ref[...]
    for k, e_ref in enumerate(rest[:-2]):
        y = y + e_ref[...] * gate_ref[:, k:k + 1]
    y = _rmsnorm_rows(y, gfin_ref[...]) if final_norm else y
    i = pl.program_id(0)

    @pl.when(i < n_prompt_tiles)
    def _():
        yp_ref[...] = y

    @pl.when(i >= n_prompt_tiles)
    def _():
        ys_ref[...] = y


def _combine(h, gates, g_final, ys, tt, n_prompt, final_norm):
    n_tok = h.shape[0]
    n_tiles, n_p = n_tok // tt, n_prompt // tt
    assert n_tok == n_tiles * tt and n_prompt == n_p * tt and n_tiles > n_p
    e_spec = lambda k: pl.BlockSpec((tt, D_MODEL), lambda i: (k * n_tiles + i, 0))
    return pl.pallas_call(
        functools.partial(_combine_kernel, final_norm=final_norm, n_prompt_tiles=n_p),
        grid=(n_tiles,),
        in_specs=[pl.BlockSpec((tt, D_MODEL), lambda i: (i, 0)),
                  pl.BlockSpec((tt, ROUTER_LANES), lambda i: (i, 0)),
                  pl.BlockSpec((1, D_MODEL), lambda i: (0, 0))] + [e_spec(k) for k in range(TOP_K_EXPERTS)],
        out_specs=[pl.BlockSpec((tt, D_MODEL), lambda i: (jnp.minimum(i, n_p - 1), 0)),
                   pl.BlockSpec((tt, D_MODEL), lambda i: (jnp.maximum(i - n_p, 0), 0))],
        out_shape=[jax.ShapeDtypeStruct((n_prompt, D_MODEL), F32),
                   jax.ShapeDtypeStruct((n_tok - n_prompt, D_MODEL), F32)],
        compiler_params=_params(("arbitrary",)),
        name="combine",
    )(h, gates, g_final, *([ys] * TOP_K_EXPERTS))


def _moe(h, hn, logits, g_final, w_gate, w_up, w_down, tt, n_prompt, final_norm):
    n_tok = h.shape[0]
    eid, gates = _router(logits)
    flat_e = eid.reshape(-1)
    nk = flat_e.shape[0]
    rows = EXPERT_ROWS
    _, order = lax.sort_key_val(flat_e, jnp.arange(nk, dtype=jnp.int32))
    experts = jnp.arange(N_EXPERTS, dtype=jnp.int32)
    counts = jnp.sum((flat_e[:, None] == experts[None, :]).astype(jnp.int32), axis=0)
    inclusive = lambda v: jnp.sum(jnp.where(experts[:, None] <= experts[None, :], v[:, None], 0), axis=0)
    start = inclusive(counts) - counts
    blocks_per_e = (counts + rows - 1) // rows
    blk_end = inclusive(blocks_per_e)
    blk_begin = blk_end - blocks_per_e
    n_blocks = -(-(nk + N_EXPERTS * (rows - 1)) // rows)
    blk = jnp.arange(n_blocks, dtype=jnp.int32)
    mine = (blk[:, None] >= blk_begin[None, :]) & (blk[:, None] < blk_end[None, :])
    pick = lambda v: jnp.sum(jnp.where(mine, v[None, :], 0), axis=1)
    first = (blk - pick(blk_begin)) * rows
    blk_cnt = jnp.clip(pick(counts) - first, 0, rows)
    blk_off = jnp.clip(pick(start) + first, 0, nk)
    order = jnp.concatenate([order, jnp.zeros((rows,), jnp.int32)])
    ys = _experts(pick(experts), blk_off, blk_cnt, blk_end[-1:], order, hn, w_gate, w_up, w_down)
    return _combine(h, gates, g_final, ys, tt, n_prompt, final_norm)


def _split_router(w_rg, b_rg, w_re, b_re):
    w = jnp.concatenate([w_rg, w_re], axis=1)
    w = jnp.pad(w, ((0, 0), (0, ROUTER_LANES - w.shape[1])))
    b = jnp.pad(jnp.concatenate([b_rg, b_re]), (0, ROUTER_LANES - N_GROUPS - N_EXPERTS)).reshape(1, ROUTER_LANES)
    w_hi = w.astype(BF16)
    w_lo = (w - w_hi.astype(F32)).astype(BF16)
    return w_hi, w_lo, b


def _layer_weights(l, g_mix, w_in, w_up_a, w_up_b, w_out, g_ffn, w_rg, b_rg, w_re, b_re, w_e_gate, w_e_up, w_e_down):
    n_rope = A_WIDTH + 2 * D_HEAD + IDX_WIDTH + D_IDX + N_IDX_HEADS
    w = w_in[l]
    w_a = jnp.pad(w[:, :n_rope], ((0, 0), (0, -n_rope % LANES))).astype(BF16)
    w_bg = w[:, n_rope:].astype(BF16)
    return dict(
        g_mix=g_mix[l].reshape(1, D_MODEL), w_a=w_a, w_bg=w_bg,
        w_up_a=w_up_a[l].astype(BF16), w_up_b=w_up_b[l].astype(BF16), w_out=w_out[l].astype(BF16),
        g_ffn=g_ffn[l].reshape(1, D_MODEL), router=_split_router(w_rg[l], b_rg[l], w_re[l], b_re[l]),
        w_e_gate=w_e_gate[l], w_e_up=w_e_up[l], w_e_down=w_e_down[l])


def _project(x2d, wts, pos, tm_rope, tm_plain, transposed_k_idx=False):
    tabs = _rope_tables(pos, D_HEAD) + _rope_tables(pos, D_IDX)
    roped = _inproj_rope(x2d, wts["g_mix"], wts["w_a"], tabs, tm_rope, pos.shape[0] // tm_rope, transposed_k_idx)
    return (tuple(roped),) + tuple(_inproj_plain(x2d, wts["g_mix"], wts["w_bg"], tm_plain))


def _pad_rows(a, n):
    return jnp.pad(a, ((0, 0), (0, n - a.shape[1]), (0, 0)))


def kernel(x_prompt, x_sample, cache_a_k, cache_a_v, cache_idx_k, cache_b_k, cache_b_v, page_table,
           g_mix, w_in, w_up_a, w_up_b, w_out, g_ffn, w_rg, b_rg, w_re, b_re,
           w_e_gate, w_e_up, w_e_down, g_final):
    depth = w_in.shape[0]
    batch, seq, _ = x_prompt.shape
    dec_batch, dec_seq, _ = x_sample.shape
    n_pages = page_table.shape[1]
    past = n_pages * PAGE_SIZE
    n_phys = cache_a_k.shape[1]
    gfin = g_final.reshape(1, D_MODEL)
    hp = x_prompt.reshape(batch * seq, D_MODEL)
    hs = x_sample.reshape(dec_batch * dec_seq, D_MODEL)
    rows_s = dec_batch * dec_seq
    new_p, new_s = [], []
    for l in range(depth):
        wts = _layer_weights(l, g_mix, w_in, w_up_a, w_up_b, w_out, g_ffn, w_rg, b_rg, w_re, b_re,
                             w_e_gate, w_e_up, w_e_down)
        last = l == depth - 1

        (q_a, k_a, v_a, q_i, kiwi, k_idx_t), k_b, v_b, qs = _project(hp, wts, np.arange(seq), 512, 1024, True)
        per_seq = lambda a: a.reshape(batch, seq, a.shape[-1])
        o_a = _dsa_prompt(per_seq(q_i), per_seq(kiwi), per_seq(q_a), per_seq(k_a), per_seq(v_a),
                          batch, seq, 256).reshape(batch * seq, A_WIDTH)
        o_b = _sb_prompt(qs, k_b, v_b, batch, seq, 512)
        prompt_rows = (hp, o_a, o_b, qs)
        new_p.append((k_a.reshape(batch, seq, D_HEAD), v_a.reshape(batch, seq, D_HEAD),
                      jnp.swapaxes(k_idx_t, 1, 2),
                      k_b.reshape(batch, seq, N_HEADS_B, D_HEAD), v_b.reshape(batch, seq, N_HEADS_B, D_HEAD)))

        pos_s = np.tile(past + np.arange(dec_seq), dec_batch)
        (q_a, k_a, v_a, q_i, kiwi), k_b, v_b, qs = _project(hs, wts, pos_s, rows_s, rows_s)
        q_b = qs[:, 2 * D_MODEL:]
        per_b = lambda a: a.reshape(dec_batch, dec_seq, -1)
        heads_first = lambda a, nh, d: (per_b(a).reshape(dec_batch, dec_seq, nh, d).transpose(0, 2, 1, 3)
                                        .reshape(dec_batch, nh * dec_seq, d))
        w_idx = per_b(kiwi)[:, :, D_IDX:D_IDX + N_IDX_HEADS] * ((N_IDX_HEADS * D_IDX) ** -0.5)
        wi_s = w_idx.transpose(0, 2, 1).reshape(dec_batch, N_IDX_HEADS * dec_seq, 1)
        kin_s = _pad_rows(per_b(kiwi)[:, :, :D_IDX], LANES)
        o_a = _dsa_sample(page_table, heads_first(q_i, N_IDX_HEADS, D_IDX), wi_s, kin_s,
                          heads_first(q_a, N_HEADS_A, D_HEAD), _pad_rows(per_b(k_a), LANES),
                          _pad_rows(per_b(v_a), LANES), jnp.swapaxes(cache_idx_k[l], 1, 2),
                          cache_a_k[l], cache_a_v[l])
        blk_lanes = SB_BLOCK_KEYS * N_HEADS_B
        kb_new = _pad_rows(k_b.reshape(dec_batch, dec_seq * N_HEADS_B, D_HEAD), blk_lanes)
        vb_new = _pad_rows(v_b.reshape(dec_batch, dec_seq * N_HEADS_B, D_HEAD), blk_lanes)
        pool_rows = lambda c: c.reshape(n_phys, PAGE_SIZE * N_HEADS_B, D_HEAD)
        o_b = _sb_sample(page_table, heads_first(q_b, N_HEADS_B, D_HEAD), kb_new, vb_new,
                         pool_rows(cache_b_k[l]), pool_rows(cache_b_v[l]))
        tok_major = lambda o, nh: (o.reshape(dec_batch, nh, dec_seq, D_HEAD).transpose(0, 2, 1, 3)
                                   .reshape(rows_s, nh * D_HEAD).astype(BF16))
        sample_rows = (hs, tok_major(o_a, N_HEADS_A), tok_major(o_b, N_HEADS_B), qs)
        new_s.append((k_a.reshape(dec_batch, dec_seq, D_HEAD), v_a.reshape(dec_batch, dec_seq, D_HEAD),
                      kiwi[:, :D_IDX].reshape(dec_batch, dec_seq, D_IDX),
                      k_b.reshape(dec_batch, dec_seq, N_HEADS_B, D_HEAD),
                      v_b.reshape(dec_batch, dec_seq, N_HEADS_B, D_HEAD)))

        h, hn, logits = _merge(prompt_rows, sample_rows, wts["w_up_a"], wts["w_up_b"], wts["w_out"], wts["g_ffn"],
                               *wts["router"], MERGE_ROWS)
        hp, hs = _moe(h, hn, logits, gfin, wts["w_e_gate"], wts["w_e_up"], wts["w_e_down"], MERGE_ROWS,
                      batch * seq, last)

    stack = lambda rows_, i: jnp.stack([r[i] for r in rows_])
    return (hp.reshape(batch, seq, D_MODEL), hs.reshape(dec_batch, dec_seq, D_MODEL),
            stack(new_p, 0), stack(new_p, 1), stack(new_p, 2), stack(new_p, 3), stack(new_p, 4),
            stack(new_s, 0), stack(new_s, 1), stack(new_s, 2), stack(new_s, 3), stack(new_s, 4))
```

```python
import functools
import math

import numpy as np
import jax
import jax.numpy as jnp
from jax import lax
from jax.experimental import pallas as pl
from jax.experimental.pallas import tpu as pltpu

D_MODEL = 2048
D_HEAD = 128
N_HEADS_A = 8
N_IDX_HEADS = 16
D_IDX = 64
TOPK_MAX = 256
N_HEADS_B = 8
ROPE_THETA = 10000.0
N_GROUPS = 4
EXPERTS_PER_GROUP = 8
N_EXPERTS = N_GROUPS * EXPERTS_PER_GROUP
TOP_K_EXPERTS = 2
D_EXPERT = 512
PAGE_SIZE = 128
NORM_EPS = 1e-6
A_WIDTH = N_HEADS_A * D_HEAD
B_WIDTH = N_HEADS_B * D_HEAD
IDX_WIDTH = N_IDX_HEADS * D_IDX

LANES = 128
VMEM_LIMIT = 56 * 1024 * 1024
EXPERT_ROWS = 256
EXPERT_K_CHUNK = 256
SB_HEADS_PER_STEP = 2
MERGE_ROWS = 256
ROUTER_LANES = 128

assert TOP_K_EXPERTS == 2

F32 = jnp.float32
BF16 = jnp.bfloat16
NEG_INF = float("-inf")
INT_MIN = -2 ** 31


def _dot(a, b):
    return jnp.dot(a, b, preferred_element_type=F32)


def _dot_nt(a, b):
    return lax.dot_general(a, b, (((1,), (1,)), ((), ())), preferred_element_type=F32)


def _params(semantics):
    return pltpu.CompilerParams(dimension_semantics=semantics, vmem_limit_bytes=VMEM_LIMIT)


def _rope_tables(pos, d):
    half = d // 2
    inv_freq = np.exp(np.arange(half, dtype=np.float32) * np.float32(-2.0 * math.log(ROPE_THETA) / d))
    ang = np.asarray(pos, np.float32)[:, None] * inv_freq[None, :]
    cos, sin = np.cos(ang), np.sin(ang)
    reps = LANES // d
    cos_t = np.tile(np.concatenate([cos, cos], axis=-1), (1, reps))
    sin_t = np.tile(np.concatenate([-sin, sin], axis=-1), (1, reps))
    return jnp.asarray(cos_t, F32), jnp.asarray(sin_t, F32)


def _swap_halves(blk, d):
    if d == LANES:
        return pltpu.roll(blk, LANES // 2, 1)
    lane = lax.broadcasted_iota(jnp.int32, blk.shape, 1)
    first_half = (lane % d) < (d // 2)
    return jnp.where(first_half, pltpu.roll(blk, LANES - d // 2, 1), pltpu.roll(blk, d // 2, 1))


def _rmsnorm_rows(x, g):
    inv = lax.rsqrt(jnp.mean(x * x, axis=-1, keepdims=True) + NORM_EPS)
    return x * inv * g


def _inproj_rope_kernel(x_ref, g_ref, w_ref, cos_ref, sin_ref, cosi_ref, sini_ref,
                        qa_ref, ka_ref, va_ref, qi_ref, kiwi_ref, kit_ref=None):
    xn = _rmsnorm_rows(x_ref[...], g_ref[...]).astype(BF16)
    y = _dot(xn, w_ref[...])
    cos, sin = cos_ref[...], sin_ref[...]
    for h in range(N_HEADS_A):
        blk = y[:, h * D_HEAD:(h + 1) * D_HEAD]
        qa_ref[:, h * D_HEAD:(h + 1) * D_HEAD] = (blk * cos + _swap_halves(blk, D_HEAD) * sin).astype(BF16)
    ka = y[:, A_WIDTH:A_WIDTH + D_HEAD]
    ka_ref[...] = ka * cos + _swap_halves(ka, D_HEAD) * sin
    va_ref[...] = y[:, A_WIDTH + D_HEAD:A_WIDTH + 2 * D_HEAD]
    cosi, sini = cosi_ref[...], sini_ref[...]
    base = A_WIDTH + 2 * D_HEAD
    for j in range(IDX_WIDTH // LANES):
        blk = y[:, base + j * LANES:base + (j + 1) * LANES]
        qi_ref[:, j * LANES:(j + 1) * LANES] = (blk * cosi + _swap_halves(blk, D_IDX) * sini).astype(BF16)
    blk = y[:, base + IDX_WIDTH:base + IDX_WIDTH + LANES]
    lane = lax.broadcasted_iota(jnp.int32, blk.shape, 1)
    is_k = lane < D_IDX
    kiwi = blk * jnp.where(is_k, cosi, 1.0) + _swap_halves(blk, D_IDX) * jnp.where(is_k, sini, 0.0)
    kiwi_ref[...] = kiwi
    if kit_ref is not None:
        kit_ref[0] = kiwi.T[:D_IDX, :]


def _inproj_rope(x, g, w_a, tabs, tm, n_pos_blocks, transposed_k_idx):
    rows = x.shape[0]
    wa_cols = w_a.shape[1]
    cos, sin, cosi, sini = tabs
    row_spec = lambda width: pl.BlockSpec((tm, width), lambda i: (i, 0))
    tab_spec = pl.BlockSpec((tm, LANES), lambda i: (i % n_pos_blocks, 0))
    out_specs = [row_spec(A_WIDTH), row_spec(D_HEAD), row_spec(D_HEAD), row_spec(IDX_WIDTH), row_spec(LANES)]
    out_shape = [jax.ShapeDtypeStruct((rows, A_WIDTH), BF16),
                 jax.ShapeDtypeStruct((rows, D_HEAD), F32),
                 jax.ShapeDtypeStruct((rows, D_HEAD), F32),
                 jax.ShapeDtypeStruct((rows, IDX_WIDTH), BF16),
                 jax.ShapeDtypeStruct((rows, LANES), F32)]
    if transposed_k_idx:
        out_specs.append(pl.BlockSpec((1, D_IDX, tm), lambda i: (i // n_pos_blocks, 0, i % n_pos_blocks)))
        out_shape.append(jax.ShapeDtypeStruct((rows // (n_pos_blocks * tm), D_IDX, n_pos_blocks * tm), F32))
    return pl.pallas_call(
        _inproj_rope_kernel,
        grid=(rows // tm,),
        in_specs=[row_spec(D_MODEL),
                  pl.BlockSpec((1, D_MODEL), lambda i: (0, 0)),
                  pl.BlockSpec((D_MODEL, wa_cols), lambda i: (0, 0)),
                  tab_spec, tab_spec, tab_spec, tab_spec],
        out_specs=out_specs,
        out_shape=out_shape,
        compiler_params=_params(("arbitrary",)),
        name="inproj_rope",
    )(x, g, w_a, cos, sin, cosi, sini)


N_GATE_TILES = 2 * D_MODEL // B_WIDTH


def _inproj_plain_kernel(x_ref, g_ref, w_ref, kb_ref, vb_ref, qs_ref, xn_ref):
    j = pl.program_id(1)

    @pl.when(j == 0)
    def _():
        xn_ref[...] = _rmsnorm_rows(x_ref[...], g_ref[...]).astype(BF16)

    product = lambda: _dot(xn_ref[...], w_ref[...])

    @pl.when(j == 0)
    def _():
        qs_ref[...] = product().astype(BF16)

    @pl.when(j == 1)
    def _():
        kb_ref[...] = product()

    @pl.when(j == 2)
    def _():
        vb_ref[...] = product()

    @pl.when(j >= 3)
    def _():
        qs_ref[...] = (1.0 / (1.0 + jnp.exp(-product()))).astype(BF16)


def _inproj_plain(x, g, w_bg, tm):
    rows = x.shape[0]
    tn = B_WIDTH
    n_col = w_bg.shape[1] // tn
    assert n_col == 3 + N_GATE_TILES
    fixed = pl.BlockSpec((tm, tn), lambda i, j: (i, 0))
    qs_map = lambda i, j: (i, jnp.where(j < 3, N_GATE_TILES, j - 3))
    return pl.pallas_call(
        _inproj_plain_kernel,
        grid=(rows // tm, n_col),
        in_specs=[pl.BlockSpec((tm, D_MODEL), lambda i, j: (i, 0)),
                  pl.BlockSpec((1, D_MODEL), lambda i, j: (0, 0)),
                  pl.BlockSpec((D_MODEL, tn), lambda i, j: (0, j))],
        out_specs=[fixed, fixed, pl.BlockSpec((tm, tn), qs_map)],
        out_shape=[jax.ShapeDtypeStruct((rows, tn), F32), jax.ShapeDtypeStruct((rows, tn), F32),
                   jax.ShapeDtypeStruct((rows, (N_GATE_TILES + 1) * tn), BF16)],
        scratch_shapes=[pltpu.VMEM((tm, D_MODEL), BF16)],
        compiler_params=_params(("arbitrary", "arbitrary")),
        name="inproj_plain",
    )(x, g, w_bg)


def _sortable(x):
    b = lax.bitcast_convert_type(x, jnp.int32)
    return b ^ ((b >> 31) & jnp.int32(0x7FFFFFFF))


def _select_topk(keys_ref, neg_ref, valid_fn, k_top):
    rows, width = keys_ref.shape
    n_chunks = width // LANES
    ones = jnp.ones((LANES, LANES), BF16)
    lane = lax.broadcasted_iota(jnp.int32, (rows, LANES), 1)

    def lane_total(acc):
        return _dot(acc.astype(BF16), ones)

    def count(pred):
        acc = jnp.zeros((rows, LANES), F32)
        for c in range(n_chunks):
            acc = acc + jnp.where(pred(keys_ref[:, c * LANES:(c + 1) * LANES], c), 1.0, 0.0)
        return lane_total(acc)

    def value_step(i, ans):
        cand_u = ans | lax.shift_left(jnp.int32(1), 31 - i)
        cand = cand_u ^ jnp.int32(INT_MIN)
        cnt = count(lambda k, c: k >= cand)
        return jnp.where(cnt >= k_top, cand_u, ans)

    thr = lax.fori_loop(0, 32, value_step, jnp.zeros((rows, LANES), jnp.int32)) ^ jnp.int32(INT_MIN)
    n_gt = count(lambda k, c: k > thr)
    n_ge = count(lambda k, c: k >= thr)
    need = k_top - n_gt

    n_bits = (width - 1).bit_length()

    def index_step(i, cst):
        cand = cst | lax.shift_left(jnp.int32(1), n_bits - 1 - i)
        cnt = count(lambda k, c: (k == thr) & (lane + c * LANES < cand))
        return jnp.where(cnt < need, cand, cst)

    cut = lax.cond(jnp.max(n_ge) > k_top,
                   lambda: lax.fori_loop(0, n_bits, index_step, jnp.zeros((rows, LANES), jnp.int32)),
                   lambda: jnp.full((rows, LANES), width, jnp.int32))

    for c in range(n_chunks):
        k = keys_ref[:, c * LANES:(c + 1) * LANES]
        tie_keep = jnp.where(lane + c * LANES <= cut, 0.0, NEG_INF)
        keep = jnp.where(k > thr, 0.0, jnp.where(k == thr, tie_keep, NEG_INF))
        neg_ref[:, c * LANES:(c + 1) * LANES] = jnp.where(valid_fn(c), keep, NEG_INF)


def _dsa_prompt_kernel(qi_ref, wq_ref, kk_ref, qa_ref, ka_ref, va_ref, *rest, k_top, q_start):
    _, o_ref, keys_ref, neg_ref = rest
    tq, width = keys_ref.shape

    w = wq_ref[0, :, D_IDX:D_IDX + N_IDX_HEADS] * ((N_IDX_HEADS * D_IDX) ** -0.5)
    k_idx = kk_ref[0, :, :D_IDX].astype(BF16)
    score = jnp.zeros((tq, width), F32)
    for h in range(N_IDX_HEADS):
        rel = jnp.maximum(_dot_nt(qi_ref[0, :, h * D_IDX:(h + 1) * D_IDX], k_idx), 0.0)
        score = score + rel * w[:, h:h + 1]

    t_pos = q_start + lax.broadcasted_iota(jnp.int32, (tq, LANES), 0)
    lane = lax.broadcasted_iota(jnp.int32, (tq, LANES), 1)
    causal = lambda c: lane + c * LANES <= t_pos
    s_pos = lax.broadcasted_iota(jnp.int32, (tq, width), 1)
    q_pos = q_start + lax.broadcasted_iota(jnp.int32, (tq, width), 0)
    keys_ref[...] = _sortable(jnp.where(s_pos <= q_pos, score, NEG_INF))
    _select_topk(keys_ref, neg_ref, causal, k_top)

    k_att = ka_ref[0].astype(BF16)
    v_att = va_ref[0].astype(BF16)
    exp2_scale = (D_HEAD ** -0.5) * math.log2(math.e)
    for h in range(N_HEADS_A):
        s = _dot_nt(qa_ref[0, :, h * D_HEAD:(h + 1) * D_HEAD], k_att) + neg_ref[...]
        m = jnp.max(s, axis=-1, keepdims=True)
        e = jnp.exp2((s - m) * exp2_scale)
        l = jnp.sum(e, axis=-1, keepdims=True)
        o = _dot(e.astype(BF16), v_att) / l
        o_ref[0, :, h * D_HEAD:(h + 1) * D_HEAD] = o.astype(BF16)


def _dsa_prompt(q_i, kiwi, q_a, k_a, v_a, batch, seq, tq):
    k_top = min(TOPK_MAX, seq // 4)
    out = jnp.zeros((batch, seq, A_WIDTH), BF16)
    for qb in range(seq // tq):
        width = (qb + 1) * tq
        q_spec = lambda w_: pl.BlockSpec((1, tq, w_), lambda b: (b, qb, 0))
        kv_spec = lambda w_: pl.BlockSpec((1, width, w_), lambda b: (b, 0, 0))
        out = pl.pallas_call(
            functools.partial(_dsa_prompt_kernel, k_top=k_top, q_start=qb * tq),
            grid=(batch,),
            in_specs=[q_spec(IDX_WIDTH), q_spec(LANES), kv_spec(LANES), q_spec(A_WIDTH), kv_spec(D_HEAD),
                      kv_spec(D_HEAD), pl.BlockSpec(memory_space=pl.ANY)],
            out_specs=q_spec(A_WIDTH),
            out_shape=jax.ShapeDtypeStruct((batch, seq, A_WIDTH), BF16),
            scratch_shapes=[pltpu.VMEM((tq, width), jnp.int32), pltpu.VMEM((tq, width), F32)],
            input_output_aliases={6: 0},
            compiler_params=_params(("arbitrary",)),
            name=f"dsa_prompt_q{qb}",
        )(q_i, kiwi, kiwi, q_a, k_a, v_a, out)
    return out


def _suffix_matrix(n, reps=1):
    j = lax.broadcasted_iota(jnp.int32, (n * reps, n * reps), 0)
    s = lax.broadcasted_iota(jnp.int32, (n * reps, n * reps), 1)
    same = (j % reps) == (s % reps)
    later = jnp.where(same & (j // reps > s // reps), 1.0, 0.0)
    total = jnp.where(same, 1.0, 0.0)
    return jnp.concatenate([later, total], axis=1).astype(BF16)


def _softplus(z):
    return jnp.maximum(z, 0.0) + jnp.log(1.0 + jnp.exp(-jnp.abs(z)))


def _sb_prompt_kernel(q_ref, k_ref, v_ref, u_ref, o_ref, k16, v16, acc_ref, later_ref):
    qb = pl.program_id(2)
    tq = q_ref.shape[0]
    sub = LANES
    n_sub = tq // sub

    @pl.when(qb == 0)
    def _():
        k16[...] = k_ref[...].astype(BF16)
        v16[...] = v_ref[...].astype(BF16)

    u = u_ref[...]
    n_heads = q_ref.shape[1] // D_HEAD

    def block(kb, diagonal):
        start = pl.multiple_of(kb * tq, tq)
        for hh in range(n_heads):
            cols = slice(hh * D_HEAD, (hh + 1) * D_HEAD)
            z = _dot_nt(q_ref[:, cols], k16[pl.ds(start, tq), cols]) * (D_HEAD ** -0.5)
            sp = _softplus(z)
            log_beta = z - sp
            if diagonal:
                before = (lax.broadcasted_iota(jnp.int32, (tq, tq), 1)
                          < lax.broadcasted_iota(jnp.int32, (tq, tq), 0))
                sp = jnp.where(before, sp, 0.0)
            sp16 = sp.astype(BF16)
            later = later_ref[:, cols]
            args = [None] * n_sub
            for j in reversed(range(n_sub)):
                cum = _dot(sp16[:, j * sub:(j + 1) * sub], u)
                args[j] = log_beta[:, j * sub:(j + 1) * sub] - cum[:, :sub] - later
                later = later + cum[:, sub:]
            a = jnp.exp(jnp.concatenate(args, axis=1))
            if diagonal:
                a = jnp.where(before, a, 0.0)
            acc_ref[:, cols] += _dot(a.astype(BF16), v16[pl.ds(start, tq), cols])
            later_ref[:, cols] = later

    acc_ref[...] = jnp.zeros_like(acc_ref)
    later_ref[...] = jnp.zeros_like(later_ref)
    block(qb, True)

    def body(it, carry):
        block(qb - 1 - it, False)
        return carry

    lax.fori_loop(0, qb, body, 0)
    o_ref[...] = acc_ref[...].astype(BF16)


def _sb_prompt(qs, k_b, v_b, batch, seq, tq):
    rows = batch * seq
    nq = seq // tq
    hw = SB_HEADS_PER_STEP * D_HEAD
    q_col = (qs.shape[1] - B_WIDTH) // hw
    q_spec = pl.BlockSpec((tq, hw), lambda b, h, q: (b * nq + q, q_col + h))
    kv_spec = pl.BlockSpec((seq, hw), lambda b, h, q: (b, h))
    return pl.pallas_call(
        _sb_prompt_kernel,
        grid=(batch, N_HEADS_B // SB_HEADS_PER_STEP, nq),
        in_specs=[q_spec, kv_spec, kv_spec, pl.BlockSpec((LANES, 2 * LANES), lambda b, h, q: (0, 0))],
        out_specs=pl.BlockSpec((tq, hw), lambda b, h, q: (b * nq + q, h)),
        out_shape=jax.ShapeDtypeStruct((rows, B_WIDTH), BF16),
        scratch_shapes=[pltpu.VMEM((seq, hw), BF16), pltpu.VMEM((seq, hw), BF16),
                        pltpu.VMEM((tq, hw), F32), pltpu.VMEM((tq, hw), F32)],
        compiler_params=_params(("arbitrary", "arbitrary", "arbitrary")),
        name="sb_prompt",
    )(qs, k_b, v_b, _suffix_matrix(LANES))


def _page_copies(pt_ref, batch_idx, first_page, n_pages, pools, bufs, slot, sem, rows_per_page, n_logical):
    copies = []
    for p in range(n_pages):
        phys = pt_ref[batch_idx * n_logical + first_page + p]
        for pool, buf in zip(pools, bufs):
            copies.append(pltpu.make_async_copy(
                pool.at[phys], buf.at[slot, pl.ds(p * rows_per_page, rows_per_page)], sem.at[slot]))
    return copies


def _paged_double_buffer(pt_ref, pools, bufs, sem, n_pages, rows_per_page):
    b = pl.program_id(0)
    slot = b % 2
    copies = lambda bb, ss: _page_copies(pt_ref, bb, 0, n_pages, pools, bufs, ss, sem, rows_per_page, n_pages)

    @pl.when(b == 0)
    def _():
        for c in copies(0, 0):
            c.start()

    @pl.when(b + 1 < pl.num_programs(0))
    def _():
        for c in copies(b + 1, 1 - slot):
            c.start()

    for c in copies(b, slot):
        c.wait()
    return slot


def _new_key_visible(n_q, n_new, rows):
    t_idx = lax.broadcasted_iota(jnp.int32, (rows, LANES), 0) % n_q
    lane = lax.broadcasted_iota(jnp.int32, (rows, LANES), 1)
    return (lane <= t_idx) & (lane < n_new)


def _idx_scores_sample_kernel(pt_ref, qi_ref, w_ref, kin_ref, idx_pool, s_ref, idx_buf, sem, *, n_pages, n_new):
    b = pl.program_id(0)
    slot = b % 2
    past = n_pages * PAGE_SIZE

    def copies(bb, ss):
        return [pltpu.make_async_copy(idx_pool.at[pt_ref[bb * n_pages + p]],
                                      idx_buf.at[ss, :, pl.ds(p * PAGE_SIZE, PAGE_SIZE)], sem.at[ss])
                for p in range(n_pages)]

    @pl.when(b == 0)
    def _():
        for c in copies(0, 0):
            c.start()

    @pl.when(b + 1 < pl.num_programs(0))
    def _():
        for c in copies(b + 1, 1 - slot):
            c.start()

    for c in copies(b, slot):
        c.wait()

    q = qi_ref[0]
    w = w_ref[0]
    n_q = q.shape[0] // N_IDX_HEADS

    def head_sum(qk):
        rel = jnp.maximum(qk, 0.0) * w
        return functools.reduce(lambda a, c: a + c, [rel[h * n_q:(h + 1) * n_q] for h in range(N_IDX_HEADS)])

    s_ref[0, :, :past] = head_sum(_dot(q, idx_buf[slot].astype(BF16)))
    s_new = head_sum(_dot_nt(q, kin_ref[0].astype(BF16)))
    s_ref[0, :, past:] = jnp.where(_new_key_visible(n_q, n_new, n_q), s_new, NEG_INF)


def _topk_mask_kernel(s_ref, neg_ref, keys_ref, *, k_top, n_q, n_new):
    rows, width = keys_ref.shape
    keys_ref[...] = _sortable(s_ref[...])
    visible = _new_key_visible(n_q, n_new, rows)
    last = width // LANES - 1
    _select_topk(keys_ref, neg_ref, lambda c: visible if c == last else jnp.full((rows, LANES), True), k_top)


def _dsa_sample_attend_kernel(pt_ref, qa_ref, neg_ref, kan_ref, van_ref, k_pool, v_pool, o_ref,
                              k_buf, v_buf, sem, *, n_pages):
    slot = _paged_double_buffer(pt_ref, (k_pool, v_pool), (k_buf, v_buf), sem, n_pages, PAGE_SIZE)
    past = n_pages * PAGE_SIZE
    k_att = k_buf[slot].astype(BF16)
    v_att = v_buf[slot].astype(BF16)
    q = qa_ref[0]
    neg = jnp.concatenate([neg_ref[0]] * N_HEADS_A, axis=0)
    s_past = _dot_nt(q, k_att) + neg[:, :past]
    s_new = _dot_nt(q, kan_ref[0].astype(BF16)) + neg[:, past:]
    m = jnp.maximum(jnp.max(s_past, axis=-1, keepdims=True), jnp.max(s_new, axis=-1, keepdims=True))
    exp2_scale = (D_HEAD ** -0.5) * math.log2(math.e)
    e_past = jnp.exp2((s_past - m) * exp2_scale)
    e_new = jnp.exp2((s_new - m) * exp2_scale)
    l = jnp.sum(e_past, axis=-1, keepdims=True) + jnp.sum(e_new, axis=-1, keepdims=True)
    o = _dot(e_past.astype(BF16), v_att) + _dot(e_new.astype(BF16), van_ref[0].astype(BF16))
    o_ref[0] = o / l


def _dsa_sample(page_table, q_i, w_i, k_i_new, q_a, k_a_new, v_a_new, idx_pool, k_pool, v_pool):
    batch, n_pages = page_table.shape
    n_q = q_a.shape[1] // N_HEADS_A
    past = n_pages * PAGE_SIZE
    width = past + LANES
    k_top = min(TOPK_MAX, (past + n_q) // 4)
    pt = page_table.reshape(-1)
    blk = lambda shape: pl.BlockSpec((1,) + shape, lambda b, pt_: (b,) + (0,) * len(shape))
    hbm = pl.BlockSpec(memory_space=pl.ANY)

    scores = pl.pallas_call(
        functools.partial(_idx_scores_sample_kernel, n_pages=n_pages, n_new=n_q),
        grid_spec=pltpu.PrefetchScalarGridSpec(
            num_scalar_prefetch=1, grid=(batch,),
            in_specs=[blk((N_IDX_HEADS * n_q, D_IDX)), blk((N_IDX_HEADS * n_q, 1)), blk((LANES, D_IDX)), hbm],
            out_specs=blk((n_q, width)),
            scratch_shapes=[pltpu.VMEM((2, D_IDX, past), F32), pltpu.SemaphoreType.DMA((2,))]),
        out_shape=jax.ShapeDtypeStruct((batch, n_q, width), F32),
        compiler_params=_params(("arbitrary",)),
        name="idx_scores_sample",
    )(pt, q_i, w_i, k_i_new, idx_pool)

    rows = batch * n_q
    tr = min(rows, LANES)
    neg = pl.pallas_call(
        functools.partial(_topk_mask_kernel, k_top=k_top, n_q=n_q, n_new=n_q),
        grid=(rows // tr,),
        in_specs=[pl.BlockSpec((tr, width), lambda i: (i, 0))],
        out_specs=pl.BlockSpec((tr, width), lambda i: (i, 0)),
        out_shape=jax.ShapeDtypeStruct((rows, width), F32),
        scratch_shapes=[pltpu.VMEM((tr, width), jnp.int32)],
        compiler_params=_params(("arbitrary",)),
        name="topk_mask_sample",
    )(scores.reshape(rows, width))

    return pl.pallas_call(
        functools.partial(_dsa_sample_attend_kernel, n_pages=n_pages),
        grid_spec=pltpu.PrefetchScalarGridSpec(
            num_scalar_prefetch=1, grid=(batch,),
            in_specs=[blk((N_HEADS_A * n_q, D_HEAD)), blk((n_q, width)), blk((LANES, D_HEAD)), blk((LANES, D_HEAD)),
                      hbm, hbm],
            out_specs=blk((N_HEADS_A * n_q, D_HEAD)),
            scratch_shapes=[pltpu.VMEM((2, past, D_HEAD), F32), pltpu.VMEM((2, past, D_HEAD), F32),
                            pltpu.SemaphoreType.DMA((2,))]),
        out_shape=jax.ShapeDtypeStruct((batch, N_HEADS_A * n_q, D_HEAD), F32),
        compiler_params=_params(("arbitrary",)),
        name="dsa_sample_attend",
    )(pt, q_a, neg.reshape(batch, n_q, width), k_a_new, v_a_new, k_pool, v_pool)


SB_CHUNK_PAGES = 8
SB_SLOTS = 3
SB_BLOCK_KEYS = 32


def _sb_sample_kernel(pt_ref, q_ref, kn_ref, vn_ref, u_ref, k_pool, v_pool, o_ref,
                      k_buf, v_buf, acc_ref, later_ref, sem, *, n_pages, n_new, total_steps):
    b = pl.program_id(0)
    j = pl.program_id(1)
    n_chunks = pl.num_programs(1)
    step = b * n_chunks + j
    n_steps = total_steps
    n_slots = k_buf.shape[0]
    slot = step % n_slots
    heads = N_HEADS_B
    rows_per_page = PAGE_SIZE * heads
    blk_lanes = SB_BLOCK_KEYS * heads

    def copies(st, ss):
        bb = st // n_chunks
        chunk = n_chunks - 1 - st % n_chunks
        return _page_copies(pt_ref, bb, chunk * SB_CHUNK_PAGES, SB_CHUNK_PAGES, (k_pool, v_pool),
                            (k_buf, v_buf), ss, sem, rows_per_page, n_pages)

    @pl.when(step == 0)
    def _():
        for ahead in range(min(n_slots - 1, total_steps)):
            for c in copies(ahead, ahead):
                c.start()

    @pl.when(step + n_slots - 1 < n_steps)
    def _():
        for c in copies(step + n_slots - 1, (step + n_slots - 1) % n_slots):
            c.start()

    q = q_ref[0]
    n_q = q.shape[0] // heads
    u = u_ref[...]
    scale = D_HEAD ** -0.5

    def head_diag(zt):
        lane_head = lax.broadcasted_iota(jnp.int32, (n_q, zt.shape[1]), 1) % heads
        out = jnp.zeros((n_q, zt.shape[1]), F32)
        for h in range(heads):
            out = out + jnp.where(lane_head == h, zt[h * n_q:(h + 1) * n_q, :], 0.0)
        return out

    def head_spread(a):
        lane_head = lax.broadcasted_iota(jnp.int32, a.shape, 1) % heads
        return jnp.concatenate([jnp.where(lane_head == h, a, 0.0) for h in range(heads)], axis=0).astype(BF16)

    def attend(k16, v16, before, later):
        n = k16.shape[0]
        z = head_diag(_dot_nt(q, k16) * scale)
        sp = _softplus(z)
        log_beta = z - sp
        sp16 = jnp.where(before, sp, 0.0).astype(BF16)
        pieces = []
        for blk in reversed(range(n // blk_lanes)):
            sl = slice(blk * blk_lanes, (blk + 1) * blk_lanes)
            cum = _dot(sp16[:, sl], u)
            pieces.append(log_beta[:, sl] - cum[:, :blk_lanes] - later)
            later = later + cum[:, blk_lanes:]
        a = jnp.exp(jnp.concatenate(pieces[::-1], axis=1))
        a = jnp.where(before, a, 0.0)
        return _dot(head_spread(a), v16), later

    @pl.when(j == 0)
    def _():
        t_idx = lax.broadcasted_iota(jnp.int32, (n_q, blk_lanes), 0)
        key = lax.broadcasted_iota(jnp.int32, (n_q, blk_lanes), 1) // heads
        before = (key < t_idx) & (key < n_new)
        out, later = attend(kn_ref[0].astype(BF16), vn_ref[0].astype(BF16), before,
                            jnp.zeros((n_q, blk_lanes), F32))
        acc_ref[...] = out
        later_ref[...] = later

    for c in copies(step, slot):
        c.wait()

    n_rows = SB_CHUNK_PAGES * rows_per_page
    out, later = attend(k_buf[slot].astype(BF16), v_buf[slot].astype(BF16),
                        jnp.full((n_q, n_rows), True), later_ref[...])
    acc_ref[...] += out
    later_ref[...] = later

    @pl.when(j == n_chunks - 1)
    def _():
        o_ref[0] = acc_ref[...]


def _sb_sample(page_table, q, k_new, v_new, k_pool, v_pool):
    batch, n_pages = page_table.shape
    rows_q = q.shape[1]
    n_q = rows_q // N_HEADS_B
    blk_lanes = SB_BLOCK_KEYS * N_HEADS_B
    n_chunks = n_pages // SB_CHUNK_PAGES
    chunk_rows = SB_CHUNK_PAGES * PAGE_SIZE * N_HEADS_B
    blk = lambda shape: pl.BlockSpec((1,) + shape, lambda b, j, pt: (b,) + (0,) * len(shape))
    hbm = pl.BlockSpec(memory_space=pl.ANY)
    grid_spec = pltpu.PrefetchScalarGridSpec(
        num_scalar_prefetch=1,
        grid=(batch, n_chunks),
        in_specs=[blk((rows_q, D_HEAD)), blk((blk_lanes, D_HEAD)), blk((blk_lanes, D_HEAD)),
                  pl.BlockSpec((blk_lanes, 2 * blk_lanes), lambda b, j, pt: (0, 0)), hbm, hbm],
        out_specs=blk((rows_q, D_HEAD)),
        scratch_shapes=[pltpu.VMEM((SB_SLOTS, chunk_rows, D_HEAD), F32), pltpu.VMEM((SB_SLOTS, chunk_rows, D_HEAD), F32),
                        pltpu.VMEM((rows_q, D_HEAD), F32), pltpu.VMEM((n_q, blk_lanes), F32),
                        pltpu.SemaphoreType.DMA((SB_SLOTS,))],
    )
    return pl.pallas_call(
        functools.partial(_sb_sample_kernel, n_pages=n_pages, n_new=n_q, total_steps=batch * n_chunks),
        grid_spec=grid_spec,
        out_shape=jax.ShapeDtypeStruct((batch, rows_q, D_HEAD), F32),
        compiler_params=_params(("arbitrary", "arbitrary")),
        name="sb_sample",
    )(page_table.reshape(-1), q, k_new, v_new, _suffix_matrix(SB_BLOCK_KEYS, N_HEADS_B), k_pool, v_pool)


def _merge_kernel(*refs, n_prompt_tiles):
    prompt, sample = refs[0:4], refs[4:8]
    wua_ref, wub_ref, wo_ref, gf_ref, wr_hi_ref, wr_lo_ref, br_ref, h_ref, hn_ref, lg_ref = refs[8:]

    def tile(x_ref, oa_ref, ob_ref, sig_ref):
        sig_a = sig_ref[:, :D_MODEL].astype(F32)
        sig_b = sig_ref[:, D_MODEL:].astype(F32)
        merged = sig_a * _dot(oa_ref[...], wua_ref[...]) + sig_b * _dot(ob_ref[...], wub_ref[...])
        h = x_ref[...] + _dot(merged.astype(BF16), wo_ref[...])
        h_ref[...] = h
        hn = _rmsnorm_rows(h, gf_ref[...])
        hn_ref[...] = hn
        hn_hi = hn.astype(BF16)
        hn_lo = (hn - hn_hi.astype(F32)).astype(BF16)
        lg_ref[...] = (_dot(hn_hi, wr_hi_ref[...]) + _dot(hn_lo, wr_hi_ref[...])
                       + _dot(hn_hi, wr_lo_ref[...]) + br_ref[...])

    i = pl.program_id(0)

    @pl.when(i < n_prompt_tiles)
    def _():
        tile(*prompt)

    @pl.when(i >= n_prompt_tiles)
    def _():
        tile(*sample)


def _merge(prompt, sample, w_up_a, w_up_b, w_out, g_ffn, wr_hi, wr_lo, b_r, tm):
    n_p, n_s = prompt[0].shape[0] // tm, sample[0].shape[0] // tm
    assert prompt[0].shape[0] == n_p * tm and sample[0].shape[0] == n_s * tm
    rows = (n_p + n_s) * tm
    widths = (D_MODEL, A_WIDTH, B_WIDTH, 2 * D_MODEL)
    p_spec = lambda width: pl.BlockSpec((tm, width), lambda i: (jnp.minimum(i, n_p - 1), 0))
    s_spec = lambda width: pl.BlockSpec((tm, width), lambda i: (jnp.maximum(i - n_p, 0), 0))
    row_spec = lambda width: pl.BlockSpec((tm, width), lambda i: (i, 0))
    const = lambda shape: pl.BlockSpec(shape, lambda i: (0, 0), pipeline_mode=pl.Buffered(1))
    return pl.pallas_call(
        functools.partial(_merge_kernel, n_prompt_tiles=n_p),
        grid=(n_p + n_s,),
        in_specs=[p_spec(w_) for w_ in widths] + [s_spec(w_) for w_ in widths] + [
                  const((A_WIDTH, D_MODEL)), const((B_WIDTH, D_MODEL)), const((D_MODEL, D_MODEL)),
                  const((1, D_MODEL)), const((D_MODEL, ROUTER_LANES)), const((D_MODEL, ROUTER_LANES)),
                  const((1, ROUTER_LANES))],
        out_specs=[row_spec(D_MODEL), row_spec(D_MODEL), row_spec(ROUTER_LANES)],
        out_shape=[jax.ShapeDtypeStruct((rows, D_MODEL), F32),
                   jax.ShapeDtypeStruct((rows, D_MODEL), F32),
                   jax.ShapeDtypeStruct((rows, ROUTER_LANES), F32)],
        compiler_params=_params(("arbitrary",)),
        name="merge",
    )(*prompt, *sample, w_up_a, w_up_b, w_out, g_ffn, wr_hi, wr_lo, b_r)


def _router_kernel(lg_ref, eid_ref, gate_ref, *, chunk):
    for c0 in range(0, lg_ref.shape[0], chunk):
        _route_chunk(lg_ref, eid_ref, gate_ref, c0, chunk)


def _route_chunk(lg_ref, eid_ref, gate_ref, c0, chunk):
    lg = lg_ref[c0:c0 + chunk, :].T
    row = lambda r: lg[r:r + 1, :]
    groups = [row(g) for g in range(N_GROUPS)]
    g_max = functools.reduce(jnp.maximum, groups)
    g_sel = jnp.full(g_max.shape, N_GROUPS - 1, jnp.int32)
    for g in reversed(range(N_GROUPS)):
        g_sel = jnp.where(groups[g] == g_max, g, g_sel)
    p_group = 1.0 / functools.reduce(lambda a, c: a + c, [jnp.exp(v - g_max) for v in groups])
    inside = []
    for e in range(EXPERTS_PER_GROUP):
        v = jnp.zeros_like(g_max)
        for g in range(N_GROUPS):
            v = jnp.where(g_sel == g, row(N_GROUPS + g * EXPERTS_PER_GROUP + e), v)
        inside.append(v)

    def top1(vals, skip):
        best = functools.reduce(jnp.maximum, [jnp.where(skip == e, NEG_INF, v) for e, v in enumerate(vals)])
        idx = jnp.full(best.shape, EXPERTS_PER_GROUP - 1, jnp.int32)
        for e in reversed(range(EXPERTS_PER_GROUP)):
            idx = jnp.where((vals[e] == best) & (skip != e), e, idx)
        return best, idx

    t1, i1 = top1(inside, jnp.full(g_max.shape, -1, jnp.int32))
    t2, i2 = top1(inside, i1)
    r = jnp.exp(t2 - t1)
    for k, ids in enumerate((g_sel * EXPERTS_PER_GROUP + i1, g_sel * EXPERTS_PER_GROUP + i2)):
        for c in range(chunk // LANES):
            row0 = c0 // LANES + c
            eid_ref[k, row0:row0 + 1, :] = ids[:, c * LANES:(c + 1) * LANES]
    gates = jnp.concatenate([p_group * (1.0 / (1.0 + r)), p_group * (r / (1.0 + r)),
                             jnp.zeros((ROUTER_LANES - TOP_K_EXPERTS, chunk), F32)], axis=0)
    gate_ref[c0:c0 + chunk, :] = gates.T


def _router(logits):
    n_tok = logits.shape[0]
    assert n_tok % LANES == 0
    chunk = max(c for c in range(LANES, 2048 + 1, LANES) if n_tok % c == 0)
    return pl.pallas_call(
        functools.partial(_router_kernel, chunk=chunk),
        out_shape=[jax.ShapeDtypeStruct((TOP_K_EXPERTS, n_tok // LANES, LANES), jnp.int32),
                   jax.ShapeDtypeStruct((n_tok, ROUTER_LANES), F32)],
        compiler_params=pltpu.CompilerParams(vmem_limit_bytes=VMEM_LIMIT),
        name="router",
    )(logits)


def _experts_kernel(blk_e_ref, blk_off_ref, blk_cnt_ref, nact_ref, order_ref,
                    hn_hbm, wg_ref, wu_ref, wd_ref, ys_hbm,
                    x_buf, y_buf, wg16, wu16, wd16, in_sem, out_sem, *, n_tok):
    i = pl.program_id(0)
    n_act = nact_ref[0]
    rows = x_buf.shape[1]
    slot = i % 2
    n_pairs = TOP_K_EXPERTS * n_tok

    def gathers(blk, ss):
        off = blk_off_ref[blk]
        copies = []
        for r in range(rows):
            p = order_ref[off + r]
            tok = jnp.where(p >= n_tok, p - n_tok, p)
            copies.append(pltpu.make_async_copy(hn_hbm.at[pl.ds(tok, 1)], x_buf.at[ss, pl.ds(r, 1)], in_sem.at[ss]))
        return copies

    def scatters(blk, ss, real):
        off = blk_off_ref[blk]
        cnt = jnp.where(real, blk_cnt_ref[blk], 0)
        copies = []
        for r in range(rows):
            dst = jnp.where(r < cnt, order_ref[off + r], n_pairs + ss * rows + r)
            copies.append(pltpu.make_async_copy(y_buf.at[ss, pl.ds(r, 1)], ys_hbm.at[pl.ds(dst, 1)], out_sem.at[ss]))
        return copies

    @pl.when(i == 0)
    def _():
        for c in gathers(0, 0):
            c.start()
        y_buf[1] = jnp.zeros(y_buf.shape[1:], F32)
        for ss in range(2):
            spare = pltpu.make_async_copy(y_buf.at[1], ys_hbm.at[pl.ds(n_pairs + ss * rows, rows)], out_sem.at[1])
            spare.start()
            spare.wait()

    @pl.when((i >= 1) & (i < n_act))
    def _():
        for c in scatters(jnp.maximum(i - 2, 0), slot, i >= 2):
            c.wait()

    first_of_expert = (i == 0) | (blk_e_ref[i] != blk_e_ref[jnp.maximum(i - 1, 0)])

    @pl.when((i < n_act) & first_of_expert)
    def _():
        wg16[...] = wg_ref[0].astype(BF16)
        wu16[...] = wu_ref[0].astype(BF16)
        wd16[...] = wd_ref[0].astype(BF16)

    @pl.when(i < n_act)
    def _():
        for c in gathers(i, slot):
            c.wait()
        reads = gathers(jnp.minimum(i + 1, n_act - 1), 1 - slot)
        writes = scatters(jnp.maximum(i - 1, 0), 1 - slot, i >= 1)
        starts = [(c, prio) for pair in zip(reads, writes) for prio, c in enumerate(pair)]
        n_chunks = 2 * D_MODEL // EXPERT_K_CHUNK
        per_chunk = len(starts) // n_chunks
        chunk_no = 0

        def issue_some():
            nonlocal chunk_no
            for c, prio in starts[chunk_no * per_chunk:(chunk_no + 1) * per_chunk]:
                c.start(priority=prio)
            chunk_no += 1

        g = jnp.zeros((rows, D_EXPERT), F32)
        u = jnp.zeros((rows, D_EXPERT), F32)
        for k0 in range(0, D_MODEL, EXPERT_K_CHUNK):
            xc = x_buf[slot, :, k0:k0 + EXPERT_K_CHUNK].astype(BF16)
            g = g + _dot(xc, wg16[k0:k0 + EXPERT_K_CHUNK, :])
            u = u + _dot(xc, wu16[k0:k0 + EXPERT_K_CHUNK, :])
            issue_some()
        act = (g * (1.0 / (1.0 + jnp.exp(-g))) * u).astype(BF16)
        for n0 in range(0, D_MODEL, EXPERT_K_CHUNK):
            y_buf[slot, :, n0:n0 + EXPERT_K_CHUNK] = _dot(act, wd16[:, n0:n0 + EXPERT_K_CHUNK])
            issue_some()

    @pl.when(i == n_act - 1)
    def _():
        for c in scatters(i, slot, True):
            c.start(priority=1)
        for c in gathers(i, 1 - slot):
            c.wait()
        for c in scatters(jnp.maximum(i - 1, 0), 1 - slot, i >= 1):
            c.wait()
        for c in scatters(i, slot, True):
            c.wait()


def _experts(blk_e, blk_off, blk_cnt, n_act, order, hn, w_gate, w_up, w_down):
    n_blocks = blk_e.shape[0]
    n_tok = hn.shape[0]
    rows = EXPERT_ROWS
    w_spec = lambda shape: pl.BlockSpec((1,) + shape, lambda i, be, *_: (be[i], 0, 0))
    row_buf = pltpu.VMEM((2, rows, D_MODEL), F32)
    grid_spec = pltpu.PrefetchScalarGridSpec(
        num_scalar_prefetch=5,
        grid=(n_blocks,),
        in_specs=[pl.BlockSpec(memory_space=pl.ANY),
                  w_spec((D_MODEL, D_EXPERT)), w_spec((D_MODEL, D_EXPERT)), w_spec((D_EXPERT, D_MODEL))],
        out_specs=pl.BlockSpec(memory_space=pl.ANY),
        scratch_shapes=[row_buf, row_buf,
                        pltpu.VMEM((D_MODEL, D_EXPERT), BF16), pltpu.VMEM((D_MODEL, D_EXPERT), BF16),
                        pltpu.VMEM((D_EXPERT, D_MODEL), BF16),
                        pltpu.SemaphoreType.DMA((2,)), pltpu.SemaphoreType.DMA((2,))],
    )
    return pl.pallas_call(
        functools.partial(_experts_kernel, n_tok=n_tok),
        grid_spec=grid_spec,
        out_shape=jax.ShapeDtypeStruct((TOP_K_EXPERTS * n_tok + 2 * rows, D_MODEL), F32),
        compiler_params=_params(("arbitrary",)),
        name="experts",
    )(blk_e, blk_off, blk_cnt, n_act, order, hn, w_gate, w_up, w_down)


def _combine_kernel(h_ref, gate_ref, gfin_ref, *rest, final_norm, n_prompt_tiles):
    yp_ref, ys_ref = rest[-2:]
    y = h_ref[...]
    for k, e_ref in enumerate(rest[:-2]):
        y = y + e_ref[...] * gate_ref[:, k:k + 1]
    y = _rmsnorm_rows(y, gfin_ref[...]) if final_norm else y
    i = pl.program_id(0)

    @pl.when(i < n_prompt_tiles)
    def _():
        yp_ref[...] = y

    @pl.when(i >= n_prompt_tiles)
    def _():
        ys_ref[...] = y


def _combine(h, gates, g_final, ys, tt, n_prompt, final_norm):
    n_tok = h.shape[0]
    n_tiles, n_p = n_tok // tt, n_prompt // tt
    assert n_tok == n_tiles * tt and n_prompt == n_p * tt and n_tiles > n_p
    e_spec = lambda k: pl.BlockSpec((tt, D_MODEL), lambda i: (k * n_tiles + i, 0))
    return pl.pallas_call(
        functools.partial(_combine_kernel, final_norm=final_norm, n_prompt_tiles=n_p),
        grid=(n_tiles,),
        in_specs=[pl.BlockSpec((tt, D_MODEL), lambda i: (i, 0)),
                  pl.BlockSpec((tt, ROUTER_LANES), lambda i: (i, 0)),
                  pl.BlockSpec((1, D_MODEL), lambda i: (0, 0))] + [e_spec(k) for k in range(TOP_K_EXPERTS)],
        out_specs=[pl.BlockSpec((tt, D_MODEL), lambda i: (jnp.minimum(i, n_p - 1), 0)),
                   pl.BlockSpec((tt, D_MODEL), lambda i: (jnp.maximum(i - n_p, 0), 0))],
        out_shape=[jax.ShapeDtypeStruct((n_prompt, D_MODEL), F32),
                   jax.ShapeDtypeStruct((n_tok - n_prompt, D_MODEL), F32)],
        compiler_params=_params(("arbitrary",)),
        name="combine",
    )(h, gates, g_final, *([ys] * TOP_K_EXPERTS))


def _moe(h, hn, logits, g_final, w_gate, w_up, w_down, tt, n_prompt, final_norm):
    n_tok = h.shape[0]
    eid, gates = _router(logits)
    flat_e = eid.reshape(-1)
    nk = flat_e.shape[0]
    rows = EXPERT_ROWS
    _, order = lax.sort_key_val(flat_e, jnp.arange(nk, dtype=jnp.int32))
    experts = jnp.arange(N_EXPERTS, dtype=jnp.int32)
    counts = jnp.sum((flat_e[:, None] == experts[None, :]).astype(jnp.int32), axis=0)
    inclusive = lambda v: jnp.sum(jnp.where(experts[:, None] <= experts[None, :], v[:, None], 0), axis=0)
    start = inclusive(counts) - counts
    blocks_per_e = (counts + rows - 1) // rows
    blk_end = inclusive(blocks_per_e)
    blk_begin = blk_end - blocks_per_e
    n_blocks = -(-(nk + N_EXPERTS * (rows - 1)) // rows)
    blk = jnp.arange(n_blocks, dtype=jnp.int32)
    mine = (blk[:, None] >= blk_begin[None, :]) & (blk[:, None] < blk_end[None, :])
    pick = lambda v: jnp.sum(jnp.where(mine, v[None, :], 0), axis=1)
    first = (blk - pick(blk_begin)) * rows
    blk_cnt = jnp.clip(pick(counts) - first, 0, rows)
    blk_off = jnp.clip(pick(start) + first, 0, nk)
    order = jnp.concatenate([order, jnp.zeros((rows,), jnp.int32)])
    ys = _experts(pick(experts), blk_off, blk_cnt, blk_end[-1:], order, hn, w_gate, w_up, w_down)
    return _combine(h, gates, g_final, ys, tt, n_prompt, final_norm)


def _split_router(w_rg, b_rg, w_re, b_re):
    w = jnp.concatenate([w_rg, w_re], axis=1)
    w = jnp.pad(w, ((0, 0), (0, ROUTER_LANES - w.shape[1])))
    b = jnp.pad(jnp.concatenate([b_rg, b_re]), (0, ROUTER_LANES - N_GROUPS - N_EXPERTS)).reshape(1, ROUTER_LANES)
    w_hi = w.astype(BF16)
    w_lo = (w - w_hi.astype(F32)).astype(BF16)
    return w_hi, w_lo, b


def _layer_weights(l, g_mix, w_in, w_up_a, w_up_b, w_out, g_ffn, w_rg, b_rg, w_re, b_re, w_e_gate, w_e_up, w_e_down):
    n_rope = A_WIDTH + 2 * D_HEAD + IDX_WIDTH + D_IDX + N_IDX_HEADS
    w = w_in[l]
    w_a = jnp.pad(w[:, :n_rope], ((0, 0), (0, -n_rope % LANES))).astype(BF16)
    w_bg = w[:, n_rope:].astype(BF16)
    return dict(
        g_mix=g_mix[l].reshape(1, D_MODEL), w_a=w_a, w_bg=w_bg,
        w_up_a=w_up_a[l].astype(BF16), w_up_b=w_up_b[l].astype(BF16), w_out=w_out[l].astype(BF16),
        g_ffn=g_ffn[l].reshape(1, D_MODEL), router=_split_router(w_rg[l], b_rg[l], w_re[l], b_re[l]),
        w_e_gate=w_e_gate[l], w_e_up=w_e_up[l], w_e_down=w_e_down[l])


def _project(x2d, wts, pos, tm_rope, tm_plain, transposed_k_idx=False):
    tabs = _rope_tables(pos, D_HEAD) + _rope_tables(pos, D_IDX)
    roped = _inproj_rope(x2d, wts["g_mix"], wts["w_a"], tabs, tm_rope, pos.shape[0] // tm_rope, transposed_k_idx)
    return (tuple(roped),) + tuple(_inproj_plain(x2d, wts["g_mix"], wts["w_bg"], tm_plain))


def _pad_rows(a, n):
    return jnp.pad(a, ((0, 0), (0, n - a.shape[1]), (0, 0)))


def kernel(x_prompt, x_sample, cache_a_k, cache_a_v, cache_idx_k, cache_b_k, cache_b_v, page_table,
           g_mix, w_in, w_up_a, w_up_b, w_out, g_ffn, w_rg, b_rg, w_re, b_re,
           w_e_gate, w_e_up, w_e_down, g_final):
    depth = w_in.shape[0]
    batch, seq, _ = x_prompt.shape
    dec_batch, dec_seq, _ = x_sample.shape
    n_pages = page_table.shape[1]
    past = n_pages * PAGE_SIZE
    n_phys = cache_a_k.shape[1]
    gfin = g_final.reshape(1, D_MODEL)
    hp = x_prompt.reshape(batch * seq, D_MODEL)
    hs = x_sample.reshape(dec_batch * dec_seq, D_MODEL)
    rows_s = dec_batch * dec_seq
    new_p, new_s = [], []
    for l in range(depth):
        wts = _layer_weights(l, g_mix, w_in, w_up_a, w_up_b, w_out, g_ffn, w_rg, b_rg, w_re, b_re,
                             w_e_gate, w_e_up, w_e_down)
        last = l == depth - 1

        (q_a, k_a, v_a, q_i, kiwi, k_idx_t), k_b, v_b, qs = _project(hp, wts, np.arange(seq), 512, 1024, True)
        per_seq = lambda a: a.reshape(batch, seq, a.shape[-1])
        o_a = _dsa_prompt(per_seq(q_i), per_seq(kiwi), per_seq(q_a), per_seq(k_a), per_seq(v_a),
                          batch, seq, 256).reshape(batch * seq, A_WIDTH)
        o_b = _sb_prompt(qs, k_b, v_b, batch, seq, 512)
        prompt_rows = (hp, o_a, o_b, qs)
        new_p.append((k_a.reshape(batch, seq, D_HEAD), v_a.reshape(batch, seq, D_HEAD),
                      jnp.swapaxes(k_idx_t, 1, 2),
                      k_b.reshape(batch, seq, N_HEADS_B, D_HEAD), v_b.reshape(batch, seq, N_HEADS_B, D_HEAD)))

        pos_s = np.tile(past + np.arange(dec_seq), dec_batch)
        (q_a, k_a, v_a, q_i, kiwi), k_b, v_b, qs = _project(hs, wts, pos_s, rows_s, rows_s)
        q_b = qs[:, 2 * D_MODEL:]
        per_b = lambda a: a.reshape(dec_batch, dec_seq, -1)
        heads_first = lambda a, nh, d: (per_b(a).reshape(dec_batch, dec_seq, nh, d).transpose(0, 2, 1, 3)
                                        .reshape(dec_batch, nh * dec_seq, d))
        w_idx = per_b(kiwi)[:, :, D_IDX:D_IDX + N_IDX_HEADS] * ((N_IDX_HEADS * D_IDX) ** -0.5)
        wi_s = w_idx.transpose(0, 2, 1).reshape(dec_batch, N_IDX_HEADS * dec_seq, 1)
        kin_s = _pad_rows(per_b(kiwi)[:, :, :D_IDX], LANES)
        o_a = _dsa_sample(page_table, heads_first(q_i, N_IDX_HEADS, D_IDX), wi_s, kin_s,
                          heads_first(q_a, N_HEADS_A, D_HEAD), _pad_rows(per_b(k_a), LANES),
                          _pad_rows(per_b(v_a), LANES), jnp.swapaxes(cache_idx_k[l], 1, 2),
                          cache_a_k[l], cache_a_v[l])
        blk_lanes = SB_BLOCK_KEYS * N_HEADS_B
        kb_new = _pad_rows(k_b.reshape(dec_batch, dec_seq * N_HEADS_B, D_HEAD), blk_lanes)
        vb_new = _pad_rows(v_b.reshape(dec_batch, dec_seq * N_HEADS_B, D_HEAD), blk_lanes)
        pool_rows = lambda c: c.reshape(n_phys, PAGE_SIZE * N_HEADS_B, D_HEAD)
        o_b = _sb_sample(page_table, heads_first(q_b, N_HEADS_B, D_HEAD), kb_new, vb_new,
                         pool_rows(cache_b_k[l]), pool_rows(cache_b_v[l]))
        tok_major = lambda o, nh: (o.reshape(dec_batch, nh, dec_seq, D_HEAD).transpose(0, 2, 1, 3)
                                   .reshape(rows_s, nh * D_HEAD).astype(BF16))
        sample_rows = (hs, tok_major(o_a, N_HEADS_A), tok_major(o_b, N_HEADS_B), qs)
        new_s.append((k_a.reshape(dec_batch, dec_seq, D_HEAD), v_a.reshape(dec_batch, dec_seq, D_HEAD),
                      kiwi[:, :D_IDX].reshape(dec_batch, dec_seq, D_IDX),
                      k_b.reshape(dec_batch, dec_seq, N_HEADS_B, D_HEAD),
                      v_b.reshape(dec_batch, dec_seq, N_HEADS_B, D_HEAD)))

        h, hn, logits = _merge(prompt_rows, sample_rows, wts["w_up_a"], wts["w_up_b"], wts["w_out"], wts["g_ffn"],
                               *wts["router"], MERGE_ROWS)
        hp, hs = _moe(h, hn, logits, gfin, wts["w_e_gate"], wts["w_e_up"], wts["w_e_down"], MERGE_ROWS,
                      batch * seq, last)

    stack = lambda rows_, i: jnp.stack([r[i] for r in rows_])
    return (hp.reshape(batch, seq, D_MODEL), hs.reshape(dec_batch, dec_seq, D_MODEL),
            stack(new_p, 0), stack(new_p, 1), stack(new_p, 2), stack(new_p, 3), stack(new_p, 4),
            stack(new_s, 0), stack(new_s, 1), stack(new_s, 2), stack(new_s, 3), stack(new_s, 4))
```

```python
import functools
import math

import numpy as np
import jax
import jax.numpy as jnp
from jax import lax
from jax.experimental import pallas as pl
from jax.experimental.pallas import tpu as pltpu

D_MODEL = 2048
D_HEAD = 128
N_HEADS_A = 8
N_IDX_HEADS = 16
D_IDX = 64
TOPK_MAX = 256
N_HEADS_B = 8
ROPE_THETA = 10000.0
N_GROUPS = 4
EXPERTS_PER_GROUP = 8
N_EXPERTS = N_GROUPS * EXPERTS_PER_GROUP
TOP_K_EXPERTS = 2
D_EXPERT = 512
PAGE_SIZE = 128
NORM_EPS = 1e-6
A_WIDTH = N_HEADS_A * D_HEAD
B_WIDTH = N_HEADS_B * D_HEAD
IDX_WIDTH = N_IDX_HEADS * D_IDX

LANES = 128
VMEM_LIMIT = 56 * 1024 * 1024
EXPERT_ROWS = 256
EXPERT_K_CHUNK = 256
SB_HEADS_PER_STEP = 2
MERGE_ROWS = 256
ROUTER_LANES = 128

assert TOP_K_EXPERTS == 2

F32 = jnp.float32
BF16 = jnp.bfloat16
NEG_INF = float("-inf")
INT_MIN = -2 ** 31


def _dot(a, b):
    return jnp.dot(a, b, preferred_element_type=F32)


def _dot_nt(a, b):
    return lax.dot_general(a, b, (((1,), (1,)), ((), ())), preferred_element_type=F32)


def _params(semantics):
    return pltpu.CompilerParams(dimension_semantics=semantics, vmem_limit_bytes=VMEM_LIMIT)


def _rope_tables(pos, d):
    half = d // 2
    inv_freq = np.exp(np.arange(half, dtype=np.float32) * np.float32(-2.0 * math.log(ROPE_THETA) / d))
    ang = np.asarray(pos, np.float32)[:, None] * inv_freq[None, :]
    cos, sin = np.cos(ang), np.sin(ang)
    reps = LANES // d
    cos_t = np.tile(np.concatenate([cos, cos], axis=-1), (1, reps))
    sin_t = np.tile(np.concatenate([-sin, sin], axis=-1), (1, reps))
    return jnp.asarray(cos_t, F32), jnp.asarray(sin_t, F32)


def _swap_halves(blk, d):
    if d == LANES:
        return pltpu.roll(blk, LANES // 2, 1)
    lane = lax.broadcasted_iota(jnp.int32, blk.shape, 1)
    first_half = (lane % d) < (d // 2)
    return jnp.where(first_half, pltpu.roll(blk, LANES - d // 2, 1), pltpu.roll(blk, d // 2, 1))


def _rmsnorm_rows(x, g):
    inv = lax.rsqrt(jnp.mean(x * x, axis=-1, keepdims=True) + NORM_EPS)
    return x * inv * g


def _inproj_rope_kernel(x_ref, g_ref, w_ref, cos_ref, sin_ref, cosi_ref, sini_ref,
                        qa_ref, ka_ref, va_ref, qi_ref, kiwi_ref, kit_ref=None):
    xn = _rmsnorm_rows(x_ref[...], g_ref[...]).astype(BF16)
    y = _dot(xn, w_ref[...])
    cos, sin = cos_ref[...], sin_ref[...]
    for h in range(N_HEADS_A):
        blk = y[:, h * D_HEAD:(h + 1) * D_HEAD]
        qa_ref[:, h * D_HEAD:(h + 1) * D_HEAD] = (blk * cos + _swap_halves(blk, D_HEAD) * sin).astype(BF16)
    ka = y[:, A_WIDTH:A_WIDTH + D_HEAD]
    ka_ref[...] = ka * cos + _swap_halves(ka, D_HEAD) * sin
    va_ref[...] = y[:, A_WIDTH + D_HEAD:A_WIDTH + 2 * D_HEAD]
    cosi, sini = cosi_ref[...], sini_ref[...]
    base = A_WIDTH + 2 * D_HEAD
    for j in range(IDX_WIDTH // LANES):
        blk = y[:, base + j * LANES:base + (j + 1) * LANES]
        qi_ref[:, j * LANES:(j + 1) * LANES] = (blk * cosi + _swap_halves(blk, D_IDX) * sini).astype(BF16)
    blk = y[:, base + IDX_WIDTH:base + IDX_WIDTH + LANES]
    lane = lax.broadcasted_iota(jnp.int32, blk.shape, 1)
    is_k = lane < D_IDX
    kiwi = blk * jnp.where(is_k, cosi, 1.0) + _swap_halves(blk, D_IDX) * jnp.where(is_k, sini, 0.0)
    kiwi_ref[...] = kiwi
    if kit_ref is not None:
        kit_ref[0] = kiwi.T[:D_IDX, :]


def _inproj_rope(x, g, w_a, tabs, tm, n_pos_blocks, transposed_k_idx):
    rows = x.shape[0]
    wa_cols = w_a.shape[1]
    cos, sin, cosi, sini = tabs
    row_spec = lambda width: pl.BlockSpec((tm, width), lambda i: (i, 0))
    tab_spec = pl.BlockSpec((tm, LANES), lambda i: (i % n_pos_blocks, 0))
    out_specs = [row_spec(A_WIDTH), row_spec(D_HEAD), row_spec(D_HEAD), row_spec(IDX_WIDTH), row_spec(LANES)]
    out_shape = [jax.ShapeDtypeStruct((rows, A_WIDTH), BF16),
                 jax.ShapeDtypeStruct((rows, D_HEAD), F32),
                 jax.ShapeDtypeStruct((rows, D_HEAD), F32),
                 jax.ShapeDtypeStruct((rows, IDX_WIDTH), BF16),
                 jax.ShapeDtypeStruct((rows, LANES), F32)]
    if transposed_k_idx:
        out_specs.append(pl.BlockSpec((1, D_IDX, tm), lambda i: (i // n_pos_blocks, 0, i % n_pos_blocks)))
        out_shape.append(jax.ShapeDtypeStruct((rows // (n_pos_blocks * tm), D_IDX, n_pos_blocks * tm), F32))
    return pl.pallas_call(
        _inproj_rope_kernel,
        grid=(rows // tm,),
        in_specs=[row_spec(D_MODEL),
                  pl.BlockSpec((1, D_MODEL), lambda i: (0, 0)),
                  pl.BlockSpec((D_MODEL, wa_cols), lambda i: (0, 0)),
                  tab_spec, tab_spec, tab_spec, tab_spec],
        out_specs=out_specs,
        out_shape=out_shape,
        compiler_params=_params(("arbitrary",)),
        name="inproj_rope",
    )(x, g, w_a, cos, sin, cosi, sini)


N_GATE_TILES = 2 * D_MODEL // B_WIDTH


def _inproj_plain_kernel(x_ref, g_ref, w_ref, kb_ref, vb_ref, qs_ref, xn_ref):
    j = pl.program_id(1)

    @pl.when(j == 0)
    def _():
        xn_ref[...] = _rmsnorm_rows(x_ref[...], g_ref[...]).astype(BF16)

    product = lambda: _dot(xn_ref[...], w_ref[...])

    @pl.when(j == 0)
    def _():
        qs_ref[...] = product().astype(BF16)

    @pl.when(j == 1)
    def _():
        kb_ref[...] = product()

    @pl.when(j == 2)
    def _():
        vb_ref[...] = product()

    @pl.when(j >= 3)
    def _():
        qs_ref[...] = (1.0 / (1.0 + jnp.exp(-product()))).astype(BF16)


def _inproj_plain(x, g, w_bg, tm):
    rows = x.shape[0]
    tn = B_WIDTH
    n_col = w_bg.shape[1] // tn
    assert n_col == 3 + N_GATE_TILES
    fixed = pl.BlockSpec((tm, tn), lambda i, j: (i, 0))
    qs_map = lambda i, j: (i, jnp.where(j < 3, N_GATE_TILES, j - 3))
    return pl.pallas_call(
        _inproj_plain_kernel,
        grid=(rows // tm, n_col),
        in_specs=[pl.BlockSpec((tm, D_MODEL), lambda i, j: (i, 0)),
                  pl.BlockSpec((1, D_MODEL), lambda i, j: (0, 0)),
                  pl.BlockSpec((D_MODEL, tn), lambda i, j: (0, j))],
        out_specs=[fixed, fixed, pl.BlockSpec((tm, tn), qs_map)],
        out_shape=[jax.ShapeDtypeStruct((rows, tn), F32), jax.ShapeDtypeStruct((rows, tn), F32),
                   jax.ShapeDtypeStruct((rows, (N_GATE_TILES + 1) * tn), BF16)],
        scratch_shapes=[pltpu.VMEM((tm, D_MODEL), BF16)],
        compiler_params=_params(("arbitrary", "arbitrary")),
        name="inproj_plain",
    )(x, g, w_bg)


def _sortable(x):
    b = lax.bitcast_convert_type(x, jnp.int32)
    return b ^ ((b >> 31) & jnp.int32(0x7FFFFFFF))


def _select_topk(keys_ref, neg_ref, valid_fn, k_top):
    rows, width = keys_ref.shape
    n_chunks = width // LANES
    ones = jnp.ones((LANES, LANES), BF16)
    lane = lax.broadcasted_iota(jnp.int32, (rows, LANES), 1)

    def lane_total(acc):
        return _dot(acc.astype(BF16), ones)

    def count(pred):
        acc = jnp.zeros((rows, LANES), F32)
        for c in range(n_chunks):
            acc = acc + jnp.where(pred(keys_ref[:, c * LANES:(c + 1) * LANES], c), 1.0, 0.0)
        return lane_total(acc)

    def value_step(i, ans):
        cand_u = ans | lax.shift_left(jnp.int32(1), 31 - i)
        cand = cand_u ^ jnp.int32(INT_MIN)
        cnt = count(lambda k, c: k >= cand)
        return jnp.where(cnt >= k_top, cand_u, ans)

    thr = lax.fori_loop(0, 32, value_step, jnp.zeros((rows, LANES), jnp.int32)) ^ jnp.int32(INT_MIN)
    n_gt = count(lambda k, c: k > thr)
    n_ge = count(lambda k, c: k >= thr)
    need = k_top - n_gt

    n_bits = (width - 1).bit_length()

    def index_step(i, cst):
        cand = cst | lax.shift_left(jnp.int32(1), n_bits - 1 - i)
        cnt = count(lambda k, c: (k == thr) & (lane + c * LANES < cand))
        return jnp.where(cnt < need, cand, cst)

    cut = lax.cond(jnp.max(n_ge) > k_top,
                   lambda: lax.fori_loop(0, n_bits, index_step, jnp.zeros((rows, LANES), jnp.int32)),
                   lambda: jnp.full((rows, LANES), width, jnp.int32))

    for c in range(n_chunks):
        k = keys_ref[:, c * LANES:(c + 1) * LANES]
        tie_keep = jnp.where(lane + c * LANES <= cut, 0.0, NEG_INF)
        keep = jnp.where(k > thr, 0.0, jnp.where(k == thr, tie_keep, NEG_INF))
        neg_ref[:, c * LANES:(c + 1) * LANES] = jnp.where(valid_fn(c), keep, NEG_INF)


def _dsa_prompt_kernel(qi_ref, wq_ref, kk_ref, qa_ref, ka_ref, va_ref, *rest, k_top, q_start):
    _, o_ref, keys_ref, neg_ref = rest
    tq, width = keys_ref.shape

    w = wq_ref[0, :, D_IDX:D_IDX + N_IDX_HEADS] * ((N_IDX_HEADS * D_IDX) ** -0.5)
    k_idx = kk_ref[0, :, :D_IDX].astype(BF16)
    score = jnp.zeros((tq, width), F32)
    for h in range(N_IDX_HEADS):
        rel = jnp.maximum(_dot_nt(qi_ref[0, :, h * D_IDX:(h + 1) * D_IDX], k_idx), 0.0)
        score = score + rel * w[:, h:h + 1]

    t_pos = q_start + lax.broadcasted_iota(jnp.int32, (tq, LANES), 0)
    lane = lax.broadcasted_iota(jnp.int32, (tq, LANES), 1)
    causal = lambda c: lane + c * LANES <= t_pos
    s_pos = lax.broadcasted_iota(jnp.int32, (tq, width), 1)
    q_pos = q_start + lax.broadcasted_iota(jnp.int32, (tq, width), 0)
    keys_ref[...] = _sortable(jnp.where(s_pos <= q_pos, score, NEG_INF))
    _select_topk(keys_ref, neg_ref, causal, k_top)

    k_att = ka_ref[0].astype(BF16)
    v_att = va_ref[0].astype(BF16)
    exp2_scale = (D_HEAD ** -0.5) * math.log2(math.e)
    for h in range(N_HEADS_A):
        s = _dot_nt(qa_ref[0, :, h * D_HEAD:(h + 1) * D_HEAD], k_att) + neg_ref[...]
        m = jnp.max(s, axis=-1, keepdims=True)
        e = jnp.exp2((s - m) * exp2_scale)
        l = jnp.sum(e, axis=-1, keepdims=True)
        o = _dot(e.astype(BF16), v_att) / l
        o_ref[0, :, h * D_HEAD:(h + 1) * D_HEAD] = o.astype(BF16)


def _dsa_prompt(q_i, kiwi, q_a, k_a, v_a, batch, seq, tq):
    k_top = min(TOPK_MAX, seq // 4)
    out = jnp.zeros((batch, seq, A_WIDTH), BF16)
    for qb in range(seq // tq):
        width = (qb + 1) * tq
        q_spec = lambda w_: pl.BlockSpec((1, tq, w_), lambda b: (b, qb, 0))
        kv_spec = lambda w_: pl.BlockSpec((1, width, w_), lambda b: (b, 0, 0))
        out = pl.pallas_call(
            functools.partial(_dsa_prompt_kernel, k_top=k_top, q_start=qb * tq),
            grid=(batch,),
            in_specs=[q_spec(IDX_WIDTH), q_spec(LANES), kv_spec(LANES), q_spec(A_WIDTH), kv_spec(D_HEAD),
                      kv_spec(D_HEAD), pl.BlockSpec(memory_space=pl.ANY)],
            out_specs=q_spec(A_WIDTH),
            out_shape=jax.ShapeDtypeStruct((batch, seq, A_WIDTH), BF16),
            scratch_shapes=[pltpu.VMEM((tq, width), jnp.int32), pltpu.VMEM((tq, width), F32)],
            input_output_aliases={6: 0},
            compiler_params=_params(("arbitrary",)),
            name=f"dsa_prompt_q{qb}",
        )(q_i, kiwi, kiwi, q_a, k_a, v_a, out)
    return out


def _suffix_matrix(n, reps=1):
    j = lax.broadcasted_iota(jnp.int32, (n * reps, n * reps), 0)
    s = lax.broadcasted_iota(jnp.int32, (n * reps, n * reps), 1)
    same = (j % reps) == (s % reps)
    later = jnp.where(same & (j // reps > s // reps), 1.0, 0.0)
    total = jnp.where(same, 1.0, 0.0)
    return jnp.concatenate([later, total], axis=1).astype(BF16)


def _softplus(z):
    return jnp.maximum(z, 0.0) + jnp.log(1.0 + jnp.exp(-jnp.abs(z)))


def _sb_prompt_kernel(q_ref, k_ref, v_ref, u_ref, o_ref, k16, v16, acc_ref, later_ref):
    qb = pl.program_id(2)
    tq = q_ref.shape[0]
    sub = LANES
    n_sub = tq // sub

    @pl.when(qb == 0)
    def _():
        k16[...] = k_ref[...].astype(BF16)
        v16[...] = v_ref[...].astype(BF16)

    u = u_ref[...]
    n_heads = q_ref.shape[1] // D_HEAD

    def block(kb, diagonal):
        start = pl.multiple_of(kb * tq, tq)
        for hh in range(n_heads):
            cols = slice(hh * D_HEAD, (hh + 1) * D_HEAD)
            z = _dot_nt(q_ref[:, cols], k16[pl.ds(start, tq), cols]) * (D_HEAD ** -0.5)
            sp = _softplus(z)
            log_beta = z - sp
            if diagonal:
                before = (lax.broadcasted_iota(jnp.int32, (tq, tq), 1)
                          < lax.broadcasted_iota(jnp.int32, (tq, tq), 0))
                sp = jnp.where(before, sp, 0.0)
            sp16 = sp.astype(BF16)
            later = later_ref[:, cols]
            args = [None] * n_sub
            for j in reversed(range(n_sub)):
                cum = _dot(sp16[:, j * sub:(j + 1) * sub], u)
                args[j] = log_beta[:, j * sub:(j + 1) * sub] - cum[:, :sub] - later
                later = later + cum[:, sub:]
            a = jnp.exp(jnp.concatenate(args, axis=1))
            if diagonal:
                a = jnp.where(before, a, 0.0)
            acc_ref[:, cols] += _dot(a.astype(BF16), v16[pl.ds(start, tq), cols])
            later_ref[:, cols] = later

    acc_ref[...] = jnp.zeros_like(acc_ref)
    later_ref[...] = jnp.zeros_like(later_ref)
    block(qb, True)

    def body(it, carry):
        block(qb - 1 - it, False)
        return carry

    lax.fori_loop(0, qb, body, 0)
    o_ref[...] = acc_ref[...].astype(BF16)


def _sb_prompt(qs, k_b, v_b, batch, seq, tq):
    rows = batch * seq
    nq = seq // tq
    hw = SB_HEADS_PER_STEP * D_HEAD
    q_col = (qs.shape[1] - B_WIDTH) // hw
    q_spec = pl.BlockSpec((tq, hw), lambda b, h, q: (b * nq + q, q_col + h))
    kv_spec = pl.BlockSpec((seq, hw), lambda b, h, q: (b, h))
    return pl.pallas_call(
        _sb_prompt_kernel,
        grid=(batch, N_HEADS_B // SB_HEADS_PER_STEP, nq),
        in_specs=[q_spec, kv_spec, kv_spec, pl.BlockSpec((LANES, 2 * LANES), lambda b, h, q: (0, 0))],
        out_specs=pl.BlockSpec((tq, hw), lambda b, h, q: (b * nq + q, h)),
        out_shape=jax.ShapeDtypeStruct((rows, B_WIDTH), BF16),
        scratch_shapes=[pltpu.VMEM((seq, hw), BF16), pltpu.VMEM((seq, hw), BF16),
                        pltpu.VMEM((tq, hw), F32), pltpu.VMEM((tq, hw), F32)],
        compiler_params=_params(("arbitrary", "arbitrary", "arbitrary")),
        name="sb_prompt",
    )(qs, k_b, v_b, _suffix_matrix(LANES))


def _page_copies(pt_ref, batch_idx, first_page, n_pages, pools, bufs, slot, sem, rows_per_page, n_logical):
    copies = []
    for p in range(n_pages):
        phys = pt_ref[batch_idx * n_logical + first_page + p]
        for pool, buf in zip(pools, bufs):
            copies.append(pltpu.make_async_copy(
                pool.at[phys], buf.at[slot, pl.ds(p * rows_per_page, rows_per_page)], sem.at[slot]))
    return copies


def _paged_double_buffer(pt_ref, pools, bufs, sem, n_pages, rows_per_page):
    b = pl.program_id(0)
    slot = b % 2
    copies = lambda bb, ss: _page_copies(pt_ref, bb, 0, n_pages, pools, bufs, ss, sem, rows_per_page, n_pages)

    @pl.when(b == 0)
    def _():
        for c in copies(0, 0):
            c.start()

    @pl.when(b + 1 < pl.num_programs(0))
    def _():
        for c in copies(b + 1, 1 - slot):
            c.start()

    for c in copies(b, slot):
        c.wait()
    return slot


def _new_key_visible(n_q, n_new, rows):
    t_idx = lax.broadcasted_iota(jnp.int32, (rows, LANES), 0) % n_q
    lane = lax.broadcasted_iota(jnp.int32, (rows, LANES), 1)
    return (lane <= t_idx) & (lane < n_new)


def _idx_scores_sample_kernel(pt_ref, qi_ref, w_ref, kin_ref, idx_pool, s_ref, idx_buf, sem, *, n_pages, n_new):
    b = pl.program_id(0)
    slot = b % 2
    past = n_pages * PAGE_SIZE

    def copies(bb, ss):
        return [pltpu.make_async_copy(idx_pool.at[pt_ref[bb * n_pages + p]],
                                      idx_buf.at[ss, :, pl.ds(p * PAGE_SIZE, PAGE_SIZE)], sem.at[ss])
                for p in range(n_pages)]

    @pl.when(b == 0)
    def _():
        for c in copies(0, 0):
            c.start()

    @pl.when(b + 1 < pl.num_programs(0))
    def _():
        for c in copies(b + 1, 1 - slot):
            c.start()

    for c in copies(b, slot):
        c.wait()

    q = qi_ref[0]
    w = w_ref[0]
    n_q = q.shape[0] // N_IDX_HEADS

    def head_sum(qk):
        rel = jnp.maximum(qk, 0.0) * w
        return functools.reduce(lambda a, c: a + c, [rel[h * n_q:(h + 1) * n_q] for h in range(N_IDX_HEADS)])

    s_ref[0, :, :past] = head_sum(_dot(q, idx_buf[slot].astype(BF16)))
    s_new = head_sum(_dot_nt(q, kin_ref[0].astype(BF16)))
    s_ref[0, :, past:] = jnp.where(_new_key_visible(n_q, n_new, n_q), s_new, NEG_INF)


def _topk_mask_kernel(s_ref, neg_ref, keys_ref, *, k_top, n_q, n_new):
    rows, width = keys_ref.shape
    keys_ref[...] = _sortable(s_ref[...])
    visible = _new_key_visible(n_q, n_new, rows)
    last = width // LANES - 1
    _select_topk(keys_ref, neg_ref, lambda c: visible if c == last else jnp.full((rows, LANES), True), k_top)


def _dsa_sample_attend_kernel(pt_ref, qa_ref, neg_ref, kan_ref, van_ref, k_pool, v_pool, o_ref,
                              k_buf, v_buf, sem, *, n_pages):
    slot = _paged_double_buffer(pt_ref, (k_pool, v_pool), (k_buf, v_buf), sem, n_pages, PAGE_SIZE)
    past = n_pages * PAGE_SIZE
    k_att = k_buf[slot].astype(BF16)
    v_att = v_buf[slot].astype(BF16)
    q = qa_ref[0]
    neg = jnp.concatenate([neg_ref[0]] * N_HEADS_A, axis=0)
    s_past = _dot_nt(q, k_att) + neg[:, :past]
    s_new = _dot_nt(q, kan_ref[0].astype(BF16)) + neg[:, past:]
    m = jnp.maximum(jnp.max(s_past, axis=-1, keepdims=True), jnp.max(s_new, axis=-1, keepdims=True))
    exp2_scale = (D_HEAD ** -0.5) * math.log2(math.e)
    e_past = jnp.exp2((s_past - m) * exp2_scale)
    e_new = jnp.exp2((s_new - m) * exp2_scale)
    l = jnp.sum(e_past, axis=-1, keepdims=True) + jnp.sum(e_new, axis=-1, keepdims=True)
    o = _dot(e_past.astype(BF16), v_att) + _dot(e_new.astype(BF16), van_ref[0].astype(BF16))
    o_ref[0] = o / l


def _dsa_sample(page_table, q_i, w_i, k_i_new, q_a, k_a_new, v_a_new, idx_pool, k_pool, v_pool):
    batch, n_pages = page_table.shape
    n_q = q_a.shape[1] // N_HEADS_A
    past = n_pages * PAGE_SIZE
    width = past + LANES
    k_top = min(TOPK_MAX, (past + n_q) // 4)
    pt = page_table.reshape(-1)
    blk = lambda shape: pl.BlockSpec((1,) + shape, lambda b, pt_: (b,) + (0,) * len(shape))
    hbm = pl.BlockSpec(memory_space=pl.ANY)

    scores = pl.pallas_call(
        functools.partial(_idx_scores_sample_kernel, n_pages=n_pages, n_new=n_q),
        grid_spec=pltpu.PrefetchScalarGridSpec(
            num_scalar_prefetch=1, grid=(batch,),
            in_specs=[blk((N_IDX_HEADS * n_q, D_IDX)), blk((N_IDX_HEADS * n_q, 1)), blk((LANES, D_IDX)), hbm],
            out_specs=blk((n_q, width)),
            scratch_shapes=[pltpu.VMEM((2, D_IDX, past), F32), pltpu.SemaphoreType.DMA((2,))]),
        out_shape=jax.ShapeDtypeStruct((batch, n_q, width), F32),
        compiler_params=_params(("arbitrary",)),
        name="idx_scores_sample",
    )(pt, q_i, w_i, k_i_new, idx_pool)

    rows = batch * n_q
    tr = min(rows, LANES)
    neg = pl.pallas_call(
        functools.partial(_topk_mask_kernel, k_top=k_top, n_q=n_q, n_new=n_q),
        grid=(rows // tr,),
        in_specs=[pl.BlockSpec((tr, width), lambda i: (i, 0))],
        out_specs=pl.BlockSpec((tr, width), lambda i: (i, 0)),
        out_shape=jax.ShapeDtypeStruct((rows, width), F32),
        scratch_shapes=[pltpu.VMEM((tr, width), jnp.int32)],
        compiler_params=_params(("arbitrary",)),
        name="topk_mask_sample",
    )(scores.reshape(rows, width))

    return pl.pallas_call(
        functools.partial(_dsa_sample_attend_kernel, n_pages=n_pages),
        grid_spec=pltpu.PrefetchScalarGridSpec(
            num_scalar_prefetch=1, grid=(batch,),
            in_specs=[blk((N_HEADS_A * n_q, D_HEAD)), blk((n_q, width)), blk((LANES, D_HEAD)), blk((LANES, D_HEAD)),
                      hbm, hbm],
            out_specs=blk((N_HEADS_A * n_q, D_HEAD)),
            scratch_shapes=[pltpu.VMEM((2, past, D_HEAD), F32), pltpu.VMEM((2, past, D_HEAD), F32),
                            pltpu.SemaphoreType.DMA((2,))]),
        out_shape=jax.ShapeDtypeStruct((batch, N_HEADS_A * n_q, D_HEAD), F32),
        compiler_params=_params(("arbitrary",)),
        name="dsa_sample_attend",
    )(pt, q_a, neg.reshape(batch, n_q, width), k_a_new, v_a_new, k_pool, v_pool)


SB_CHUNK_PAGES = 8
SB_SLOTS = 3
SB_BLOCK_KEYS = 32


def _sb_sample_kernel(pt_ref, q_ref, kn_ref, vn_ref, u_ref, k_pool, v_pool, o_ref,
                      k_buf, v_buf, acc_ref, later_ref, sem, *, n_pages, n_new, total_steps):
    b = pl.program_id(0)
    j = pl.program_id(1)
    n_chunks = pl.num_programs(1)
    step = b * n_chunks + j
    n_steps = total_steps
    n_slots = k_buf.shape[0]
    slot = step % n_slots
    heads = N_HEADS_B
    rows_per_page = PAGE_SIZE * heads
    blk_lanes = SB_BLOCK_KEYS * heads

    def copies(st, ss):
        bb = st // n_chunks
        chunk = n_chunks - 1 - st % n_chunks
        return _page_copies(pt_ref, bb, chunk * SB_CHUNK_PAGES, SB_CHUNK_PAGES, (k_pool, v_pool),
                            (k_buf, v_buf), ss, sem, rows_per_page, n_pages)

    @pl.when(step == 0)
    def _():
        for ahead in range(min(n_slots - 1, total_steps)):
            for c in copies(ahead, ahead):
                c.start()

    @pl.when(step + n_slots - 1 < n_steps)
    def _():
        for c in copies(step + n_slots - 1, (step + n_slots - 1) % n_slots):
            c.start()

    q = q_ref[0]
    n_q = q.shape[0] // heads
    u = u_ref[...]
    scale = D_HEAD ** -0.5

    def head_diag(zt):
        lane_head = lax.broadcasted_iota(jnp.int32, (n_q, zt.shape[1]), 1) % heads
        out = jnp.zeros((n_q, zt.shape[1]), F32)
        for h in range(heads):
            out = out + jnp.where(lane_head == h, zt[h * n_q:(h + 1) * n_q, :], 0.0)
        return out

    def head_spread(a):
        lane_head = lax.broadcasted_iota(jnp.int32, a.shape, 1) % heads
        return jnp.concatenate([jnp.where(lane_head == h, a, 0.0) for h in range(heads)], axis=0).astype(BF16)

    def attend(k16, v16, before, later):
        n = k16.shape[0]
        z = head_diag(_dot_nt(q, k16) * scale)
        sp = _softplus(z)
        log_beta = z - sp
        sp16 = jnp.where(before, sp, 0.0).astype(BF16)
        pieces = []
        for blk in reversed(range(n // blk_lanes)):
            sl = slice(blk * blk_lanes, (blk + 1) * blk_lanes)
            cum = _dot(sp16[:, sl], u)
            pieces.append(log_beta[:, sl] - cum[:, :blk_lanes] - later)
            later = later + cum[:, blk_lanes:]
        a = jnp.exp(jnp.concatenate(pieces[::-1], axis=1))
        a = jnp.where(before, a, 0.0)
        return _dot(head_spread(a), v16), later

    @pl.when(j == 0)
    def _():
        t_idx = lax.broadcasted_iota(jnp.int32, (n_q, blk_lanes), 0)
        key = lax.broadcasted_iota(jnp.int32, (n_q, blk_lanes), 1) // heads
        before = (key < t_idx) & (key < n_new)
        out, later = attend(kn_ref[0].astype(BF16), vn_ref[0].astype(BF16), before,
                            jnp.zeros((n_q, blk_lanes), F32))
        acc_ref[...] = out
        later_ref[...] = later

    for c in copies(step, slot):
        c.wait()

    n_rows = SB_CHUNK_PAGES * rows_per_page
    out, later = attend(k_buf[slot].astype(BF16), v_buf[slot].astype(BF16),
                        jnp.full((n_q, n_rows), True), later_ref[...])
    acc_ref[...] += out
    later_ref[...] = later

    @pl.when(j == n_chunks - 1)
    def _():
        o_ref[0] = acc_ref[...]


def _sb_sample(page_table, q, k_new, v_new, k_pool, v_pool):
    batch, n_pages = page_table.shape
    rows_q = q.shape[1]
    n_q = rows_q // N_HEADS_B
    blk_lanes = SB_BLOCK_KEYS * N_HEADS_B
    n_chunks = n_pages // SB_CHUNK_PAGES
    chunk_rows = SB_CHUNK_PAGES * PAGE_SIZE * N_HEADS_B
    blk = lambda shape: pl.BlockSpec((1,) + shape, lambda b, j, pt: (b,) + (0,) * len(shape))
    hbm = pl.BlockSpec(memory_space=pl.ANY)
    grid_spec = pltpu.PrefetchScalarGridSpec(
        num_scalar_prefetch=1,
        grid=(batch, n_chunks),
        in_specs=[blk((rows_q, D_HEAD)), blk((blk_lanes, D_HEAD)), blk((blk_lanes, D_HEAD)),
                  pl.BlockSpec((blk_lanes, 2 * blk_lanes), lambda b, j, pt: (0, 0)), hbm, hbm],
        out_specs=blk((rows_q, D_HEAD)),
        scratch_shapes=[pltpu.VMEM((SB_SLOTS, chunk_rows, D_HEAD), F32), pltpu.VMEM((SB_SLOTS, chunk_rows, D_HEAD), F32),
                        pltpu.VMEM((rows_q, D_HEAD), F32), pltpu.VMEM((n_q, blk_lanes), F32),
                        pltpu.SemaphoreType.DMA((SB_SLOTS,))],
    )
    return pl.pallas_call(
        functools.partial(_sb_sample_kernel, n_pages=n_pages, n_new=n_q, total_steps=batch * n_chunks),
        grid_spec=grid_spec,
        out_shape=jax.ShapeDtypeStruct((batch, rows_q, D_HEAD), F32),
        compiler_params=_params(("arbitrary", "arbitrary")),
        name="sb_sample",
    )(page_table.reshape(-1), q, k_new, v_new, _suffix_matrix(SB_BLOCK_KEYS, N_HEADS_B), k_pool, v_pool)


PACKED_WIDTH = D_MODEL // 2


def _pack_bf16_pair(lo, hi):
    lo_bits = lax.bitcast_convert_type(lo.astype(BF16).astype(F32), jnp.uint32)
    hi_bits = lax.bitcast_convert_type(hi.astype(BF16).astype(F32), jnp.uint32)
    return (lo_bits >> 16) | (hi_bits & jnp.uint32(0xFFFF0000))


def _unpack_bf16_pair(words):
    lo = lax.bitcast_convert_type(words << 16, F32)
    hi = lax.bitcast_convert_type(words & jnp.uint32(0xFFFF0000), F32)
    return lo, hi


def _merge_kernel(*refs, n_prompt_tiles):
    prompt, sample = refs[0:4], refs[4:8]
    wua_ref, wub_ref, wo_ref, gf_ref, wr_hi_ref, wr_lo_ref, br_ref, h_ref, hn_ref, lg_ref = refs[8:]

    def tile(x_ref, oa_ref, ob_ref, sig_ref):
        sig_a = sig_ref[:, :D_MODEL].astype(F32)
        sig_b = sig_ref[:, D_MODEL:].astype(F32)
        merged = sig_a * _dot(oa_ref[...], wua_ref[...]) + sig_b * _dot(ob_ref[...], wub_ref[...])
        h = x_ref[...] + _dot(merged.astype(BF16), wo_ref[...])
        h_ref[...] = h
        hn = _rmsnorm_rows(h, gf_ref[...])
        hn_ref[...] = _pack_bf16_pair(hn[:, :PACKED_WIDTH], hn[:, PACKED_WIDTH:])
        hn_hi = hn.astype(BF16)
        hn_lo = (hn - hn_hi.astype(F32)).astype(BF16)
        lg_ref[...] = (_dot(hn_hi, wr_hi_ref[...]) + _dot(hn_lo, wr_hi_ref[...])
                       + _dot(hn_hi, wr_lo_ref[...]) + br_ref[...])

    i = pl.program_id(0)

    @pl.when(i < n_prompt_tiles)
    def _():
        tile(*prompt)

    @pl.when(i >= n_prompt_tiles)
    def _():
        tile(*sample)


def _merge(prompt, sample, w_up_a, w_up_b, w_out, g_ffn, wr_hi, wr_lo, b_r, tm):
    n_p, n_s = prompt[0].shape[0] // tm, sample[0].shape[0] // tm
    assert prompt[0].shape[0] == n_p * tm and sample[0].shape[0] == n_s * tm
    rows = (n_p + n_s) * tm
    widths = (D_MODEL, A_WIDTH, B_WIDTH, 2 * D_MODEL)
    p_spec = lambda width: pl.BlockSpec((tm, width), lambda i: (jnp.minimum(i, n_p - 1), 0))
    s_spec = lambda width: pl.BlockSpec((tm, width), lambda i: (jnp.maximum(i - n_p, 0), 0))
    row_spec = lambda width: pl.BlockSpec((tm, width), lambda i: (i, 0))
    const = lambda shape: pl.BlockSpec(shape, lambda i: (0, 0), pipeline_mode=pl.Buffered(1))
    return pl.pallas_call(
        functools.partial(_merge_kernel, n_prompt_tiles=n_p),
        grid=(n_p + n_s,),
        in_specs=[p_spec(w_) for w_ in widths] + [s_spec(w_) for w_ in widths] + [
                  const((A_WIDTH, D_MODEL)), const((B_WIDTH, D_MODEL)), const((D_MODEL, D_MODEL)),
                  const((1, D_MODEL)), const((D_MODEL, ROUTER_LANES)), const((D_MODEL, ROUTER_LANES)),
                  const((1, ROUTER_LANES))],
        out_specs=[row_spec(D_MODEL), row_spec(PACKED_WIDTH), row_spec(ROUTER_LANES)],
        out_shape=[jax.ShapeDtypeStruct((rows, D_MODEL), F32),
                   jax.ShapeDtypeStruct((rows, PACKED_WIDTH), jnp.uint32),
                   jax.ShapeDtypeStruct((rows, ROUTER_LANES), F32)],
        compiler_params=_params(("arbitrary",)),
        name="merge",
    )(*prompt, *sample, w_up_a, w_up_b, w_out, g_ffn, wr_hi, wr_lo, b_r)


def _router_kernel(lg_ref, eid_ref, gate_ref, *, chunk):
    for c0 in range(0, lg_ref.shape[0], chunk):
        _route_chunk(lg_ref, eid_ref, gate_ref, c0, chunk)


def _route_chunk(lg_ref, eid_ref, gate_ref, c0, chunk):
    lg = lg_ref[c0:c0 + chunk, :].T
    row = lambda r: lg[r:r + 1, :]
    groups = [row(g) for g in range(N_GROUPS)]
    g_max = functools.reduce(jnp.maximum, groups)
    g_sel = jnp.full(g_max.shape, N_GROUPS - 1, jnp.int32)
    for g in reversed(range(N_GROUPS)):
        g_sel = jnp.where(groups[g] == g_max, g, g_sel)
    p_group = 1.0 / functools.reduce(lambda a, c: a + c, [jnp.exp(v - g_max) for v in groups])
    inside = []
    for e in range(EXPERTS_PER_GROUP):
        v = jnp.zeros_like(g_max)
        for g in range(N_GROUPS):
            v = jnp.where(g_sel == g, row(N_GROUPS + g * EXPERTS_PER_GROUP + e), v)
        inside.append(v)

    def top1(vals, skip):
        best = functools.reduce(jnp.maximum, [jnp.where(skip == e, NEG_INF, v) for e, v in enumerate(vals)])
        idx = jnp.full(best.shape, EXPERTS_PER_GROUP - 1, jnp.int32)
        for e in reversed(range(EXPERTS_PER_GROUP)):
            idx = jnp.where((vals[e] == best) & (skip != e), e, idx)
        return best, idx

    t1, i1 = top1(inside, jnp.full(g_max.shape, -1, jnp.int32))
    t2, i2 = top1(inside, i1)
    r = jnp.exp(t2 - t1)
    for k, ids in enumerate((g_sel * EXPERTS_PER_GROUP + i1, g_sel * EXPERTS_PER_GROUP + i2)):
        for c in range(chunk // LANES):
            row0 = c0 // LANES + c
            eid_ref[k, row0:row0 + 1, :] = ids[:, c * LANES:(c + 1) * LANES]
    gates = jnp.concatenate([p_group * (1.0 / (1.0 + r)), p_group * (r / (1.0 + r)),
                             jnp.zeros((ROUTER_LANES - TOP_K_EXPERTS, chunk), F32)], axis=0)
    gate_ref[c0:c0 + chunk, :] = gates.T


def _router(logits):
    n_tok = logits.shape[0]
    assert n_tok % LANES == 0
    chunk = max(c for c in range(LANES, 2048 + 1, LANES) if n_tok % c == 0)
    return pl.pallas_call(
        functools.partial(_router_kernel, chunk=chunk),
        out_shape=[jax.ShapeDtypeStruct((TOP_K_EXPERTS, n_tok // LANES, LANES), jnp.int32),
                   jax.ShapeDtypeStruct((n_tok, ROUTER_LANES), F32)],
        compiler_params=pltpu.CompilerParams(vmem_limit_bytes=VMEM_LIMIT),
        name="router",
    )(logits)


def _experts_kernel(blk_e_ref, blk_off_ref, blk_cnt_ref, nact_ref, order_ref,
                    hn_hbm, wg_ref, wu_ref, wd_ref, ys_hbm,
                    x_buf, y_buf, wg16, wu16, wd16, in_sem, out_sem, *, n_tok):
    i = pl.program_id(0)
    n_act = nact_ref[0]
    rows = x_buf.shape[1]
    slot = i % 2
    n_pairs = TOP_K_EXPERTS * n_tok

    def gathers(blk, ss):
        off = blk_off_ref[blk]
        copies = []
        for r in range(rows):
            p = order_ref[off + r]
            tok = jnp.where(p >= n_tok, p - n_tok, p)
            copies.append(pltpu.make_async_copy(hn_hbm.at[pl.ds(tok, 1)], x_buf.at[ss, pl.ds(r, 1)], in_sem.at[ss]))
        return copies

    def scatters(blk, ss, real):
        off = blk_off_ref[blk]
        cnt = jnp.where(real, blk_cnt_ref[blk], 0)
        copies = []
        for r in range(rows):
            dst = jnp.where(r < cnt, order_ref[off + r], n_pairs + ss * rows + r)
            copies.append(pltpu.make_async_copy(y_buf.at[ss, pl.ds(r, 1)], ys_hbm.at[pl.ds(dst, 1)], out_sem.at[ss]))
        return copies

    @pl.when(i == 0)
    def _():
        for c in gathers(0, 0):
            c.start()
        y_buf[1] = jnp.zeros(y_buf.shape[1:], y_buf.dtype)
        for ss in range(2):
            spare = pltpu.make_async_copy(y_buf.at[1], ys_hbm.at[pl.ds(n_pairs + ss * rows, rows)], out_sem.at[1])
            spare.start()
            spare.wait()

    @pl.when((i >= 1) & (i < n_act))
    def _():
        for c in scatters(jnp.maximum(i - 2, 0), slot, i >= 2):
            c.wait()

    first_of_expert = (i == 0) | (blk_e_ref[i] != blk_e_ref[jnp.maximum(i - 1, 0)])

    @pl.when((i < n_act) & first_of_expert)
    def _():
        wg16[...] = wg_ref[0].astype(BF16)
        wu16[...] = wu_ref[0].astype(BF16)
        wd16[...] = wd_ref[0].astype(BF16)

    @pl.when(i < n_act)
    def _():
        for c in gathers(i, slot):
            c.wait()
        reads = gathers(jnp.minimum(i + 1, n_act - 1), 1 - slot)
        writes = scatters(jnp.maximum(i - 1, 0), 1 - slot, i >= 1)
        starts = [(c, prio) for pair in zip(reads, writes) for prio, c in enumerate(pair)]
        n_chunks = 2 * D_MODEL // EXPERT_K_CHUNK
        per_chunk = len(starts) // n_chunks
        chunk_no = 0

        def issue_some():
            nonlocal chunk_no
            for c, prio in starts[chunk_no * per_chunk:(chunk_no + 1) * per_chunk]:
                c.start(priority=prio)
            chunk_no += 1

        g = jnp.zeros((rows, D_EXPERT), F32)
        u = jnp.zeros((rows, D_EXPERT), F32)
        for k0 in range(0, D_MODEL, EXPERT_K_CHUNK):
            c0 = k0 % PACKED_WIDTH
            xc = _unpack_bf16_pair(x_buf[slot, :, c0:c0 + EXPERT_K_CHUNK])[k0 // PACKED_WIDTH].astype(BF16)
            g = g + _dot(xc, wg16[k0:k0 + EXPERT_K_CHUNK, :])
            u = u + _dot(xc, wu16[k0:k0 + EXPERT_K_CHUNK, :])
            issue_some()
        act = (g * (1.0 / (1.0 + jnp.exp(-g))) * u).astype(BF16)
        for n0 in range(0, PACKED_WIDTH, EXPERT_K_CHUNK):
            lo = _dot(act, wd16[:, n0:n0 + EXPERT_K_CHUNK])
            issue_some()
            hi = _dot(act, wd16[:, PACKED_WIDTH + n0:PACKED_WIDTH + n0 + EXPERT_K_CHUNK])
            y_buf[slot, :, n0:n0 + EXPERT_K_CHUNK] = _pack_bf16_pair(lo, hi)
            issue_some()

    @pl.when(i == n_act - 1)
    def _():
        for c in scatters(i, slot, True):
            c.start(priority=1)
        for c in gathers(i, 1 - slot):
            c.wait()
        for c in scatters(jnp.maximum(i - 1, 0), 1 - slot, i >= 1):
            c.wait()
        for c in scatters(i, slot, True):
            c.wait()


def _experts(blk_e, blk_off, blk_cnt, n_act, order, hn, w_gate, w_up, w_down):
    n_blocks = blk_e.shape[0]
    n_tok = hn.shape[0]
    rows = EXPERT_ROWS
    w_spec = lambda shape: pl.BlockSpec((1,) + shape, lambda i, be, *_: (be[i], 0, 0))
    row_buf = pltpu.VMEM((2, rows, PACKED_WIDTH), jnp.uint32)
    grid_spec = pltpu.PrefetchScalarGridSpec(
        num_scalar_prefetch=5,
        grid=(n_blocks,),
        in_specs=[pl.BlockSpec(memory_space=pl.ANY),
                  w_spec((D_MODEL, D_EXPERT)), w_spec((D_MODEL, D_EXPERT)), w_spec((D_EXPERT, D_MODEL))],
        out_specs=pl.BlockSpec(memory_space=pl.ANY),
        scratch_shapes=[row_buf, row_buf,
                        pltpu.VMEM((D_MODEL, D_EXPERT), BF16), pltpu.VMEM((D_MODEL, D_EXPERT), BF16),
                        pltpu.VMEM((D_EXPERT, D_MODEL), BF16),
                        pltpu.SemaphoreType.DMA((2,)), pltpu.SemaphoreType.DMA((2,))],
    )
    return pl.pallas_call(
        functools.partial(_experts_kernel, n_tok=n_tok),
        grid_spec=grid_spec,
        out_shape=jax.ShapeDtypeStruct((TOP_K_EXPERTS * n_tok + 2 * rows, PACKED_WIDTH), jnp.uint32),
        compiler_params=_params(("arbitrary",)),
        name="experts",
    )(blk_e, blk_off, blk_cnt, n_act, order, hn, w_gate, w_up, w_down)


def _combine_kernel(h_ref, gate_ref, gfin_ref, *rest, final_norm, n_prompt_tiles):
    yp_ref, ys_ref = rest[-2:]
    y = h_ref[...]
    for k, e_ref in enumerate(rest[:-2]):
        y = y + jnp.concatenate(_unpack_bf16_pair(e_ref[...]), axis=1) * gate_ref[:, k:k + 1]
    y = _rmsnorm_rows(y, gfin_ref[...]) if final_norm else y
    i = pl.program_id(0)

    @pl.when(i < n_prompt_tiles)
    def _():
        yp_ref[...] = y

    @pl.when(i >= n_prompt_tiles)
    def _():
        ys_ref[...] = y


def _combine(h, gates, g_final, ys, tt, n_prompt, final_norm):
    n_tok = h.shape[0]
    n_tiles, n_p = n_tok // tt, n_prompt // tt
    assert n_tok == n_tiles * tt and n_prompt == n_p * tt and n_tiles > n_p
    e_spec = lambda k: pl.BlockSpec((tt, PACKED_WIDTH), lambda i: (k * n_tiles + i, 0))
    return pl.pallas_call(
        functools.partial(_combine_kernel, final_norm=final_norm, n_prompt_tiles=n_p),
        grid=(n_tiles,),
        in_specs=[pl.BlockSpec((tt, D_MODEL), lambda i: (i, 0)),
                  pl.BlockSpec((tt, ROUTER_LANES), lambda i: (i, 0)),
                  pl.BlockSpec((1, D_MODEL), lambda i: (0, 0))] + [e_spec(k) for k in range(TOP_K_EXPERTS)],
        out_specs=[pl.BlockSpec((tt, D_MODEL), lambda i: (jnp.minimum(i, n_p - 1), 0)),
                   pl.BlockSpec((tt, D_MODEL), lambda i: (jnp.maximum(i - n_p, 0), 0))],
        out_shape=[jax.ShapeDtypeStruct((n_prompt, D_MODEL), F32),
                   jax.ShapeDtypeStruct((n_tok - n_prompt, D_MODEL), F32)],
        compiler_params=_params(("arbitrary",)),
        name="combine",
    )(h, gates, g_final, *([ys] * TOP_K_EXPERTS))


def _moe(h, hn, logits, g_final, w_gate, w_up, w_down, tt, n_prompt, final_norm):
    n_tok = h.shape[0]
    eid, gates = _router(logits)
    flat_e = eid.reshape(-1)
    nk = flat_e.shape[0]
    rows = EXPERT_ROWS
    _, order = lax.sort_key_val(flat_e, jnp.arange(nk, dtype=jnp.int32))
    experts = jnp.arange(N_EXPERTS, dtype=jnp.int32)
    counts = jnp.sum((flat_e[:, None] == experts[None, :]).astype(jnp.int32), axis=0)
    inclusive = lambda v: jnp.sum(jnp.where(experts[:, None] <= experts[None, :], v[:, None], 0), axis=0)
    start = inclusive(counts) - counts
    blocks_per_e = (counts + rows - 1) // rows
    blk_end = inclusive(blocks_per_e)
    blk_begin = blk_end - blocks_per_e
    n_blocks = -(-(nk + N_EXPERTS * (rows - 1)) // rows)
    blk = jnp.arange(n_blocks, dtype=jnp.int32)
    mine = (blk[:, None] >= blk_begin[None, :]) & (blk[:, None] < blk_end[None, :])
    pick = lambda v: jnp.sum(jnp.where(mine, v[None, :], 0), axis=1)
    first = (blk - pick(blk_begin)) * rows
    blk_cnt = jnp.clip(pick(counts) - first, 0, rows)
    blk_off = jnp.clip(pick(start) + first, 0, nk)
    order = jnp.concatenate([order, jnp.zeros((rows,), jnp.int32)])
    ys = _experts(pick(experts), blk_off, blk_cnt, blk_end[-1:], order, hn, w_gate, w_up, w_down)
    return _combine(h, gates, g_final, ys, tt, n_prompt, final_norm)


def _split_router(w_rg, b_rg, w_re, b_re):
    w = jnp.concatenate([w_rg, w_re], axis=1)
    w = jnp.pad(w, ((0, 0), (0, ROUTER_LANES - w.shape[1])))
    b = jnp.pad(jnp.concatenate([b_rg, b_re]), (0, ROUTER_LANES - N_GROUPS - N_EXPERTS)).reshape(1, ROUTER_LANES)
    w_hi = w.astype(BF16)
    w_lo = (w - w_hi.astype(F32)).astype(BF16)
    return w_hi, w_lo, b


def _layer_weights(l, g_mix, w_in, w_up_a, w_up_b, w_out, g_ffn, w_rg, b_rg, w_re, b_re, w_e_gate, w_e_up, w_e_down):
    n_rope = A_WIDTH + 2 * D_HEAD + IDX_WIDTH + D_IDX + N_IDX_HEADS
    w = w_in[l]
    w_a = jnp.pad(w[:, :n_rope], ((0, 0), (0, -n_rope % LANES))).astype(BF16)
    w_bg = w[:, n_rope:].astype(BF16)
    return dict(
        g_mix=g_mix[l].reshape(1, D_MODEL), w_a=w_a, w_bg=w_bg,
        w_up_a=w_up_a[l].astype(BF16), w_up_b=w_up_b[l].astype(BF16), w_out=w_out[l].astype(BF16),
        g_ffn=g_ffn[l].reshape(1, D_MODEL), router=_split_router(w_rg[l], b_rg[l], w_re[l], b_re[l]),
        w_e_gate=w_e_gate[l], w_e_up=w_e_up[l], w_e_down=w_e_down[l])


def _project(x2d, wts, pos, tm_rope, tm_plain, transposed_k_idx=False):
    tabs = _rope_tables(pos, D_HEAD) + _rope_tables(pos, D_IDX)
    roped = _inproj_rope(x2d, wts["g_mix"], wts["w_a"], tabs, tm_rope, pos.shape[0] // tm_rope, transposed_k_idx)
    return (tuple(roped),) + tuple(_inproj_plain(x2d, wts["g_mix"], wts["w_bg"], tm_plain))


def _pad_rows(a, n):
    return jnp.pad(a, ((0, 0), (0, n - a.shape[1]), (0, 0)))


def kernel(x_prompt, x_sample, cache_a_k, cache_a_v, cache_idx_k, cache_b_k, cache_b_v, page_table,
           g_mix, w_in, w_up_a, w_up_b, w_out, g_ffn, w_rg, b_rg, w_re, b_re,
           w_e_gate, w_e_up, w_e_down, g_final):
    depth = w_in.shape[0]
    batch, seq, _ = x_prompt.shape
    dec_batch, dec_seq, _ = x_sample.shape
    n_pages = page_table.shape[1]
    past = n_pages * PAGE_SIZE
    n_phys = cache_a_k.shape[1]
    gfin = g_final.reshape(1, D_MODEL)
    hp = x_prompt.reshape(batch * seq, D_MODEL)
    hs = x_sample.reshape(dec_batch * dec_seq, D_MODEL)
    rows_s = dec_batch * dec_seq
    new_p, new_s = [], []
    for l in range(depth):
        wts = _layer_weights(l, g_mix, w_in, w_up_a, w_up_b, w_out, g_ffn, w_rg, b_rg, w_re, b_re,
                             w_e_gate, w_e_up, w_e_down)
        last = l == depth - 1

        (q_a, k_a, v_a, q_i, kiwi, k_idx_t), k_b, v_b, qs = _project(hp, wts, np.arange(seq), 512, 1024, True)
        per_seq = lambda a: a.reshape(batch, seq, a.shape[-1])
        o_a = _dsa_prompt(per_seq(q_i), per_seq(kiwi), per_seq(q_a), per_seq(k_a), per_seq(v_a),
                          batch, seq, 256).reshape(batch * seq, A_WIDTH)
        o_b = _sb_prompt(qs, k_b, v_b, batch, seq, 512)
        prompt_rows = (hp, o_a, o_b, qs)
        new_p.append((k_a.reshape(batch, seq, D_HEAD), v_a.reshape(batch, seq, D_HEAD),
                      jnp.swapaxes(k_idx_t, 1, 2),
                      k_b.reshape(batch, seq, N_HEADS_B, D_HEAD), v_b.reshape(batch, seq, N_HEADS_B, D_HEAD)))

        pos_s = np.tile(past + np.arange(dec_seq), dec_batch)
        (q_a, k_a, v_a, q_i, kiwi), k_b, v_b, qs = _project(hs, wts, pos_s, rows_s, rows_s)
        q_b = qs[:, 2 * D_MODEL:]
        per_b = lambda a: a.reshape(dec_batch, dec_seq, -1)
        heads_first = lambda a, nh, d: (per_b(a).reshape(dec_batch, dec_seq, nh, d).transpose(0, 2, 1, 3)
                                        .reshape(dec_batch, nh * dec_seq, d))
        w_idx = per_b(kiwi)[:, :, D_IDX:D_IDX + N_IDX_HEADS] * ((N_IDX_HEADS * D_IDX) ** -0.5)
        wi_s = w_idx.transpose(0, 2, 1).reshape(dec_batch, N_IDX_HEADS * dec_seq, 1)
        kin_s = _pad_rows(per_b(kiwi)[:, :, :D_IDX], LANES)
        o_a = _dsa_sample(page_table, heads_first(q_i, N_IDX_HEADS, D_IDX), wi_s, kin_s,
                          heads_first(q_a, N_HEADS_A, D_HEAD), _pad_rows(per_b(k_a), LANES),
                          _pad_rows(per_b(v_a), LANES), jnp.swapaxes(cache_idx_k[l], 1, 2),
                          cache_a_k[l], cache_a_v[l])
        blk_lanes = SB_BLOCK_KEYS * N_HEADS_B
        kb_new = _pad_rows(k_b.reshape(dec_batch, dec_seq * N_HEADS_B, D_HEAD), blk_lanes)
        vb_new = _pad_rows(v_b.reshape(dec_batch, dec_seq * N_HEADS_B, D_HEAD), blk_lanes)
        pool_rows = lambda c: c.reshape(n_phys, PAGE_SIZE * N_HEADS_B, D_HEAD)
        o_b = _sb_sample(page_table, heads_first(q_b, N_HEADS_B, D_HEAD), kb_new, vb_new,
                         pool_rows(cache_b_k[l]), pool_rows(cache_b_v[l]))
        tok_major = lambda o, nh: (o.reshape(dec_batch, nh, dec_seq, D_HEAD).transpose(0, 2, 1, 3)
                                   .reshape(rows_s, nh * D_HEAD).astype(BF16))
        sample_rows = (hs, tok_major(o_a, N_HEADS_A), tok_major(o_b, N_HEADS_B), qs)
        new_s.append((k_a.reshape(dec_batch, dec_seq, D_HEAD), v_a.reshape(dec_batch, dec_seq, D_HEAD),
                      kiwi[:, :D_IDX].reshape(dec_batch, dec_seq, D_IDX),
                      k_b.reshape(dec_batch, dec_seq, N_HEADS_B, D_HEAD),
                      v_b.reshape(dec_batch, dec_seq, N_HEADS_B, D_HEAD)))

        h, hn, logits = _merge(prompt_rows, sample_rows, wts["w_up_a"], wts["w_up_b"], wts["w_out"], wts["g_ffn"],
                               *wts["router"], MERGE_ROWS)
        hp, hs = _moe(h, hn, logits, gfin, wts["w_e_gate"], wts["w_e_up"], wts["w_e_down"], MERGE_ROWS,
                      batch * seq, last)

    stack = lambda rows_, i: jnp.stack([r[i] for r in rows_])
    return (hp.reshape(batch, seq, D_MODEL), hs.reshape(dec_batch, dec_seq, D_MODEL),
            stack(new_p, 0), stack(new_p, 1), stack(new_p, 2), stack(new_p, 3), stack(new_p, 4),
            stack(new_s, 0), stack(new_s, 1), stack(new_s, 2), stack(new_s, 3), stack(new_s, 4))
```

```python
import functools
import math

import numpy as np
import jax
import jax.numpy as jnp
from jax import lax
from jax.experimental import pallas as pl
from jax.experimental.pallas import tpu as pltpu

D_MODEL = 2048
D_HEAD = 128
N_HEADS_A = 8
N_IDX_HEADS = 16
D_IDX = 64
TOPK_MAX = 256
N_HEADS_B = 8
ROPE_THETA = 10000.0
N_GROUPS = 4
EXPERTS_PER_GROUP = 8
N_EXPERTS = N_GROUPS * EXPERTS_PER_GROUP
TOP_K_EXPERTS = 2
D_EXPERT = 512
PAGE_SIZE = 128
NORM_EPS = 1e-6
A_WIDTH = N_HEADS_A * D_HEAD
B_WIDTH = N_HEADS_B * D_HEAD
IDX_WIDTH = N_IDX_HEADS * D_IDX

LANES = 128
VMEM_LIMIT = 56 * 1024 * 1024
ROPE_ROWS = 512
PLAIN_ROWS = 1024
DSA_Q_ROWS = 256
SB_Q_ROWS = 512
EXPERT_ROWS = 256
EXPERT_K_CHUNK = 256
SB_HEADS_PER_STEP = 4
MERGE_ROWS = 256
ROUTER_LANES = 128

assert TOP_K_EXPERTS == 2

F32 = jnp.float32
BF16 = jnp.bfloat16
NEG_INF = float("-inf")
INT_MIN = -2 ** 31


def _dot(a, b):
    return jnp.dot(a, b, preferred_element_type=F32)


def _dot_nt(a, b):
    return lax.dot_general(a, b, (((1,), (1,)), ((), ())), preferred_element_type=F32)


def _params(semantics):
    return pltpu.CompilerParams(dimension_semantics=semantics, vmem_limit_bytes=VMEM_LIMIT)


def _rope_tables(pos, d):
    half = d // 2
    inv_freq = np.exp(np.arange(half, dtype=np.float32) * np.float32(-2.0 * math.log(ROPE_THETA) / d))
    ang = np.asarray(pos, np.float32)[:, None] * inv_freq[None, :]
    cos, sin = np.cos(ang), np.sin(ang)
    reps = LANES // d
    cos_t = np.tile(np.concatenate([cos, cos], axis=-1), (1, reps))
    sin_t = np.tile(np.concatenate([-sin, sin], axis=-1), (1, reps))
    return jnp.asarray(cos_t, F32), jnp.asarray(sin_t, F32)


def _swap_halves(blk, d):
    if d == LANES:
        return pltpu.roll(blk, LANES // 2, 1)
    lane = lax.broadcasted_iota(jnp.int32, blk.shape, 1)
    first_half = (lane % d) < (d // 2)
    return jnp.where(first_half, pltpu.roll(blk, LANES - d // 2, 1), pltpu.roll(blk, d // 2, 1))


def _rmsnorm_rows(x, g):
    inv = lax.rsqrt(jnp.mean(x * x, axis=-1, keepdims=True) + NORM_EPS)
    return x * inv * g


def _inproj_rope_kernel(x_ref, g_ref, w_ref, cos_ref, sin_ref, cosi_ref, sini_ref,
                        qa_ref, ka_ref, va_ref, qi_ref, kiwi_ref, kit_ref=None):
    xn = _rmsnorm_rows(x_ref[...], g_ref[...]).astype(BF16)
    y = _dot(xn, w_ref[...])
    cos, sin = cos_ref[...], sin_ref[...]
    for h in range(N_HEADS_A):
        blk = y[:, h * D_HEAD:(h + 1) * D_HEAD]
        qa_ref[:, h * D_HEAD:(h + 1) * D_HEAD] = (blk * cos + _swap_halves(blk, D_HEAD) * sin).astype(BF16)
    ka = y[:, A_WIDTH:A_WIDTH + D_HEAD]
    ka_ref[...] = ka * cos + _swap_halves(ka, D_HEAD) * sin
    va_ref[...] = y[:, A_WIDTH + D_HEAD:A_WIDTH + 2 * D_HEAD]
    cosi, sini = cosi_ref[...], sini_ref[...]
    base = A_WIDTH + 2 * D_HEAD
    for j in range(IDX_WIDTH // LANES):
        blk = y[:, base + j * LANES:base + (j + 1) * LANES]
        qi_ref[:, j * LANES:(j + 1) * LANES] = (blk * cosi + _swap_halves(blk, D_IDX) * sini).astype(BF16)
    blk = y[:, base + IDX_WIDTH:base + IDX_WIDTH + LANES]
    lane = lax.broadcasted_iota(jnp.int32, blk.shape, 1)
    is_k = lane < D_IDX
    kiwi = blk * jnp.where(is_k, cosi, 1.0) + _swap_halves(blk, D_IDX) * jnp.where(is_k, sini, 0.0)
    kiwi_ref[...] = kiwi
    if kit_ref is not None:
        kit_ref[0] = kiwi.T[:D_IDX, :]


def _inproj_rope(x, g, w_a, tabs, tm, n_pos_blocks, transposed_k_idx):
    rows = x.shape[0]
    wa_cols = w_a.shape[1]
    cos, sin, cosi, sini = tabs
    row_spec = lambda width: pl.BlockSpec((tm, width), lambda i: (i, 0))
    tab_spec = pl.BlockSpec((tm, LANES), lambda i: (i % n_pos_blocks, 0))
    out_specs = [row_spec(A_WIDTH), row_spec(D_HEAD), row_spec(D_HEAD), row_spec(IDX_WIDTH), row_spec(LANES)]
    out_shape = [jax.ShapeDtypeStruct((rows, A_WIDTH), BF16),
                 jax.ShapeDtypeStruct((rows, D_HEAD), F32),
                 jax.ShapeDtypeStruct((rows, D_HEAD), F32),
                 jax.ShapeDtypeStruct((rows, IDX_WIDTH), BF16),
                 jax.ShapeDtypeStruct((rows, LANES), F32)]
    if transposed_k_idx:
        out_specs.append(pl.BlockSpec((1, D_IDX, tm), lambda i: (i // n_pos_blocks, 0, i % n_pos_blocks)))
        out_shape.append(jax.ShapeDtypeStruct((rows // (n_pos_blocks * tm), D_IDX, n_pos_blocks * tm), F32))
    return pl.pallas_call(
        _inproj_rope_kernel,
        grid=(rows // tm,),
        in_specs=[row_spec(D_MODEL),
                  pl.BlockSpec((1, D_MODEL), lambda i: (0, 0)),
                  pl.BlockSpec((D_MODEL, wa_cols), lambda i: (0, 0)),
                  tab_spec, tab_spec, tab_spec, tab_spec],
        out_specs=out_specs,
        out_shape=out_shape,
        compiler_params=_params(("arbitrary",)),
        name="inproj_rope",
    )(x, g, w_a, cos, sin, cosi, sini)


N_GATE_TILES = 2 * D_MODEL // B_WIDTH


def _inproj_plain_kernel(x_ref, g_ref, w_ref, kb_ref, vb_ref, qs_ref, xn_ref):
    j = pl.program_id(1)

    @pl.when(j == 0)
    def _():
        xn_ref[...] = _rmsnorm_rows(x_ref[...], g_ref[...]).astype(BF16)

    product = lambda: _dot(xn_ref[...], w_ref[...])

    @pl.when(j == 0)
    def _():
        qs_ref[...] = product().astype(BF16)

    @pl.when(j == 1)
    def _():
        kb_ref[...] = product()

    @pl.when(j == 2)
    def _():
        vb_ref[...] = product()

    @pl.when(j >= 3)
    def _():
        qs_ref[...] = (1.0 / (1.0 + jnp.exp(-product()))).astype(BF16)


def _inproj_plain(x, g, w_bg, tm):
    rows = x.shape[0]
    tn = B_WIDTH
    n_col = w_bg.shape[1] // tn
    assert n_col == 3 + N_GATE_TILES
    fixed = pl.BlockSpec((tm, tn), lambda i, j: (i, 0))
    qs_map = lambda i, j: (i, jnp.where(j < 3, N_GATE_TILES, j - 3))
    return pl.pallas_call(
        _inproj_plain_kernel,
        grid=(rows // tm, n_col),
        in_specs=[pl.BlockSpec((tm, D_MODEL), lambda i, j: (i, 0)),
                  pl.BlockSpec((1, D_MODEL), lambda i, j: (0, 0)),
                  pl.BlockSpec((D_MODEL, tn), lambda i, j: (0, j))],
        out_specs=[fixed, fixed, pl.BlockSpec((tm, tn), qs_map)],
        out_shape=[jax.ShapeDtypeStruct((rows, tn), F32), jax.ShapeDtypeStruct((rows, tn), F32),
                   jax.ShapeDtypeStruct((rows, (N_GATE_TILES + 1) * tn), BF16)],
        scratch_shapes=[pltpu.VMEM((tm, D_MODEL), BF16)],
        compiler_params=_params(("arbitrary", "arbitrary")),
        name="inproj_plain",
    )(x, g, w_bg)


def _sortable(x):
    b = lax.bitcast_convert_type(x, jnp.int32)
    return b ^ ((b >> 31) & jnp.int32(0x7FFFFFFF))


def _select_topk(keys_ref, neg_ref, valid_fn, k_top):
    rows, width = keys_ref.shape
    n_chunks = width // LANES
    ones = jnp.ones((LANES, LANES), BF16)
    lane = lax.broadcasted_iota(jnp.int32, (rows, LANES), 1)

    def lane_total(acc):
        return _dot(acc.astype(BF16), ones)

    def count(pred):
        acc = jnp.zeros((rows, LANES), F32)
        for c in range(n_chunks):
            acc = acc + jnp.where(pred(keys_ref[:, c * LANES:(c + 1) * LANES], c), 1.0, 0.0)
        return lane_total(acc)

    def value_step(i, ans):
        cand_u = ans | lax.shift_left(jnp.int32(1), 31 - i)
        cand = cand_u ^ jnp.int32(INT_MIN)
        cnt = count(lambda k, c: k >= cand)
        return jnp.where(cnt >= k_top, cand_u, ans)

    thr = lax.fori_loop(0, 32, value_step, jnp.zeros((rows, LANES), jnp.int32)) ^ jnp.int32(INT_MIN)
    n_gt = count(lambda k, c: k > thr)
    n_ge = count(lambda k, c: k >= thr)
    need = k_top - n_gt

    n_bits = (width - 1).bit_length()

    def index_step(i, cst):
        cand = cst | lax.shift_left(jnp.int32(1), n_bits - 1 - i)
        cnt = count(lambda k, c: (k == thr) & (lane + c * LANES < cand))
        return jnp.where(cnt < need, cand, cst)

    cut = lax.cond(jnp.max(n_ge) > k_top,
                   lambda: lax.fori_loop(0, n_bits, index_step, jnp.zeros((rows, LANES), jnp.int32)),
                   lambda: jnp.full((rows, LANES), width, jnp.int32))

    for c in range(n_chunks):
        k = keys_ref[:, c * LANES:(c + 1) * LANES]
        tie_keep = jnp.where(lane + c * LANES <= cut, 0.0, NEG_INF)
        keep = jnp.where(k > thr, 0.0, jnp.where(k == thr, tie_keep, NEG_INF))
        neg_ref[:, c * LANES:(c + 1) * LANES] = jnp.where(valid_fn(c), keep, NEG_INF)


def _dsa_prompt_kernel(qi_ref, wq_ref, kk_ref, qa_ref, ka_ref, va_ref, *rest, k_top, q_start):
    _, o_ref, keys_ref, neg_ref = rest
    tq, width = keys_ref.shape

    w = wq_ref[0, :, D_IDX:D_IDX + N_IDX_HEADS] * ((N_IDX_HEADS * D_IDX) ** -0.5)
    k_idx = kk_ref[0, :, :D_IDX].astype(BF16)
    score = jnp.zeros((tq, width), F32)
    for h in range(N_IDX_HEADS):
        rel = jnp.maximum(_dot_nt(qi_ref[0, :, h * D_IDX:(h + 1) * D_IDX], k_idx), 0.0)
        score = score + rel * w[:, h:h + 1]

    t_pos = q_start + lax.broadcasted_iota(jnp.int32, (tq, LANES), 0)
    lane = lax.broadcasted_iota(jnp.int32, (tq, LANES), 1)
    causal = lambda c: lane + c * LANES <= t_pos
    s_pos = lax.broadcasted_iota(jnp.int32, (tq, width), 1)
    q_pos = q_start + lax.broadcasted_iota(jnp.int32, (tq, width), 0)
    keys_ref[...] = _sortable(jnp.where(s_pos <= q_pos, score, NEG_INF))
    _select_topk(keys_ref, neg_ref, causal, k_top)

    k_att = ka_ref[0].astype(BF16)
    v_att = va_ref[0].astype(BF16)
    exp2_scale = (D_HEAD ** -0.5) * math.log2(math.e)
    for h in range(N_HEADS_A):
        s = _dot_nt(qa_ref[0, :, h * D_HEAD:(h + 1) * D_HEAD], k_att) + neg_ref[...]
        m = jnp.max(s, axis=-1, keepdims=True)
        e = jnp.exp2((s - m) * exp2_scale)
        l = jnp.sum(e, axis=-1, keepdims=True)
        o = _dot(e.astype(BF16), v_att) / l
        o_ref[0, :, h * D_HEAD:(h + 1) * D_HEAD] = o.astype(BF16)


def _dsa_prompt(q_i, kiwi, q_a, k_a, v_a, batch, seq, tq):
    k_top = min(TOPK_MAX, seq // 4)
    out = jnp.zeros((batch, seq, A_WIDTH), BF16)
    for qb in range(seq // tq):
        width = (qb + 1) * tq
        q_spec = lambda w_: pl.BlockSpec((1, tq, w_), lambda b: (b, qb, 0))
        kv_spec = lambda w_: pl.BlockSpec((1, width, w_), lambda b: (b, 0, 0))
        out = pl.pallas_call(
            functools.partial(_dsa_prompt_kernel, k_top=k_top, q_start=qb * tq),
            grid=(batch,),
            in_specs=[q_spec(IDX_WIDTH), q_spec(LANES), kv_spec(LANES), q_spec(A_WIDTH), kv_spec(D_HEAD),
                      kv_spec(D_HEAD), pl.BlockSpec(memory_space=pl.ANY)],
            out_specs=q_spec(A_WIDTH),
            out_shape=jax.ShapeDtypeStruct((batch, seq, A_WIDTH), BF16),
            scratch_shapes=[pltpu.VMEM((tq, width), jnp.int32), pltpu.VMEM((tq, width), F32)],
            input_output_aliases={6: 0},
            compiler_params=_params(("arbitrary",)),
            name=f"dsa_prompt_q{qb}",
        )(q_i, kiwi, kiwi, q_a, k_a, v_a, out)
    return out


def _suffix_matrix(n, reps=1):
    j = lax.broadcasted_iota(jnp.int32, (n * reps, n * reps), 0)
    s = lax.broadcasted_iota(jnp.int32, (n * reps, n * reps), 1)
    same = (j % reps) == (s % reps)
    later = jnp.where(same & (j // reps > s // reps), 1.0, 0.0)
    total = jnp.where(same, 1.0, 0.0)
    return jnp.concatenate([later, total], axis=1).astype(BF16)


def _softplus(z):
    return jnp.maximum(z, 0.0) + jnp.log(1.0 + jnp.exp(-jnp.abs(z)))


def _sb_prompt_kernel(q_ref, k_ref, v_ref, u_ref, o_ref, k16, v16, acc_ref, later_ref):
    qb = pl.program_id(2)
    tq = q_ref.shape[0]
    sub = LANES
    n_sub = tq // sub

    @pl.when(qb == 0)
    def _():
        k16[...] = k_ref[...].astype(BF16)
        v16[...] = v_ref[...].astype(BF16)

    u = u_ref[...]
    n_heads = q_ref.shape[1] // D_HEAD

    def block(kb, diagonal):
        start = pl.multiple_of(kb * tq, tq)
        for hh in range(n_heads):
            cols = slice(hh * D_HEAD, (hh + 1) * D_HEAD)
            z = _dot_nt(q_ref[:, cols], k16[pl.ds(start, tq), cols]) * (D_HEAD ** -0.5)
            sp = _softplus(z)
            log_beta = z - sp
            if diagonal:
                before = (lax.broadcasted_iota(jnp.int32, (tq, tq), 1)
                          < lax.broadcasted_iota(jnp.int32, (tq, tq), 0))
                sp = jnp.where(before, sp, 0.0)
            sp16 = sp.astype(BF16)
            later = later_ref[:, cols]
            args = [None] * n_sub
            for j in reversed(range(n_sub)):
                cum = _dot(sp16[:, j * sub:(j + 1) * sub], u)
                args[j] = log_beta[:, j * sub:(j + 1) * sub] - cum[:, :sub] - later
                later = later + cum[:, sub:]
            a = jnp.exp(jnp.concatenate(args, axis=1))
            if diagonal:
                a = jnp.where(before, a, 0.0)
            acc_ref[:, cols] += _dot(a.astype(BF16), v16[pl.ds(start, tq), cols])
            later_ref[:, cols] = later

    acc_ref[...] = jnp.zeros_like(acc_ref)
    later_ref[...] = jnp.zeros_like(later_ref)
    block(qb, True)

    def body(it, carry):
        block(qb - 1 - it, False)
        return carry

    lax.fori_loop(0, qb, body, 0)
    o_ref[...] = acc_ref[...].astype(BF16)


def _sb_prompt(qs, k_b, v_b, batch, seq, tq):
    rows = batch * seq
    nq = seq // tq
    hw = SB_HEADS_PER_STEP * D_HEAD
    q_col = (qs.shape[1] - B_WIDTH) // hw
    q_spec = pl.BlockSpec((tq, hw), lambda b, h, q: (b * nq + q, q_col + h))
    kv_spec = pl.BlockSpec((seq, hw), lambda b, h, q: (b, h))
    return pl.pallas_call(
        _sb_prompt_kernel,
        grid=(batch, N_HEADS_B // SB_HEADS_PER_STEP, nq),
        in_specs=[q_spec, kv_spec, kv_spec, pl.BlockSpec((LANES, 2 * LANES), lambda b, h, q: (0, 0))],
        out_specs=pl.BlockSpec((tq, hw), lambda b, h, q: (b * nq + q, h)),
        out_shape=jax.ShapeDtypeStruct((rows, B_WIDTH), BF16),
        scratch_shapes=[pltpu.VMEM((seq, hw), BF16), pltpu.VMEM((seq, hw), BF16),
                        pltpu.VMEM((tq, hw), F32), pltpu.VMEM((tq, hw), F32)],
        compiler_params=_params(("arbitrary", "arbitrary", "arbitrary")),
        name="sb_prompt",
    )(qs, k_b, v_b, _suffix_matrix(LANES))


def _page_copies(pt_ref, batch_idx, first_page, n_pages, pools, bufs, slot, sem, rows_per_page, n_logical):
    copies = []
    for p in range(n_pages):
        phys = pt_ref[batch_idx * n_logical + first_page + p]
        for pool, buf in zip(pools, bufs):
            copies.append(pltpu.make_async_copy(
                pool.at[phys], buf.at[slot, pl.ds(p * rows_per_page, rows_per_page)], sem.at[slot]))
    return copies


def _paged_double_buffer(pt_ref, pools, bufs, sem, n_pages, rows_per_page):
    b = pl.program_id(0)
    slot = b % 2
    copies = lambda bb, ss: _page_copies(pt_ref, bb, 0, n_pages, pools, bufs, ss, sem, rows_per_page, n_pages)

    @pl.when(b == 0)
    def _():
        for c in copies(0, 0):
            c.start()

    @pl.when(b + 1 < pl.num_programs(0))
    def _():
        for c in copies(b + 1, 1 - slot):
            c.start()

    for c in copies(b, slot):
        c.wait()
    return slot


def _new_key_visible(n_q, n_new, rows):
    t_idx = lax.broadcasted_iota(jnp.int32, (rows, LANES), 0) % n_q
    lane = lax.broadcasted_iota(jnp.int32, (rows, LANES), 1)
    return (lane <= t_idx) & (lane < n_new)


def _idx_scores_sample_kernel(pt_ref, qi_ref, w_ref, kin_ref, idx_pool, s_ref, idx_buf, sem, *, n_pages, n_new):
    b = pl.program_id(0)
    slot = b % 2
    past = n_pages * PAGE_SIZE

    def copies(bb, ss):
        return [pltpu.make_async_copy(idx_pool.at[pt_ref[bb * n_pages + p]],
                                      idx_buf.at[ss, :, pl.ds(p * PAGE_SIZE, PAGE_SIZE)], sem.at[ss])
                for p in range(n_pages)]

    @pl.when(b == 0)
    def _():
        for c in copies(0, 0):
            c.start()

    @pl.when(b + 1 < pl.num_programs(0))
    def _():
        for c in copies(b + 1, 1 - slot):
            c.start()

    for c in copies(b, slot):
        c.wait()

    q = qi_ref[0]
    w = w_ref[0]
    n_q = q.shape[0] // N_IDX_HEADS

    def head_sum(qk):
        rel = jnp.maximum(qk, 0.0) * w
        return functools.reduce(lambda a, c: a + c, [rel[h * n_q:(h + 1) * n_q] for h in range(N_IDX_HEADS)])

    s_ref[0, :, :past] = head_sum(_dot(q, idx_buf[slot].astype(BF16)))
    s_new = head_sum(_dot_nt(q, kin_ref[0].astype(BF16)))
    s_ref[0, :, past:] = jnp.where(_new_key_visible(n_q, n_new, n_q), s_new, NEG_INF)


def _topk_mask_kernel(s_ref, neg_ref, keys_ref, *, k_top, n_q, n_new):
    rows, width = keys_ref.shape
    keys_ref[...] = _sortable(s_ref[...])
    visible = _new_key_visible(n_q, n_new, rows)
    last = width // LANES - 1
    _select_topk(keys_ref, neg_ref, lambda c: visible if c == last else jnp.full((rows, LANES), True), k_top)


def _dsa_sample_attend_kernel(pt_ref, qa_ref, neg_ref, kan_ref, van_ref, k_pool, v_pool, o_ref,
                              k_buf, v_buf, sem, *, n_pages):
    slot = _paged_double_buffer(pt_ref, (k_pool, v_pool), (k_buf, v_buf), sem, n_pages, PAGE_SIZE)
    past = n_pages * PAGE_SIZE
    k_att = k_buf[slot].astype(BF16)
    v_att = v_buf[slot].astype(BF16)
    q = qa_ref[0]
    neg = jnp.concatenate([neg_ref[0]] * N_HEADS_A, axis=0)
    s_past = _dot_nt(q, k_att) + neg[:, :past]
    s_new = _dot_nt(q, kan_ref[0].astype(BF16)) + neg[:, past:]
    m = jnp.maximum(jnp.max(s_past, axis=-1, keepdims=True), jnp.max(s_new, axis=-1, keepdims=True))
    exp2_scale = (D_HEAD ** -0.5) * math.log2(math.e)
    e_past = jnp.exp2((s_past - m) * exp2_scale)
    e_new = jnp.exp2((s_new - m) * exp2_scale)
    l = jnp.sum(e_past, axis=-1, keepdims=True) + jnp.sum(e_new, axis=-1, keepdims=True)
    o = _dot(e_past.astype(BF16), v_att) + _dot(e_new.astype(BF16), van_ref[0].astype(BF16))
    o_ref[0] = o / l


def _dsa_sample(page_table, q_i, w_i, k_i_new, q_a, k_a_new, v_a_new, idx_pool, k_pool, v_pool):
    batch, n_pages = page_table.shape
    n_q = q_a.shape[1] // N_HEADS_A
    past = n_pages * PAGE_SIZE
    width = past + LANES
    k_top = min(TOPK_MAX, (past + n_q) // 4)
    pt = page_table.reshape(-1)
    blk = lambda shape: pl.BlockSpec((1,) + shape, lambda b, pt_: (b,) + (0,) * len(shape))
    hbm = pl.BlockSpec(memory_space=pl.ANY)

    scores = pl.pallas_call(
        functools.partial(_idx_scores_sample_kernel, n_pages=n_pages, n_new=n_q),
        grid_spec=pltpu.PrefetchScalarGridSpec(
            num_scalar_prefetch=1, grid=(batch,),
            in_specs=[blk((N_IDX_HEADS * n_q, D_IDX)), blk((N_IDX_HEADS * n_q, 1)), blk((LANES, D_IDX)), hbm],
            out_specs=blk((n_q, width)),
            scratch_shapes=[pltpu.VMEM((2, D_IDX, past), F32), pltpu.SemaphoreType.DMA((2,))]),
        out_shape=jax.ShapeDtypeStruct((batch, n_q, width), F32),
        compiler_params=_params(("arbitrary",)),
        name="idx_scores_sample",
    )(pt, q_i, w_i, k_i_new, idx_pool)

    rows = batch * n_q
    tr = min(rows, LANES)
    neg = pl.pallas_call(
        functools.partial(_topk_mask_kernel, k_top=k_top, n_q=n_q, n_new=n_q),
        grid=(rows // tr,),
        in_specs=[pl.BlockSpec((tr, width), lambda i: (i, 0))],
        out_specs=pl.BlockSpec((tr, width), lambda i: (i, 0)),
        out_shape=jax.ShapeDtypeStruct((rows, width), F32),
        scratch_shapes=[pltpu.VMEM((tr, width), jnp.int32)],
        compiler_params=_params(("arbitrary",)),
        name="topk_mask_sample",
    )(scores.reshape(rows, width))

    return pl.pallas_call(
        functools.partial(_dsa_sample_attend_kernel, n_pages=n_pages),
        grid_spec=pltpu.PrefetchScalarGridSpec(
            num_scalar_prefetch=1, grid=(batch,),
            in_specs=[blk((N_HEADS_A * n_q, D_HEAD)), blk((n_q, width)), blk((LANES, D_HEAD)), blk((LANES, D_HEAD)),
                      hbm, hbm],
            out_specs=blk((N_HEADS_A * n_q, D_HEAD)),
            scratch_shapes=[pltpu.VMEM((2, past, D_HEAD), F32), pltpu.VMEM((2, past, D_HEAD), F32),
                            pltpu.SemaphoreType.DMA((2,))]),
        out_shape=jax.ShapeDtypeStruct((batch, N_HEADS_A * n_q, D_HEAD), F32),
        compiler_params=_params(("arbitrary",)),
        name="dsa_sample_attend",
    )(pt, q_a, neg.reshape(batch, n_q, width), k_a_new, v_a_new, k_pool, v_pool)


SB_CHUNK_PAGES = 8
SB_SLOTS = 4
SB_BLOCK_KEYS = 32


def _sb_sample_kernel(pt_ref, q_ref, kn_ref, vn_ref, u_ref, k_pool, v_pool, o_ref,
                      k_buf, v_buf, acc_ref, later_ref, sem, *, n_pages, n_new, total_steps):
    b = pl.program_id(0)
    j = pl.program_id(1)
    n_chunks = pl.num_programs(1)
    step = b * n_chunks + j
    n_steps = total_steps
    n_slots = k_buf.shape[0]
    slot = step % n_slots
    heads = N_HEADS_B
    rows_per_page = PAGE_SIZE * heads
    blk_lanes = SB_BLOCK_KEYS * heads

    def copies(st, ss):
        bb = st // n_chunks
        chunk = n_chunks - 1 - st % n_chunks
        return _page_copies(pt_ref, bb, chunk * SB_CHUNK_PAGES, SB_CHUNK_PAGES, (k_pool, v_pool),
                            (k_buf, v_buf), ss, sem, rows_per_page, n_pages)

    @pl.when(step == 0)
    def _():
        for ahead in range(min(n_slots - 1, total_steps)):
            for c in copies(ahead, ahead):
                c.start()

    @pl.when(step + n_slots - 1 < n_steps)
    def _():
        for c in copies(step + n_slots - 1, (step + n_slots - 1) % n_slots):
            c.start()

    q = q_ref[0]
    n_q = q.shape[0] // heads
    u = u_ref[...]
    scale = D_HEAD ** -0.5

    def head_diag(zt):
        lane_head = lax.broadcasted_iota(jnp.int32, (n_q, zt.shape[1]), 1) % heads
        out = jnp.zeros((n_q, zt.shape[1]), F32)
        for h in range(heads):
            out = out + jnp.where(lane_head == h, zt[h * n_q:(h + 1) * n_q, :], 0.0)
        return out

    def head_spread(a):
        lane_head = lax.broadcasted_iota(jnp.int32, a.shape, 1) % heads
        return jnp.concatenate([jnp.where(lane_head == h, a, 0.0) for h in range(heads)], axis=0).astype(BF16)

    def attend(k16, v16, before, later):
        n = k16.shape[0]
        z = head_diag(_dot_nt(q, k16) * scale)
        sp = _softplus(z)
        log_beta = z - sp
        sp16 = jnp.where(before, sp, 0.0).astype(BF16)
        pieces = []
        for blk in reversed(range(n // blk_lanes)):
            sl = slice(blk * blk_lanes, (blk + 1) * blk_lanes)
            cum = _dot(sp16[:, sl], u)
            pieces.append(log_beta[:, sl] - cum[:, :blk_lanes] - later)
            later = later + cum[:, blk_lanes:]
        a = jnp.exp(jnp.concatenate(pieces[::-1], axis=1))
        a = jnp.where(before, a, 0.0)
        return _dot(head_spread(a), v16), later

    @pl.when(j == 0)
    def _():
        t_idx = lax.broadcasted_iota(jnp.int32, (n_q, blk_lanes), 0)
        key = lax.broadcasted_iota(jnp.int32, (n_q, blk_lanes), 1) // heads
        before = (key < t_idx) & (key < n_new)
        out, later = attend(kn_ref[0].astype(BF16), vn_ref[0].astype(BF16), before,
                            jnp.zeros((n_q, blk_lanes), F32))
        acc_ref[...] = out
        later_ref[...] = later

    for c in copies(step, slot):
        c.wait()

    n_rows = SB_CHUNK_PAGES * rows_per_page
    out, later = attend(k_buf[slot].astype(BF16), v_buf[slot].astype(BF16),
                        jnp.full((n_q, n_rows), True), later_ref[...])
    acc_ref[...] += out
    later_ref[...] = later

    @pl.when(j == n_chunks - 1)
    def _():
        o_ref[0] = acc_ref[...]


def _sb_sample(page_table, q, k_new, v_new, k_pool, v_pool):
    batch, n_pages = page_table.shape
    rows_q = q.shape[1]
    n_q = rows_q // N_HEADS_B
    blk_lanes = SB_BLOCK_KEYS * N_HEADS_B
    n_chunks = n_pages // SB_CHUNK_PAGES
    chunk_rows = SB_CHUNK_PAGES * PAGE_SIZE * N_HEADS_B
    blk = lambda shape: pl.BlockSpec((1,) + shape, lambda b, j, pt: (b,) + (0,) * len(shape))
    hbm = pl.BlockSpec(memory_space=pl.ANY)
    grid_spec = pltpu.PrefetchScalarGridSpec(
        num_scalar_prefetch=1,
        grid=(batch, n_chunks),
        in_specs=[blk((rows_q, D_HEAD)), blk((blk_lanes, D_HEAD)), blk((blk_lanes, D_HEAD)),
                  pl.BlockSpec((blk_lanes, 2 * blk_lanes), lambda b, j, pt: (0, 0)), hbm, hbm],
        out_specs=blk((rows_q, D_HEAD)),
        scratch_shapes=[pltpu.VMEM((SB_SLOTS, chunk_rows, D_HEAD), F32), pltpu.VMEM((SB_SLOTS, chunk_rows, D_HEAD), F32),
                        pltpu.VMEM((rows_q, D_HEAD), F32), pltpu.VMEM((n_q, blk_lanes), F32),
                        pltpu.SemaphoreType.DMA((SB_SLOTS,))],
    )
    return pl.pallas_call(
        functools.partial(_sb_sample_kernel, n_pages=n_pages, n_new=n_q, total_steps=batch * n_chunks),
        grid_spec=grid_spec,
        out_shape=jax.ShapeDtypeStruct((batch, rows_q, D_HEAD), F32),
        compiler_params=_params(("arbitrary", "arbitrary")),
        name="sb_sample",
    )(page_table.reshape(-1), q, k_new, v_new, _suffix_matrix(SB_BLOCK_KEYS, N_HEADS_B), k_pool, v_pool)


PACKED_WIDTH = D_MODEL // 2


def _pack_bf16_pair(lo, hi):
    lo_bits = lax.bitcast_convert_type(lo.astype(BF16).astype(F32), jnp.uint32)
    hi_bits = lax.bitcast_convert_type(hi.astype(BF16).astype(F32), jnp.uint32)
    return (lo_bits >> 16) | (hi_bits & jnp.uint32(0xFFFF0000))


def _unpack_bf16_pair(words):
    lo = lax.bitcast_convert_type(words << 16, F32)
    hi = lax.bitcast_convert_type(words & jnp.uint32(0xFFFF0000), F32)
    return lo, hi


def _merge_kernel(*refs, n_prompt_tiles):
    prompt, sample = refs[0:4], refs[4:8]
    wua_ref, wub_ref, wo_ref, gf_ref, wr_hi_ref, wr_lo_ref, br_ref, h_ref, hn_ref, lg_ref = refs[8:]

    def tile(x_ref, oa_ref, ob_ref, sig_ref):
        sig_a = sig_ref[:, :D_MODEL].astype(F32)
        sig_b = sig_ref[:, D_MODEL:].astype(F32)
        merged = sig_a * _dot(oa_ref[...], wua_ref[...]) + sig_b * _dot(ob_ref[...], wub_ref[...])
        h = x_ref[...] + _dot(merged.astype(BF16), wo_ref[...])
        h_ref[...] = h
        hn = _rmsnorm_rows(h, gf_ref[...])
        hn_ref[...] = _pack_bf16_pair(hn[:, :PACKED_WIDTH], hn[:, PACKED_WIDTH:])
        hn_hi = hn.astype(BF16)
        hn_lo = (hn - hn_hi.astype(F32)).astype(BF16)
        lg_ref[...] = (_dot(hn_hi, wr_hi_ref[...]) + _dot(hn_lo, wr_hi_ref[...])
                       + _dot(hn_hi, wr_lo_ref[...]) + br_ref[...])

    i = pl.program_id(0)

    @pl.when(i < n_prompt_tiles)
    def _():
        tile(*prompt)

    @pl.when(i >= n_prompt_tiles)
    def _():
        tile(*sample)


def _merge(prompt, sample, w_up_a, w_up_b, w_out, g_ffn, wr_hi, wr_lo, b_r, tm):
    n_p, n_s = prompt[0].shape[0] // tm, sample[0].shape[0] // tm
    assert prompt[0].shape[0] == n_p * tm and sample[0].shape[0] == n_s * tm
    rows = (n_p + n_s) * tm
    widths = (D_MODEL, A_WIDTH, B_WIDTH, 2 * D_MODEL)
    p_spec = lambda width: pl.BlockSpec((tm, width), lambda i: (jnp.minimum(i, n_p - 1), 0))
    s_spec = lambda width: pl.BlockSpec((tm, width), lambda i: (jnp.maximum(i - n_p, 0), 0))
    row_spec = lambda width: pl.BlockSpec((tm, width), lambda i: (i, 0))
    const = lambda shape: pl.BlockSpec(shape, lambda i: (0, 0), pipeline_mode=pl.Buffered(1))
    return pl.pallas_call(
        functools.partial(_merge_kernel, n_prompt_tiles=n_p),
        grid=(n_p + n_s,),
        in_specs=[p_spec(w_) for w_ in widths] + [s_spec(w_) for w_ in widths] + [
                  const((A_WIDTH, D_MODEL)), const((B_WIDTH, D_MODEL)), const((D_MODEL, D_MODEL)),
                  const((1, D_MODEL)), const((D_MODEL, ROUTER_LANES)), const((D_MODEL, ROUTER_LANES)),
                  const((1, ROUTER_LANES))],
        out_specs=[row_spec(D_MODEL), row_spec(PACKED_WIDTH), row_spec(ROUTER_LANES)],
        out_shape=[jax.ShapeDtypeStruct((rows, D_MODEL), F32),
                   jax.ShapeDtypeStruct((rows, PACKED_WIDTH), jnp.uint32),
                   jax.ShapeDtypeStruct((rows, ROUTER_LANES), F32)],
        compiler_params=_params(("arbitrary",)),
        name="merge",
    )(*prompt, *sample, w_up_a, w_up_b, w_out, g_ffn, wr_hi, wr_lo, b_r)


def _router_kernel(lg_ref, eid_ref, gate_ref, *, chunk):
    for c0 in range(0, lg_ref.shape[0], chunk):
        _route_chunk(lg_ref, eid_ref, gate_ref, c0, chunk)


def _route_chunk(lg_ref, eid_ref, gate_ref, c0, chunk):
    lg = lg_ref[c0:c0 + chunk, :].T
    row = lambda r: lg[r:r + 1, :]
    groups = [row(g) for g in range(N_GROUPS)]
    g_max = functools.reduce(jnp.maximum, groups)
    g_sel = jnp.full(g_max.shape, N_GROUPS - 1, jnp.int32)
    for g in reversed(range(N_GROUPS)):
        g_sel = jnp.where(groups[g] == g_max, g, g_sel)
    p_group = 1.0 / functools.reduce(lambda a, c: a + c, [jnp.exp(v - g_max) for v in groups])
    inside = []
    for e in range(EXPERTS_PER_GROUP):
        v = jnp.zeros_like(g_max)
        for g in range(N_GROUPS):
            v = jnp.where(g_sel == g, row(N_GROUPS + g * EXPERTS_PER_GROUP + e), v)
        inside.append(v)

    def top1(vals, skip):
        best = functools.reduce(jnp.maximum, [jnp.where(skip == e, NEG_INF, v) for e, v in enumerate(vals)])
        idx = jnp.full(best.shape, EXPERTS_PER_GROUP - 1, jnp.int32)
        for e in reversed(range(EXPERTS_PER_GROUP)):
            idx = jnp.where((vals[e] == best) & (skip != e), e, idx)
        return best, idx

    t1, i1 = top1(inside, jnp.full(g_max.shape, -1, jnp.int32))
    t2, i2 = top1(inside, i1)
    r = jnp.exp(t2 - t1)
    for k, ids in enumerate((g_sel * EXPERTS_PER_GROUP + i1, g_sel * EXPERTS_PER_GROUP + i2)):
        for c in range(chunk // LANES):
            row0 = c0 // LANES + c
            eid_ref[k, row0:row0 + 1, :] = ids[:, c * LANES:(c + 1) * LANES]
    gates = jnp.concatenate([p_group * (1.0 / (1.0 + r)), p_group * (r / (1.0 + r)),
                             jnp.zeros((ROUTER_LANES - TOP_K_EXPERTS, chunk), F32)], axis=0)
    gate_ref[c0:c0 + chunk, :] = gates.T


def _router(logits):
    n_tok = logits.shape[0]
    assert n_tok % LANES == 0
    chunk = max(c for c in range(LANES, 2048 + 1, LANES) if n_tok % c == 0)
    return pl.pallas_call(
        functools.partial(_router_kernel, chunk=chunk),
        out_shape=[jax.ShapeDtypeStruct((TOP_K_EXPERTS, n_tok // LANES, LANES), jnp.int32),
                   jax.ShapeDtypeStruct((n_tok, ROUTER_LANES), F32)],
        compiler_params=pltpu.CompilerParams(vmem_limit_bytes=VMEM_LIMIT),
        name="router",
    )(logits)


def _experts_kernel(blk_e_ref, blk_off_ref, blk_cnt_ref, nact_ref, order_ref,
                    hn_hbm, wg_ref, wu_ref, wd_ref, ys_hbm,
                    x_buf, y_buf, wg16, wu16, wd16, in_sem, out_sem, *, n_tok):
    i = pl.program_id(0)
    n_act = nact_ref[0]
    rows = x_buf.shape[1]
    slot = i % 2
    n_pairs = TOP_K_EXPERTS * n_tok

    def gathers(blk, ss):
        off = blk_off_ref[blk]
        copies = []
        for r in range(rows):
            p = order_ref[off + r]
            tok = jnp.where(p >= n_tok, p - n_tok, p)
            copies.append(pltpu.make_async_copy(hn_hbm.at[pl.ds(tok, 1)], x_buf.at[ss, pl.ds(r, 1)], in_sem.at[ss]))
        return copies

    def scatters(blk, ss, real):
        off = blk_off_ref[blk]
        cnt = jnp.where(real, blk_cnt_ref[blk], 0)
        copies = []
        for r in range(rows):
            dst = jnp.where(r < cnt, order_ref[off + r], n_pairs + ss * rows + r)
            copies.append(pltpu.make_async_copy(y_buf.at[ss, pl.ds(r, 1)], ys_hbm.at[pl.ds(dst, 1)], out_sem.at[ss]))
        return copies

    @pl.when(i == 0)
    def _():
        for c in gathers(0, 0):
            c.start()
        y_buf[1] = jnp.zeros(y_buf.shape[1:], y_buf.dtype)
        for ss in range(2):
            spare = pltpu.make_async_copy(y_buf.at[1], ys_hbm.at[pl.ds(n_pairs + ss * rows, rows)], out_sem.at[1])
            spare.start()
            spare.wait()

    @pl.when((i >= 1) & (i < n_act))
    def _():
        for c in scatters(jnp.maximum(i - 2, 0), slot, i >= 2):
            c.wait()

    first_of_expert = (i == 0) | (blk_e_ref[i] != blk_e_ref[jnp.maximum(i - 1, 0)])

    @pl.when((i < n_act) & first_of_expert)
    def _():
        wg16[...] = wg_ref[0].astype(BF16)
        wu16[...] = wu_ref[0].astype(BF16)
        wd16[...] = wd_ref[0].astype(BF16)

    @pl.when(i < n_act)
    def _():
        for c in gathers(i, slot):
            c.wait()
        reads = gathers(jnp.minimum(i + 1, n_act - 1), 1 - slot)
        writes = scatters(jnp.maximum(i - 1, 0), 1 - slot, i >= 1)
        starts = [(c, prio) for pair in zip(reads, writes) for prio, c in enumerate(pair)]
        n_chunks = 2 * D_MODEL // EXPERT_K_CHUNK
        per_chunk = len(starts) // n_chunks
        chunk_no = 0

        def issue_some():
            nonlocal chunk_no
            for c, prio in starts[chunk_no * per_chunk:(chunk_no + 1) * per_chunk]:
                c.start(priority=prio)
            chunk_no += 1

        g = jnp.zeros((rows, D_EXPERT), F32)
        u = jnp.zeros((rows, D_EXPERT), F32)
        for k0 in range(0, D_MODEL, EXPERT_K_CHUNK):
            c0 = k0 % PACKED_WIDTH
            xc = _unpack_bf16_pair(x_buf[slot, :, c0:c0 + EXPERT_K_CHUNK])[k0 // PACKED_WIDTH].astype(BF16)
            g = g + _dot(xc, wg16[k0:k0 + EXPERT_K_CHUNK, :])
            u = u + _dot(xc, wu16[k0:k0 + EXPERT_K_CHUNK, :])
            issue_some()
        act = (g * (1.0 / (1.0 + jnp.exp(-g))) * u).astype(BF16)
        for n0 in range(0, PACKED_WIDTH, EXPERT_K_CHUNK):
            lo = _dot(act, wd16[:, n0:n0 + EXPERT_K_CHUNK])
            issue_some()
            hi = _dot(act, wd16[:, PACKED_WIDTH + n0:PACKED_WIDTH + n0 + EXPERT_K_CHUNK])
            y_buf[slot, :, n0:n0 + EXPERT_K_CHUNK] = _pack_bf16_pair(lo, hi)
            issue_some()

    @pl.when(i == n_act - 1)
    def _():
        for c in scatters(i, slot, True):
            c.start(priority=1)
        for c in gathers(i, 1 - slot):
            c.wait()
        for c in scatters(jnp.maximum(i - 1, 0), 1 - slot, i >= 1):
            c.wait()
        for c in scatters(i, slot, True):
            c.wait()


def _experts(blk_e, blk_off, blk_cnt, n_act, order, hn, w_gate, w_up, w_down):
    n_blocks = blk_e.shape[0]
    n_tok = hn.shape[0]
    rows = EXPERT_ROWS
    w_spec = lambda shape: pl.BlockSpec((1,) + shape, lambda i, be, *_: (be[i], 0, 0))
    row_buf = pltpu.VMEM((2, rows, PACKED_WIDTH), jnp.uint32)
    grid_spec = pltpu.PrefetchScalarGridSpec(
        num_scalar_prefetch=5,
        grid=(n_blocks,),
        in_specs=[pl.BlockSpec(memory_space=pl.ANY),
                  w_spec((D_MODEL, D_EXPERT)), w_spec((D_MODEL, D_EXPERT)), w_spec((D_EXPERT, D_MODEL))],
        out_specs=pl.BlockSpec(memory_space=pl.ANY),
        scratch_shapes=[row_buf, row_buf,
                        pltpu.VMEM((D_MODEL, D_EXPERT), BF16), pltpu.VMEM((D_MODEL, D_EXPERT), BF16),
                        pltpu.VMEM((D_EXPERT, D_MODEL), BF16),
                        pltpu.SemaphoreType.DMA((2,)), pltpu.SemaphoreType.DMA((2,))],
    )
    return pl.pallas_call(
        functools.partial(_experts_kernel, n_tok=n_tok),
        grid_spec=grid_spec,
        out_shape=jax.ShapeDtypeStruct((TOP_K_EXPERTS * n_tok + 2 * rows, PACKED_WIDTH), jnp.uint32),
        compiler_params=_params(("arbitrary",)),
        name="experts",
    )(blk_e, blk_off, blk_cnt, n_act, order, hn, w_gate, w_up, w_down)


def _combine_kernel(h_ref, gate_ref, gfin_ref, *rest, final_norm, n_prompt_tiles):
    yp_ref, ys_ref = rest[-2:]
    y = h_ref[...]
    for k, e_ref in enumerate(rest[:-2]):
        y = y + jnp.concatenate(_unpack_bf16_pair(e_ref[...]), axis=1) * gate_ref[:, k:k + 1]
    y = _rmsnorm_rows(y, gfin_ref[...]) if final_norm else y
    i = pl.program_id(0)

    @pl.when(i < n_prompt_tiles)
    def _():
        yp_ref[...] = y

    @pl.when(i >= n_prompt_tiles)
    def _():
        ys_ref[...] = y


def _combine(h, gates, g_final, ys, tt, n_prompt, final_norm):
    n_tok = h.shape[0]
    n_tiles, n_p = n_tok // tt, n_prompt // tt
    assert n_tok == n_tiles * tt and n_prompt == n_p * tt and n_tiles > n_p
    e_spec = lambda k: pl.BlockSpec((tt, PACKED_WIDTH), lambda i: (k * n_tiles + i, 0))
    return pl.pallas_call(
        functools.partial(_combine_kernel, final_norm=final_norm, n_prompt_tiles=n_p),
        grid=(n_tiles,),
        in_specs=[pl.BlockSpec((tt, D_MODEL), lambda i: (i, 0)),
                  pl.BlockSpec((tt, ROUTER_LANES), lambda i: (i, 0)),
                  pl.BlockSpec((1, D_MODEL), lambda i: (0, 0))] + [e_spec(k) for k in range(TOP_K_EXPERTS)],
        out_specs=[pl.BlockSpec((tt, D_MODEL), lambda i: (jnp.minimum(i, n_p - 1), 0)),
                   pl.BlockSpec((tt, D_MODEL), lambda i: (jnp.maximum(i - n_p, 0), 0))],
        out_shape=[jax.ShapeDtypeStruct((n_prompt, D_MODEL), F32),
                   jax.ShapeDtypeStruct((n_tok - n_prompt, D_MODEL), F32)],
        compiler_params=_params(("arbitrary",)),
        name="combine",
    )(h, gates, g_final, *([ys] * TOP_K_EXPERTS))


def _moe(h, hn, logits, g_final, w_gate, w_up, w_down, tt, n_prompt, final_norm):
    n_tok = h.shape[0]
    eid, gates = _router(logits)
    flat_e = eid.reshape(-1)
    nk = flat_e.shape[0]
    rows = EXPERT_ROWS
    _, order = lax.sort_key_val(flat_e, jnp.arange(nk, dtype=jnp.int32))
    experts = jnp.arange(N_EXPERTS, dtype=jnp.int32)
    counts = jnp.sum((flat_e[:, None] == experts[None, :]).astype(jnp.int32), axis=0)
    inclusive = lambda v: jnp.sum(jnp.where(experts[:, None] <= experts[None, :], v[:, None], 0), axis=0)
    start = inclusive(counts) - counts
    blocks_per_e = (counts + rows - 1) // rows
    blk_end = inclusive(blocks_per_e)
    blk_begin = blk_end - blocks_per_e
    n_blocks = -(-(nk + N_EXPERTS * (rows - 1)) // rows)
    blk = jnp.arange(n_blocks, dtype=jnp.int32)
    mine = (blk[:, None] >= blk_begin[None, :]) & (blk[:, None] < blk_end[None, :])
    pick = lambda v: jnp.sum(jnp.where(mine, v[None, :], 0), axis=1)
    first = (blk - pick(blk_begin)) * rows
    blk_cnt = jnp.clip(pick(counts) - first, 0, rows)
    blk_off = jnp.clip(pick(start) + first, 0, nk)
    order = jnp.concatenate([order, jnp.zeros((rows,), jnp.int32)])
    ys = _experts(pick(experts), blk_off, blk_cnt, blk_end[-1:], order, hn, w_gate, w_up, w_down)
    return _combine(h, gates, g_final, ys, tt, n_prompt, final_norm)


def _split_router(w_rg, b_rg, w_re, b_re):
    w = jnp.concatenate([w_rg, w_re], axis=1)
    w = jnp.pad(w, ((0, 0), (0, ROUTER_LANES - w.shape[1])))
    b = jnp.pad(jnp.concatenate([b_rg, b_re]), (0, ROUTER_LANES - N_GROUPS - N_EXPERTS)).reshape(1, ROUTER_LANES)
    w_hi = w.astype(BF16)
    w_lo = (w - w_hi.astype(F32)).astype(BF16)
    return w_hi, w_lo, b


def _layer_weights(l, g_mix, w_in, w_up_a, w_up_b, w_out, g_ffn, w_rg, b_rg, w_re, b_re, w_e_gate, w_e_up, w_e_down):
    n_rope = A_WIDTH + 2 * D_HEAD + IDX_WIDTH + D_IDX + N_IDX_HEADS
    w = w_in[l]
    w_a = jnp.pad(w[:, :n_rope], ((0, 0), (0, -n_rope % LANES))).astype(BF16)
    w_bg = w[:, n_rope:].astype(BF16)
    return dict(
        g_mix=g_mix[l].reshape(1, D_MODEL), w_a=w_a, w_bg=w_bg,
        w_up_a=w_up_a[l].astype(BF16), w_up_b=w_up_b[l].astype(BF16), w_out=w_out[l].astype(BF16),
        g_ffn=g_ffn[l].reshape(1, D_MODEL), router=_split_router(w_rg[l], b_rg[l], w_re[l], b_re[l]),
        w_e_gate=w_e_gate[l], w_e_up=w_e_up[l], w_e_down=w_e_down[l])


def _project(x2d, wts, pos, tm_rope, tm_plain, transposed_k_idx=False):
    tabs = _rope_tables(pos, D_HEAD) + _rope_tables(pos, D_IDX)
    roped = _inproj_rope(x2d, wts["g_mix"], wts["w_a"], tabs, tm_rope, pos.shape[0] // tm_rope, transposed_k_idx)
    return (tuple(roped),) + tuple(_inproj_plain(x2d, wts["g_mix"], wts["w_bg"], tm_plain))


def _pad_rows(a, n):
    return jnp.pad(a, ((0, 0), (0, n - a.shape[1]), (0, 0)))


def kernel(x_prompt, x_sample, cache_a_k, cache_a_v, cache_idx_k, cache_b_k, cache_b_v, page_table,
           g_mix, w_in, w_up_a, w_up_b, w_out, g_ffn, w_rg, b_rg, w_re, b_re,
           w_e_gate, w_e_up, w_e_down, g_final):
    depth = w_in.shape[0]
    batch, seq, _ = x_prompt.shape
    dec_batch, dec_seq, _ = x_sample.shape
    n_pages = page_table.shape[1]
    past = n_pages * PAGE_SIZE
    n_phys = cache_a_k.shape[1]
    gfin = g_final.reshape(1, D_MODEL)
    hp = x_prompt.reshape(batch * seq, D_MODEL)
    hs = x_sample.reshape(dec_batch * dec_seq, D_MODEL)
    rows_s = dec_batch * dec_seq
    new_p, new_s = [], []
    for l in range(depth):
        wts = _layer_weights(l, g_mix, w_in, w_up_a, w_up_b, w_out, g_ffn, w_rg, b_rg, w_re, b_re,
                             w_e_gate, w_e_up, w_e_down)
        last = l == depth - 1

        (q_a, k_a, v_a, q_i, kiwi, k_idx_t), k_b, v_b, qs = _project(hp, wts, np.arange(seq), ROPE_ROWS, PLAIN_ROWS,
                                                                     True)
        per_seq = lambda a: a.reshape(batch, seq, a.shape[-1])
        o_a = _dsa_prompt(per_seq(q_i), per_seq(kiwi), per_seq(q_a), per_seq(k_a), per_seq(v_a),
                          batch, seq, DSA_Q_ROWS).reshape(batch * seq, A_WIDTH)
        o_b = _sb_prompt(qs, k_b, v_b, batch, seq, SB_Q_ROWS)
        prompt_rows = (hp, o_a, o_b, qs)
        new_p.append((k_a.reshape(batch, seq, D_HEAD), v_a.reshape(batch, seq, D_HEAD),
                      jnp.swapaxes(k_idx_t, 1, 2),
                      k_b.reshape(batch, seq, N_HEADS_B, D_HEAD), v_b.reshape(batch, seq, N_HEADS_B, D_HEAD)))

        pos_s = np.tile(past + np.arange(dec_seq), dec_batch)
        (q_a, k_a, v_a, q_i, kiwi), k_b, v_b, qs = _project(hs, wts, pos_s, rows_s, rows_s)
        q_b = qs[:, 2 * D_MODEL:]
        per_b = lambda a: a.reshape(dec_batch, dec_seq, -1)
        heads_first = lambda a, nh, d: (per_b(a).reshape(dec_batch, dec_seq, nh, d).transpose(0, 2, 1, 3)
                                        .reshape(dec_batch, nh * dec_seq, d))
        w_idx = per_b(kiwi)[:, :, D_IDX:D_IDX + N_IDX_HEADS] * ((N_IDX_HEADS * D_IDX) ** -0.5)
        wi_s = w_idx.transpose(0, 2, 1).reshape(dec_batch, N_IDX_HEADS * dec_seq, 1)
        kin_s = _pad_rows(per_b(kiwi)[:, :, :D_IDX], LANES)
        o_a = _dsa_sample(page_table, heads_first(q_i, N_IDX_HEADS, D_IDX), wi_s, kin_s,
                          heads_first(q_a, N_HEADS_A, D_HEAD), _pad_rows(per_b(k_a), LANES),
                          _pad_rows(per_b(v_a), LANES), jnp.swapaxes(cache_idx_k[l], 1, 2),
                          cache_a_k[l], cache_a_v[l])
        blk_lanes = SB_BLOCK_KEYS * N_HEADS_B
        kb_new = _pad_rows(k_b.reshape(dec_batch, dec_seq * N_HEADS_B, D_HEAD), blk_lanes)
        vb_new = _pad_rows(v_b.reshape(dec_batch, dec_seq * N_HEADS_B, D_HEAD), blk_lanes)
        pool_rows = lambda c: c.reshape(n_phys, PAGE_SIZE * N_HEADS_B, D_HEAD)
        o_b = _sb_sample(page_table, heads_first(q_b, N_HEADS_B, D_HEAD), kb_new, vb_new,
                         pool_rows(cache_b_k[l]), pool_rows(cache_b_v[l]))
        tok_major = lambda o, nh: (o.reshape(dec_batch, nh, dec_seq, D_HEAD).transpose(0, 2, 1, 3)
                                   .reshape(rows_s, nh * D_HEAD).astype(BF16))
        sample_rows = (hs, tok_major(o_a, N_HEADS_A), tok_major(o_b, N_HEADS_B), qs)
        new_s.append((k_a.reshape(dec_batch, dec_seq, D_HEAD), v_a.reshape(dec_batch, dec_seq, D_HEAD),
                      kiwi[:, :D_IDX].reshape(dec_batch, dec_seq, D_IDX),
                      k_b.reshape(dec_batch, dec_seq, N_HEADS_B, D_HEAD),
                      v_b.reshape(dec_batch, dec_seq, N_HEADS_B, D_HEAD)))

        h, hn, logits = _merge(prompt_rows, sample_rows, wts["w_up_a"], wts["w_up_b"], wts["w_out"], wts["g_ffn"],
                               *wts["router"], MERGE_ROWS)
        hp, hs = _moe(h, hn, logits, gfin, wts["w_e_gate"], wts["w_e_up"], wts["w_e_down"], MERGE_ROWS,
                      batch * seq, last)

    stack = lambda rows_, i: jnp.stack([r[i] for r in rows_])
    return (hp.reshape(batch, seq, D_MODEL), hs.reshape(dec_batch, dec_seq, D_MODEL),
            stack(new_p, 0), stack(new_p, 1), stack(new_p, 2), stack(new_p, 3), stack(new_p, 4),
            stack(new_s, 0), stack(new_s, 1), stack(new_s, 2), stack(new_s, 3), stack(new_s, 4))
```

```python
import functools
import math

import numpy as np
import jax
import jax.numpy as jnp
from jax import lax
from jax.experimental import pallas as pl
from jax.experimental.pallas import tpu as pltpu

D_MODEL = 2048
D_HEAD = 128
N_HEADS_A = 8
N_IDX_HEADS = 16
D_IDX = 64
TOPK_MAX = 256
N_HEADS_B = 8
ROPE_THETA = 10000.0
N_GROUPS = 4
EXPERTS_PER_GROUP = 8
N_EXPERTS = N_GROUPS * EXPERTS_PER_GROUP
TOP_K_EXPERTS = 2
D_EXPERT = 512
PAGE_SIZE = 128
NORM_EPS = 1e-6
A_WIDTH = N_HEADS_A * D_HEAD
B_WIDTH = N_HEADS_B * D_HEAD
IDX_WIDTH = N_IDX_HEADS * D_IDX

LANES = 128
VMEM_LIMIT = 56 * 1024 * 1024
ROPE_ROWS = 512
PLAIN_ROWS = 1024
DSA_Q_ROWS = 256
SB_Q_ROWS = 512
EXPERT_ROWS = 256
EXPERT_K_CHUNK = 256
SB_HEADS_PER_STEP = 4
MERGE_ROWS = 256
ROUTER_LANES = 128

assert TOP_K_EXPERTS == 2

F32 = jnp.float32
BF16 = jnp.bfloat16
NEG_INF = float("-inf")
INT_MIN = -2 ** 31


def _dot(a, b):
    return jnp.dot(a, b, preferred_element_type=F32)


def _dot_nt(a, b):
    return lax.dot_general(a, b, (((1,), (1,)), ((), ())), preferred_element_type=F32)


def _params(semantics):
    return pltpu.CompilerParams(dimension_semantics=semantics, vmem_limit_bytes=VMEM_LIMIT)


def _rope_tables(pos, d):
    half = d // 2
    inv_freq = np.exp(np.arange(half, dtype=np.float32) * np.float32(-2.0 * math.log(ROPE_THETA) / d))
    ang = np.asarray(pos, np.float32)[:, None] * inv_freq[None, :]
    cos, sin = np.cos(ang), np.sin(ang)
    reps = LANES // d
    cos_t = np.tile(np.concatenate([cos, cos], axis=-1), (1, reps))
    sin_t = np.tile(np.concatenate([-sin, sin], axis=-1), (1, reps))
    return jnp.asarray(cos_t, F32), jnp.asarray(sin_t, F32)


def _swap_halves(blk, d):
    if d == LANES:
        return pltpu.roll(blk, LANES // 2, 1)
    lane = lax.broadcasted_iota(jnp.int32, blk.shape, 1)
    first_half = (lane % d) < (d // 2)
    return jnp.where(first_half, pltpu.roll(blk, LANES - d // 2, 1), pltpu.roll(blk, d // 2, 1))


def _rmsnorm_rows(x, g):
    inv = lax.rsqrt(jnp.mean(x * x, axis=-1, keepdims=True) + NORM_EPS)
    return x * inv * g


def _inproj_rope_kernel(x_ref, g_ref, w_ref, cos_ref, sin_ref, cosi_ref, sini_ref,
                        qa_ref, ka_ref, va_ref, qi_ref, kiwi_ref, kit_ref=None):
    xn = _rmsnorm_rows(x_ref[...], g_ref[...]).astype(BF16)
    y = _dot(xn, w_ref[...])
    cos, sin = cos_ref[...], sin_ref[...]
    for h in range(N_HEADS_A):
        blk = y[:, h * D_HEAD:(h + 1) * D_HEAD]
        qa_ref[:, h * D_HEAD:(h + 1) * D_HEAD] = (blk * cos + _swap_halves(blk, D_HEAD) * sin).astype(BF16)
    ka = y[:, A_WIDTH:A_WIDTH + D_HEAD]
    ka_ref[...] = ka * cos + _swap_halves(ka, D_HEAD) * sin
    va_ref[...] = y[:, A_WIDTH + D_HEAD:A_WIDTH + 2 * D_HEAD]
    cosi, sini = cosi_ref[...], sini_ref[...]
    base = A_WIDTH + 2 * D_HEAD
    for j in range(IDX_WIDTH // LANES):
        blk = y[:, base + j * LANES:base + (j + 1) * LANES]
        qi_ref[:, j * LANES:(j + 1) * LANES] = (blk * cosi + _swap_halves(blk, D_IDX) * sini).astype(BF16)
    blk = y[:, base + IDX_WIDTH:base + IDX_WIDTH + LANES]
    lane = lax.broadcasted_iota(jnp.int32, blk.shape, 1)
    is_k = lane < D_IDX
    kiwi = blk * jnp.where(is_k, cosi, 1.0) + _swap_halves(blk, D_IDX) * jnp.where(is_k, sini, 0.0)
    kiwi_ref[...] = kiwi
    if kit_ref is not None:
        kit_ref[0] = kiwi.T[:D_IDX, :]


def _inproj_rope(x, g, w_a, tabs, tm, n_pos_blocks, transposed_k_idx):
    rows = x.shape[0]
    wa_cols = w_a.shape[1]
    cos, sin, cosi, sini = tabs
    row_spec = lambda width: pl.BlockSpec((tm, width), lambda i: (i, 0))
    tab_spec = pl.BlockSpec((tm, LANES), lambda i: (i % n_pos_blocks, 0))
    out_specs = [row_spec(A_WIDTH), row_spec(D_HEAD), row_spec(D_HEAD), row_spec(IDX_WIDTH), row_spec(LANES)]
    out_shape = [jax.ShapeDtypeStruct((rows, A_WIDTH), BF16),
                 jax.ShapeDtypeStruct((rows, D_HEAD), F32),
                 jax.ShapeDtypeStruct((rows, D_HEAD), F32),
                 jax.ShapeDtypeStruct((rows, IDX_WIDTH), BF16),
                 jax.ShapeDtypeStruct((rows, LANES), F32)]
    if transposed_k_idx:
        out_specs.append(pl.BlockSpec((1, D_IDX, tm), lambda i: (i // n_pos_blocks, 0, i % n_pos_blocks)))
        out_shape.append(jax.ShapeDtypeStruct((rows // (n_pos_blocks * tm), D_IDX, n_pos_blocks * tm), F32))
    return pl.pallas_call(
        _inproj_rope_kernel,
        grid=(rows // tm,),
        in_specs=[row_spec(D_MODEL),
                  pl.BlockSpec((1, D_MODEL), lambda i: (0, 0)),
                  pl.BlockSpec((D_MODEL, wa_cols), lambda i: (0, 0)),
                  tab_spec, tab_spec, tab_spec, tab_spec],
        out_specs=out_specs,
        out_shape=out_shape,
        compiler_params=_params(("arbitrary",)),
        name="inproj_rope",
    )(x, g, w_a, cos, sin, cosi, sini)


N_GATE_TILES = 2 * D_MODEL // B_WIDTH


def _inproj_plain_kernel(x_ref, g_ref, w_ref, kb_ref, vb_ref, qs_ref, xn_ref):
    j = pl.program_id(1)

    @pl.when(j == 0)
    def _():
        xn_ref[...] = _rmsnorm_rows(x_ref[...], g_ref[...]).astype(BF16)

    product = lambda: _dot(xn_ref[...], w_ref[...])

    @pl.when(j == 0)
    def _():
        qs_ref[...] = product().astype(BF16)

    @pl.when(j == 1)
    def _():
        kb_ref[...] = product()

    @pl.when(j == 2)
    def _():
        vb_ref[...] = product()

    @pl.when(j >= 3)
    def _():
        qs_ref[...] = (1.0 / (1.0 + jnp.exp(-product()))).astype(BF16)


def _inproj_plain(x, g, w_bg, tm):
    rows = x.shape[0]
    tn = B_WIDTH
    n_col = w_bg.shape[1] // tn
    assert n_col == 3 + N_GATE_TILES
    fixed = pl.BlockSpec((tm, tn), lambda i, j: (i, 0))
    qs_map = lambda i, j: (i, jnp.where(j < 3, N_GATE_TILES, j - 3))
    return pl.pallas_call(
        _inproj_plain_kernel,
        grid=(rows // tm, n_col),
        in_specs=[pl.BlockSpec((tm, D_MODEL), lambda i, j: (i, 0)),
                  pl.BlockSpec((1, D_MODEL), lambda i, j: (0, 0)),
                  pl.BlockSpec((D_MODEL, tn), lambda i, j: (0, j))],
        out_specs=[fixed, fixed, pl.BlockSpec((tm, tn), qs_map)],
        out_shape=[jax.ShapeDtypeStruct((rows, tn), F32), jax.ShapeDtypeStruct((rows, tn), F32),
                   jax.ShapeDtypeStruct((rows, (N_GATE_TILES + 1) * tn), BF16)],
        scratch_shapes=[pltpu.VMEM((tm, D_MODEL), BF16)],
        compiler_params=_params(("arbitrary", "arbitrary")),
        name="inproj_plain",
    )(x, g, w_bg)


def _sortable(x):
    b = lax.bitcast_convert_type(x, jnp.int32)
    return b ^ ((b >> 31) & jnp.int32(0x7FFFFFFF))


def _select_topk(keys_ref, neg_ref, valid_fn, k_top):
    rows, width = keys_ref.shape
    n_chunks = width // LANES
    ones = jnp.ones((LANES, LANES), BF16)
    lane = lax.broadcasted_iota(jnp.int32, (rows, LANES), 1)

    def lane_total(acc):
        return _dot(acc.astype(BF16), ones)

    def count(pred, r0=0, r1=rows):
        acc = jnp.zeros((r1 - r0, LANES), F32)
        for c in range(n_chunks):
            acc = acc + jnp.where(pred(keys_ref[r0:r1, c * LANES:(c + 1) * LANES], c), 1.0, 0.0)
        return lane_total(acc)

    half = rows // 2
    spans = [(0, half), (half, rows)] if half % 8 == 0 and half > 0 else [(0, rows)]

    def value_step(i, answers):
        bit = lax.shift_left(jnp.int32(1), 31 - i)
        new = []
        for (r0, r1), ans in zip(spans, answers):
            cand_u = ans | bit
            cand = cand_u ^ jnp.int32(INT_MIN)
            cnt = count(lambda k, c, cand=cand: k >= cand, r0, r1)
            new.append(jnp.where(cnt >= k_top, cand_u, ans))
        return tuple(new)

    answers = lax.fori_loop(0, 32, value_step, tuple(jnp.zeros((r1 - r0, LANES), jnp.int32) for r0, r1 in spans))
    thr = jnp.concatenate(answers, axis=0) ^ jnp.int32(INT_MIN)
    n_gt = count(lambda k, c: k > thr)
    n_ge = count(lambda k, c: k >= thr)
    need = k_top - n_gt

    n_bits = (width - 1).bit_length()

    def index_step(i, cst):
        cand = cst | lax.shift_left(jnp.int32(1), n_bits - 1 - i)
        cnt = count(lambda k, c: (k == thr) & (lane + c * LANES < cand))
        return jnp.where(cnt < need, cand, cst)

    cut = lax.cond(jnp.max(n_ge) > k_top,
                   lambda: lax.fori_loop(0, n_bits, index_step, jnp.zeros((rows, LANES), jnp.int32)),
                   lambda: jnp.full((rows, LANES), width, jnp.int32))

    for c in range(n_chunks):
        k = keys_ref[:, c * LANES:(c + 1) * LANES]
        tie_keep = jnp.where(lane + c * LANES <= cut, 0.0, NEG_INF)
        keep = jnp.where(k > thr, 0.0, jnp.where(k == thr, tie_keep, NEG_INF))
        neg_ref[:, c * LANES:(c + 1) * LANES] = jnp.where(valid_fn(c), keep, NEG_INF)


def _dsa_prompt_kernel(qi_ref, wq_ref, kk_ref, qa_ref, ka_ref, va_ref, *rest, k_top, q_start):
    _, o_ref, keys_ref, neg_ref = rest
    tq, width = keys_ref.shape

    w = wq_ref[0, :, D_IDX:D_IDX + N_IDX_HEADS] * ((N_IDX_HEADS * D_IDX) ** -0.5)
    k_idx = kk_ref[0, :, :D_IDX].astype(BF16)
    score = jnp.zeros((tq, width), F32)
    for h in range(N_IDX_HEADS):
        rel = jnp.maximum(_dot_nt(qi_ref[0, :, h * D_IDX:(h + 1) * D_IDX], k_idx), 0.0)
        score = score + rel * w[:, h:h + 1]

    t_pos = q_start + lax.broadcasted_iota(jnp.int32, (tq, LANES), 0)
    lane = lax.broadcasted_iota(jnp.int32, (tq, LANES), 1)
    causal = lambda c: lane + c * LANES <= t_pos
    s_pos = lax.broadcasted_iota(jnp.int32, (tq, width), 1)
    q_pos = q_start + lax.broadcasted_iota(jnp.int32, (tq, width), 0)
    keys_ref[...] = _sortable(jnp.where(s_pos <= q_pos, score, NEG_INF))
    _select_topk(keys_ref, neg_ref, causal, k_top)

    k_att = ka_ref[0].astype(BF16)
    v_att = va_ref[0].astype(BF16)
    exp2_scale = (D_HEAD ** -0.5) * math.log2(math.e)
    for h in range(N_HEADS_A):
        s = _dot_nt(qa_ref[0, :, h * D_HEAD:(h + 1) * D_HEAD], k_att) + neg_ref[...]
        m = jnp.max(s, axis=-1, keepdims=True)
        e = jnp.exp2((s - m) * exp2_scale)
        l = jnp.sum(e, axis=-1, keepdims=True)
        o = _dot(e.astype(BF16), v_att) / l
        o_ref[0, :, h * D_HEAD:(h + 1) * D_HEAD] = o.astype(BF16)


def _dsa_prompt(q_i, kiwi, q_a, k_a, v_a, batch, seq, tq):
    k_top = min(TOPK_MAX, seq // 4)
    out = jnp.zeros((batch, seq, A_WIDTH), BF16)
    for qb in range(seq // tq):
        width = (qb + 1) * tq
        q_spec = lambda w_: pl.BlockSpec((1, tq, w_), lambda b: (b, qb, 0))
        kv_spec = lambda w_: pl.BlockSpec((1, width, w_), lambda b: (b, 0, 0))
        out = pl.pallas_call(
            functools.partial(_dsa_prompt_kernel, k_top=k_top, q_start=qb * tq),
            grid=(batch,),
            in_specs=[q_spec(IDX_WIDTH), q_spec(LANES), kv_spec(LANES), q_spec(A_WIDTH), kv_spec(D_HEAD),
                      kv_spec(D_HEAD), pl.BlockSpec(memory_space=pl.ANY)],
            out_specs=q_spec(A_WIDTH),
            out_shape=jax.ShapeDtypeStruct((batch, seq, A_WIDTH), BF16),
            scratch_shapes=[pltpu.VMEM((tq, width), jnp.int32), pltpu.VMEM((tq, width), F32)],
            input_output_aliases={6: 0},
            compiler_params=_params(("arbitrary",)),
            name=f"dsa_prompt_q{qb}",
        )(q_i, kiwi, kiwi, q_a, k_a, v_a, out)
    return out


def _suffix_matrix(n, reps=1):
    j = lax.broadcasted_iota(jnp.int32, (n * reps, n * reps), 0)
    s = lax.broadcasted_iota(jnp.int32, (n * reps, n * reps), 1)
    same = (j % reps) == (s % reps)
    later = jnp.where(same & (j // reps > s // reps), 1.0, 0.0)
    total = jnp.where(same, 1.0, 0.0)
    return jnp.concatenate([later, total], axis=1).astype(BF16)


def _softplus(z):
    return jnp.maximum(z, 0.0) + jnp.log(1.0 + jnp.exp(-jnp.abs(z)))


def _sb_prompt_kernel(q_ref, k_ref, v_ref, u_ref, o_ref, k16, v16, acc_ref, later_ref):
    qb = pl.program_id(2)
    tq = q_ref.shape[0]
    sub = LANES
    n_sub = tq // sub

    @pl.when(qb == 0)
    def _():
        k16[...] = k_ref[...].astype(BF16)
        v16[...] = v_ref[...].astype(BF16)

    u = u_ref[...]
    n_heads = q_ref.shape[1] // D_HEAD

    def block(kb, diagonal):
        start = pl.multiple_of(kb * tq, tq)
        for hh in range(n_heads):
            cols = slice(hh * D_HEAD, (hh + 1) * D_HEAD)
            z = _dot_nt(q_ref[:, cols], k16[pl.ds(start, tq), cols]) * (D_HEAD ** -0.5)
            sp = _softplus(z)
            log_beta = z - sp
            if diagonal:
                before = (lax.broadcasted_iota(jnp.int32, (tq, tq), 1)
                          < lax.broadcasted_iota(jnp.int32, (tq, tq), 0))
                sp = jnp.where(before, sp, 0.0)
            sp16 = sp.astype(BF16)
            later = later_ref[:, cols]
            args = [None] * n_sub
            for j in reversed(range(n_sub)):
                cum = _dot(sp16[:, j * sub:(j + 1) * sub], u)
                args[j] = log_beta[:, j * sub:(j + 1) * sub] - cum[:, :sub] - later
                later = later + cum[:, sub:]
            a = jnp.exp(jnp.concatenate(args, axis=1))
            if diagonal:
                a = jnp.where(before, a, 0.0)
            acc_ref[:, cols] += _dot(a.astype(BF16), v16[pl.ds(start, tq), cols])
            later_ref[:, cols] = later

    acc_ref[...] = jnp.zeros_like(acc_ref)
    later_ref[...] = jnp.zeros_like(later_ref)
    block(qb, True)

    def body(it, carry):
        block(qb - 1 - it, False)
        return carry

    lax.fori_loop(0, qb, body, 0)
    o_ref[...] = acc_ref[...].astype(BF16)


def _sb_prompt(qs, k_b, v_b, batch, seq, tq):
    rows = batch * seq
    nq = seq // tq
    hw = SB_HEADS_PER_STEP * D_HEAD
    q_col = (qs.shape[1] - B_WIDTH) // hw
    q_spec = pl.BlockSpec((tq, hw), lambda b, h, q: (b * nq + q, q_col + h))
    kv_spec = pl.BlockSpec((seq, hw), lambda b, h, q: (b, h))
    return pl.pallas_call(
        _sb_prompt_kernel,
        grid=(batch, N_HEADS_B // SB_HEADS_PER_STEP, nq),
        in_specs=[q_spec, kv_spec, kv_spec, pl.BlockSpec((LANES, 2 * LANES), lambda b, h, q: (0, 0))],
        out_specs=pl.BlockSpec((tq, hw), lambda b, h, q: (b * nq + q, h)),
        out_shape=jax.ShapeDtypeStruct((rows, B_WIDTH), BF16),
        scratch_shapes=[pltpu.VMEM((seq, hw), BF16), pltpu.VMEM((seq, hw), BF16),
                        pltpu.VMEM((tq, hw), F32), pltpu.VMEM((tq, hw), F32)],
        compiler_params=_params(("arbitrary", "arbitrary", "arbitrary")),
        name="sb_prompt",
    )(qs, k_b, v_b, _suffix_matrix(LANES))


def _page_copies(pt_ref, batch_idx, first_page, n_pages, pools, bufs, slot, sem, rows_per_page, n_logical):
    copies = []
    for p in range(n_pages):
        phys = pt_ref[batch_idx * n_logical + first_page + p]
        for pool, buf in zip(pools, bufs):
            copies.append(pltpu.make_async_copy(
                pool.at[phys], buf.at[slot, pl.ds(p * rows_per_page, rows_per_page)], sem.at[slot]))
    return copies


def _paged_double_buffer(pt_ref, pools, bufs, sem, n_pages, rows_per_page):
    b = pl.program_id(0)
    slot = b % 2
    copies = lambda bb, ss: _page_copies(pt_ref, bb, 0, n_pages, pools, bufs, ss, sem, rows_per_page, n_pages)

    @pl.when(b == 0)
    def _():
        for c in copies(0, 0):
            c.start()

    @pl.when(b + 1 < pl.num_programs(0))
    def _():
        for c in copies(b + 1, 1 - slot):
            c.start()

    for c in copies(b, slot):
        c.wait()
    return slot


def _new_key_visible(n_q, n_new, rows):
    t_idx = lax.broadcasted_iota(jnp.int32, (rows, LANES), 0) % n_q
    lane = lax.broadcasted_iota(jnp.int32, (rows, LANES), 1)
    return (lane <= t_idx) & (lane < n_new)


def _idx_scores_sample_kernel(pt_ref, qi_ref, w_ref, kin_ref, idx_pool, s_ref, idx_buf, sem, *, n_pages, n_new):
    b = pl.program_id(0)
    slot = b % 2
    past = n_pages * PAGE_SIZE

    def copies(bb, ss):
        return [pltpu.make_async_copy(idx_pool.at[pt_ref[bb * n_pages + p]],
                                      idx_buf.at[ss, :, pl.ds(p * PAGE_SIZE, PAGE_SIZE)], sem.at[ss])
                for p in range(n_pages)]

    @pl.when(b == 0)
    def _():
        for c in copies(0, 0):
            c.start()

    @pl.when(b + 1 < pl.num_programs(0))
    def _():
        for c in copies(b + 1, 1 - slot):
            c.start()

    for c in copies(b, slot):
        c.wait()

    q = qi_ref[0]
    w = w_ref[0]
    n_q = q.shape[0] // N_IDX_HEADS

    def head_sum(qk):
        rel = jnp.maximum(qk, 0.0) * w
        return functools.reduce(lambda a, c: a + c, [rel[h * n_q:(h + 1) * n_q] for h in range(N_IDX_HEADS)])

    s_ref[0, :, :past] = head_sum(_dot(q, idx_buf[slot].astype(BF16)))
    s_new = head_sum(_dot_nt(q, kin_ref[0].astype(BF16)))
    s_ref[0, :, past:] = jnp.where(_new_key_visible(n_q, n_new, n_q), s_new, NEG_INF)


def _topk_mask_kernel(s_ref, neg_ref, keys_ref, *, k_top, n_q, n_new):
    rows, width = keys_ref.shape
    keys_ref[...] = _sortable(s_ref[...])
    visible = _new_key_visible(n_q, n_new, rows)
    last = width // LANES - 1
    _select_topk(keys_ref, neg_ref, lambda c: visible if c == last else jnp.full((rows, LANES), True), k_top)


def _dsa_sample_attend_kernel(pt_ref, qa_ref, neg_ref, kan_ref, van_ref, k_pool, v_pool, o_ref,
                              k_buf, v_buf, sem, *, n_pages):
    slot = _paged_double_buffer(pt_ref, (k_pool, v_pool), (k_buf, v_buf), sem, n_pages, PAGE_SIZE)
    past = n_pages * PAGE_SIZE
    k_att = k_buf[slot].astype(BF16)
    v_att = v_buf[slot].astype(BF16)
    q = qa_ref[0]
    neg = jnp.concatenate([neg_ref[0]] * N_HEADS_A, axis=0)
    s_past = _dot_nt(q, k_att) + neg[:, :past]
    s_new = _dot_nt(q, kan_ref[0].astype(BF16)) + neg[:, past:]
    m = jnp.maximum(jnp.max(s_past, axis=-1, keepdims=True), jnp.max(s_new, axis=-1, keepdims=True))
    exp2_scale = (D_HEAD ** -0.5) * math.log2(math.e)
    e_past = jnp.exp2((s_past - m) * exp2_scale)
    e_new = jnp.exp2((s_new - m) * exp2_scale)
    l = jnp.sum(e_past, axis=-1, keepdims=True) + jnp.sum(e_new, axis=-1, keepdims=True)
    o = _dot(e_past.astype(BF16), v_att) + _dot(e_new.astype(BF16), van_ref[0].astype(BF16))
    o_ref[0] = o / l


def _dsa_sample(page_table, q_i, w_i, k_i_new, q_a, k_a_new, v_a_new, idx_pool, k_pool, v_pool):
    batch, n_pages = page_table.shape
    n_q = q_a.shape[1] // N_HEADS_A
    past = n_pages * PAGE_SIZE
    width = past + LANES
    k_top = min(TOPK_MAX, (past + n_q) // 4)
    pt = page_table.reshape(-1)
    blk = lambda shape: pl.BlockSpec((1,) + shape, lambda b, pt_: (b,) + (0,) * len(shape))
    hbm = pl.BlockSpec(memory_space=pl.ANY)

    scores = pl.pallas_call(
        functools.partial(_idx_scores_sample_kernel, n_pages=n_pages, n_new=n_q),
        grid_spec=pltpu.PrefetchScalarGridSpec(
            num_scalar_prefetch=1, grid=(batch,),
            in_specs=[blk((N_IDX_HEADS * n_q, D_IDX)), blk((N_IDX_HEADS * n_q, 1)), blk((LANES, D_IDX)), hbm],
            out_specs=blk((n_q, width)),
            scratch_shapes=[pltpu.VMEM((2, D_IDX, past), F32), pltpu.SemaphoreType.DMA((2,))]),
        out_shape=jax.ShapeDtypeStruct((batch, n_q, width), F32),
        compiler_params=_params(("arbitrary",)),
        name="idx_scores_sample",
    )(pt, q_i, w_i, k_i_new, idx_pool)

    rows = batch * n_q
    tr = min(rows, LANES)
    neg = pl.pallas_call(
        functools.partial(_topk_mask_kernel, k_top=k_top, n_q=n_q, n_new=n_q),
        grid=(rows // tr,),
        in_specs=[pl.BlockSpec((tr, width), lambda i: (i, 0))],
        out_specs=pl.BlockSpec((tr, width), lambda i: (i, 0)),
        out_shape=jax.ShapeDtypeStruct((rows, width), F32),
        scratch_shapes=[pltpu.VMEM((tr, width), jnp.int32)],
        compiler_params=_params(("arbitrary",)),
        name="topk_mask_sample",
    )(scores.reshape(rows, width))

    return pl.pallas_call(
        functools.partial(_dsa_sample_attend_kernel, n_pages=n_pages),
        grid_spec=pltpu.PrefetchScalarGridSpec(
            num_scalar_prefetch=1, grid=(batch,),
            in_specs=[blk((N_HEADS_A * n_q, D_HEAD)), blk((n_q, width)), blk((LANES, D_HEAD)), blk((LANES, D_HEAD)),
                      hbm, hbm],
            out_specs=blk((N_HEADS_A * n_q, D_HEAD)),
            scratch_shapes=[pltpu.VMEM((2, past, D_HEAD), F32), pltpu.VMEM((2, past, D_HEAD), F32),
                            pltpu.SemaphoreType.DMA((2,))]),
        out_shape=jax.ShapeDtypeStruct((batch, N_HEADS_A * n_q, D_HEAD), F32),
        compiler_params=_params(("arbitrary",)),
        name="dsa_sample_attend",
    )(pt, q_a, neg.reshape(batch, n_q, width), k_a_new, v_a_new, k_pool, v_pool)


SB_CHUNK_PAGES = 8
SB_SLOTS = 4
SB_BLOCK_KEYS = 32


def _sb_sample_kernel(pt_ref, q_ref, kn_ref, vn_ref, u_ref, k_pool, v_pool, o_ref,
                      k_buf, v_buf, acc_ref, later_ref, sem, *, n_pages, n_new, total_steps):
    b = pl.program_id(0)
    j = pl.program_id(1)
    n_chunks = pl.num_programs(1)
    step = b * n_chunks + j
    n_steps = total_steps
    n_slots = k_buf.shape[0]
    slot = step % n_slots
    heads = N_HEADS_B
    rows_per_page = PAGE_SIZE * heads
    blk_lanes = SB_BLOCK_KEYS * heads

    def copies(st, ss):
        bb = st // n_chunks
        chunk = n_chunks - 1 - st % n_chunks
        return _page_copies(pt_ref, bb, chunk * SB_CHUNK_PAGES, SB_CHUNK_PAGES, (k_pool, v_pool),
                            (k_buf, v_buf), ss, sem, rows_per_page, n_pages)

    @pl.when(step == 0)
    def _():
        for ahead in range(min(n_slots - 1, total_steps)):
            for c in copies(ahead, ahead):
                c.start()

    @pl.when(step + n_slots - 1 < n_steps)
    def _():
        for c in copies(step + n_slots - 1, (step + n_slots - 1) % n_slots):
            c.start()

    q = q_ref[0]
    n_q = q.shape[0] // heads
    u = u_ref[...]
    scale = D_HEAD ** -0.5

    def head_diag(zt):
        lane_head = lax.broadcasted_iota(jnp.int32, (n_q, zt.shape[1]), 1) % heads
        out = jnp.zeros((n_q, zt.shape[1]), F32)
        for h in range(heads):
            out = out + jnp.where(lane_head == h, zt[h * n_q:(h + 1) * n_q, :], 0.0)
        return out

    def head_spread(a):
        lane_head = lax.broadcasted_iota(jnp.int32, a.shape, 1) % heads
        return jnp.concatenate([jnp.where(lane_head == h, a, 0.0) for h in range(heads)], axis=0).astype(BF16)

    def attend(k16, v16, before, later):
        n = k16.shape[0]
        z = head_diag(_dot_nt(q, k16) * scale)
        sp = _softplus(z)
        log_beta = z - sp
        sp16 = jnp.where(before, sp, 0.0).astype(BF16)
        pieces = []
        for blk in reversed(range(n // blk_lanes)):
            sl = slice(blk * blk_lanes, (blk + 1) * blk_lanes)
            cum = _dot(sp16[:, sl], u)
            pieces.append(log_beta[:, sl] - cum[:, :blk_lanes] - later)
            later = later + cum[:, blk_lanes:]
        a = jnp.exp(jnp.concatenate(pieces[::-1], axis=1))
        a = jnp.where(before, a, 0.0)
        return _dot(head_spread(a), v16), later

    @pl.when(j == 0)
    def _():
        t_idx = lax.broadcasted_iota(jnp.int32, (n_q, blk_lanes), 0)
        key = lax.broadcasted_iota(jnp.int32, (n_q, blk_lanes), 1) // heads
        before = (key < t_idx) & (key < n_new)
        out, later = attend(kn_ref[0].astype(BF16), vn_ref[0].astype(BF16), before,
                            jnp.zeros((n_q, blk_lanes), F32))
        acc_ref[...] = out
        later_ref[...] = later

    for c in copies(step, slot):
        c.wait()

    n_rows = SB_CHUNK_PAGES * rows_per_page
    out, later = attend(k_buf[slot].astype(BF16), v_buf[slot].astype(BF16),
                        jnp.full((n_q, n_rows), True), later_ref[...])
    acc_ref[...] += out
    later_ref[...] = later

    @pl.when(j == n_chunks - 1)
    def _():
        o_ref[0] = acc_ref[...]


def _sb_sample(page_table, q, k_new, v_new, k_pool, v_pool):
    batch, n_pages = page_table.shape
    rows_q = q.shape[1]
    n_q = rows_q // N_HEADS_B
    blk_lanes = SB_BLOCK_KEYS * N_HEADS_B
    n_chunks = n_pages // SB_CHUNK_PAGES
    chunk_rows = SB_CHUNK_PAGES * PAGE_SIZE * N_HEADS_B
    blk = lambda shape: pl.BlockSpec((1,) + shape, lambda b, j, pt: (b,) + (0,) * len(shape))
    hbm = pl.BlockSpec(memory_space=pl.ANY)
    grid_spec = pltpu.PrefetchScalarGridSpec(
        num_scalar_prefetch=1,
        grid=(batch, n_chunks),
        in_specs=[blk((rows_q, D_HEAD)), blk((blk_lanes, D_HEAD)), blk((blk_lanes, D_HEAD)),
                  pl.BlockSpec((blk_lanes, 2 * blk_lanes), lambda b, j, pt: (0, 0)), hbm, hbm],
        out_specs=blk((rows_q, D_HEAD)),
        scratch_shapes=[pltpu.VMEM((SB_SLOTS, chunk_rows, D_HEAD), F32), pltpu.VMEM((SB_SLOTS, chunk_rows, D_HEAD), F32),
                        pltpu.VMEM((rows_q, D_HEAD), F32), pltpu.VMEM((n_q, blk_lanes), F32),
                        pltpu.SemaphoreType.DMA((SB_SLOTS,))],
    )
    return pl.pallas_call(
        functools.partial(_sb_sample_kernel, n_pages=n_pages, n_new=n_q, total_steps=batch * n_chunks),
        grid_spec=grid_spec,
        out_shape=jax.ShapeDtypeStruct((batch, rows_q, D_HEAD), F32),
        compiler_params=_params(("arbitrary", "arbitrary")),
        name="sb_sample",
    )(page_table.reshape(-1), q, k_new, v_new, _suffix_matrix(SB_BLOCK_KEYS, N_HEADS_B), k_pool, v_pool)


PACKED_WIDTH = D_MODEL // 2


def _pack_bf16_pair(lo, hi):
    lo_bits = lax.bitcast_convert_type(lo.astype(BF16).astype(F32), jnp.uint32)
    hi_bits = lax.bitcast_convert_type(hi.astype(BF16).astype(F32), jnp.uint32)
    return (lo_bits >> 16) | (hi_bits & jnp.uint32(0xFFFF0000))


def _unpack_bf16_pair(words):
    lo = lax.bitcast_convert_type(words << 16, F32)
    hi = lax.bitcast_convert_type(words & jnp.uint32(0xFFFF0000), F32)
    return lo, hi


def _merge_kernel(*refs, n_prompt_tiles):
    prompt, sample = refs[0:4], refs[4:8]
    wua_ref, wub_ref, wo_ref, gf_ref, wr_hi_ref, wr_lo_ref, br_ref, h_ref, hn_ref, lg_ref = refs[8:]

    def tile(x_ref, oa_ref, ob_ref, sig_ref):
        sig_a = sig_ref[:, :D_MODEL].astype(F32)
        sig_b = sig_ref[:, D_MODEL:].astype(F32)
        merged = sig_a * _dot(oa_ref[...], wua_ref[...]) + sig_b * _dot(ob_ref[...], wub_ref[...])
        h = x_ref[...] + _dot(merged.astype(BF16), wo_ref[...])
        h_ref[...] = h
        hn = _rmsnorm_rows(h, gf_ref[...])
        hn_ref[...] = _pack_bf16_pair(hn[:, :PACKED_WIDTH], hn[:, PACKED_WIDTH:])
        hn_hi = hn.astype(BF16)
        hn_lo = (hn - hn_hi.astype(F32)).astype(BF16)
        lg_ref[...] = (_dot(hn_hi, wr_hi_ref[...]) + _dot(hn_lo, wr_hi_ref[...])
                       + _dot(hn_hi, wr_lo_ref[...]) + br_ref[...])

    i = pl.program_id(0)

    @pl.when(i < n_prompt_tiles)
    def _():
        tile(*prompt)

    @pl.when(i >= n_prompt_tiles)
    def _():
        tile(*sample)


def _merge(prompt, sample, w_up_a, w_up_b, w_out, g_ffn, wr_hi, wr_lo, b_r, tm):
    n_p, n_s = prompt[0].shape[0] // tm, sample[0].shape[0] // tm
    assert prompt[0].shape[0] == n_p * tm and sample[0].shape[0] == n_s * tm
    rows = (n_p + n_s) * tm
    widths = (D_MODEL, A_WIDTH, B_WIDTH, 2 * D_MODEL)
    p_spec = lambda width: pl.BlockSpec((tm, width), lambda i: (jnp.minimum(i, n_p - 1), 0))
    s_spec = lambda width: pl.BlockSpec((tm, width), lambda i: (jnp.maximum(i - n_p, 0), 0))
    row_spec = lambda width: pl.BlockSpec((tm, width), lambda i: (i, 0))
    const = lambda shape: pl.BlockSpec(shape, lambda i: (0, 0), pipeline_mode=pl.Buffered(1))
    return pl.pallas_call(
        functools.partial(_merge_kernel, n_prompt_tiles=n_p),
        grid=(n_p + n_s,),
        in_specs=[p_spec(w_) for w_ in widths] + [s_spec(w_) for w_ in widths] + [
                  const((A_WIDTH, D_MODEL)), const((B_WIDTH, D_MODEL)), const((D_MODEL, D_MODEL)),
                  const((1, D_MODEL)), const((D_MODEL, ROUTER_LANES)), const((D_MODEL, ROUTER_LANES)),
                  const((1, ROUTER_LANES))],
        out_specs=[row_spec(D_MODEL), row_spec(PACKED_WIDTH), row_spec(ROUTER_LANES)],
        out_shape=[jax.ShapeDtypeStruct((rows, D_MODEL), F32),
                   jax.ShapeDtypeStruct((rows, PACKED_WIDTH), jnp.uint32),
                   jax.ShapeDtypeStruct((rows, ROUTER_LANES), F32)],
        compiler_params=_params(("arbitrary",)),
        name="merge",
    )(*prompt, *sample, w_up_a, w_up_b, w_out, g_ffn, wr_hi, wr_lo, b_r)


def _router_kernel(lg_ref, eid_ref, gate_ref, *, chunk):
    for c0 in range(0, lg_ref.shape[0], chunk):
        _route_chunk(lg_ref, eid_ref, gate_ref, c0, chunk)


def _route_chunk(lg_ref, eid_ref, gate_ref, c0, chunk):
    lg = lg_ref[c0:c0 + chunk, :].T
    row = lambda r: lg[r:r + 1, :]
    groups = [row(g) for g in range(N_GROUPS)]
    g_max = functools.reduce(jnp.maximum, groups)
    g_sel = jnp.full(g_max.shape, N_GROUPS - 1, jnp.int32)
    for g in reversed(range(N_GROUPS)):
        g_sel = jnp.where(groups[g] == g_max, g, g_sel)
    p_group = 1.0 / functools.reduce(lambda a, c: a + c, [jnp.exp(v - g_max) for v in groups])
    inside = []
    for e in range(EXPERTS_PER_GROUP):
        v = jnp.zeros_like(g_max)
        for g in range(N_GROUPS):
            v = jnp.where(g_sel == g, row(N_GROUPS + g * EXPERTS_PER_GROUP + e), v)
        inside.append(v)

    def top1(vals, skip):
        best = functools.reduce(jnp.maximum, [jnp.where(skip == e, NEG_INF, v) for e, v in enumerate(vals)])
        idx = jnp.full(best.shape, EXPERTS_PER_GROUP - 1, jnp.int32)
        for e in reversed(range(EXPERTS_PER_GROUP)):
            idx = jnp.where((vals[e] == best) & (skip != e), e, idx)
        return best, idx

    t1, i1 = top1(inside, jnp.full(g_max.shape, -1, jnp.int32))
    t2, i2 = top1(inside, i1)
    r = jnp.exp(t2 - t1)
    for k, ids in enumerate((g_sel * EXPERTS_PER_GROUP + i1, g_sel * EXPERTS_PER_GROUP + i2)):
        for c in range(chunk // LANES):
            row0 = c0 // LANES + c
            eid_ref[k, row0:row0 + 1, :] = ids[:, c * LANES:(c + 1) * LANES]
    gates = jnp.concatenate([p_group * (1.0 / (1.0 + r)), p_group * (r / (1.0 + r)),
                             jnp.zeros((ROUTER_LANES - TOP_K_EXPERTS, chunk), F32)], axis=0)
    gate_ref[c0:c0 + chunk, :] = gates.T


def _router(logits):
    n_tok = logits.shape[0]
    assert n_tok % LANES == 0
    chunk = max(c for c in range(LANES, 2048 + 1, LANES) if n_tok % c == 0)
    return pl.pallas_call(
        functools.partial(_router_kernel, chunk=chunk),
        out_shape=[jax.ShapeDtypeStruct((TOP_K_EXPERTS, n_tok // LANES, LANES), jnp.int32),
                   jax.ShapeDtypeStruct((n_tok, ROUTER_LANES), F32)],
        compiler_params=pltpu.CompilerParams(vmem_limit_bytes=VMEM_LIMIT),
        name="router",
    )(logits)


def _experts_kernel(blk_e_ref, blk_off_ref, blk_cnt_ref, nact_ref, order_ref,
                    hn_hbm, wg_ref, wu_ref, wd_ref, ys_hbm,
                    x_buf, y_buf, wg16, wu16, wd16, in_sem, out_sem, *, n_tok):
    i = pl.program_id(0)
    n_act = nact_ref[0]
    rows = x_buf.shape[1]
    slot = i % 2
    n_pairs = TOP_K_EXPERTS * n_tok

    def gathers(blk, ss):
        off = blk_off_ref[blk]
        copies = []
        for r in range(rows):
            p = order_ref[off + r]
            tok = jnp.where(p >= n_tok, p - n_tok, p)
            copies.append(pltpu.make_async_copy(hn_hbm.at[pl.ds(tok, 1)], x_buf.at[ss, pl.ds(r, 1)], in_sem.at[ss]))
        return copies

    def scatters(blk, ss, real):
        off = blk_off_ref[blk]
        cnt = jnp.where(real, blk_cnt_ref[blk], 0)
        copies = []
        for r in range(rows):
            dst = jnp.where(r < cnt, order_ref[off + r], n_pairs + ss * rows + r)
            copies.append(pltpu.make_async_copy(y_buf.at[ss, pl.ds(r, 1)], ys_hbm.at[pl.ds(dst, 1)], out_sem.at[ss]))
        return copies

    @pl.when(i == 0)
    def _():
        for c in gathers(0, 0):
            c.start()
        y_buf[1] = jnp.zeros(y_buf.shape[1:], y_buf.dtype)
        for ss in range(2):
            spare = pltpu.make_async_copy(y_buf.at[1], ys_hbm.at[pl.ds(n_pairs + ss * rows, rows)], out_sem.at[1])
            spare.start()
            spare.wait()

    @pl.when((i >= 1) & (i < n_act))
    def _():
        for c in scatters(jnp.maximum(i - 2, 0), slot, i >= 2):
            c.wait()

    first_of_expert = (i == 0) | (blk_e_ref[i] != blk_e_ref[jnp.maximum(i - 1, 0)])

    @pl.when((i < n_act) & first_of_expert)
    def _():
        wg16[...] = wg_ref[0].astype(BF16)
        wu16[...] = wu_ref[0].astype(BF16)
        wd16[...] = wd_ref[0].astype(BF16)

    @pl.when(i < n_act)
    def _():
        for c in gathers(i, slot):
            c.wait()
        reads = gathers(jnp.minimum(i + 1, n_act - 1), 1 - slot)
        writes = scatters(jnp.maximum(i - 1, 0), 1 - slot, i >= 1)
        starts = [(c, prio) for pair in zip(reads, writes) for prio, c in enumerate(pair)]
        n_chunks = 2 * D_MODEL // EXPERT_K_CHUNK
        per_chunk = len(starts) // n_chunks
        chunk_no = 0

        def issue_some():
            nonlocal chunk_no
            for c, prio in starts[chunk_no * per_chunk:(chunk_no + 1) * per_chunk]:
                c.start(priority=prio)
            chunk_no += 1

        g = jnp.zeros((rows, D_EXPERT), F32)
        u = jnp.zeros((rows, D_EXPERT), F32)
        for k0 in range(0, D_MODEL, EXPERT_K_CHUNK):
            c0 = k0 % PACKED_WIDTH
            xc = _unpack_bf16_pair(x_buf[slot, :, c0:c0 + EXPERT_K_CHUNK])[k0 // PACKED_WIDTH].astype(BF16)
            g = g + _dot(xc, wg16[k0:k0 + EXPERT_K_CHUNK, :])
            u = u + _dot(xc, wu16[k0:k0 + EXPERT_K_CHUNK, :])
            issue_some()
        act = (g * (1.0 / (1.0 + jnp.exp(-g))) * u).astype(BF16)
        for n0 in range(0, PACKED_WIDTH, EXPERT_K_CHUNK):
            lo = _dot(act, wd16[:, n0:n0 + EXPERT_K_CHUNK])
            issue_some()
            hi = _dot(act, wd16[:, PACKED_WIDTH + n0:PACKED_WIDTH + n0 + EXPERT_K_CHUNK])
            y_buf[slot, :, n0:n0 + EXPERT_K_CHUNK] = _pack_bf16_pair(lo, hi)
            issue_some()

    @pl.when(i == n_act - 1)
    def _():
        for c in scatters(i, slot, True):
            c.start(priority=1)
        for c in gathers(i, 1 - slot):
            c.wait()
        for c in scatters(jnp.maximum(i - 1, 0), 1 - slot, i >= 1):
            c.wait()
        for c in scatters(i, slot, True):
            c.wait()


def _experts(blk_e, blk_off, blk_cnt, n_act, order, hn, w_gate, w_up, w_down):
    n_blocks = blk_e.shape[0]
    n_tok = hn.shape[0]
    rows = EXPERT_ROWS
    w_spec = lambda shape: pl.BlockSpec((1,) + shape, lambda i, be, *_: (be[i], 0, 0))
    row_buf = pltpu.VMEM((2, rows, PACKED_WIDTH), jnp.uint32)
    grid_spec = pltpu.PrefetchScalarGridSpec(
        num_scalar_prefetch=5,
        grid=(n_blocks,),
        in_specs=[pl.BlockSpec(memory_space=pl.ANY),
                  w_spec((D_MODEL, D_EXPERT)), w_spec((D_MODEL, D_EXPERT)), w_spec((D_EXPERT, D_MODEL))],
        out_specs=pl.BlockSpec(memory_space=pl.ANY),
        scratch_shapes=[row_buf, row_buf,
                        pltpu.VMEM((D_MODEL, D_EXPERT), BF16), pltpu.VMEM((D_MODEL, D_EXPERT), BF16),
                        pltpu.VMEM((D_EXPERT, D_MODEL), BF16),
                        pltpu.SemaphoreType.DMA((2,)), pltpu.SemaphoreType.DMA((2,))],
    )
    return pl.pallas_call(
        functools.partial(_experts_kernel, n_tok=n_tok),
        grid_spec=grid_spec,
        out_shape=jax.ShapeDtypeStruct((TOP_K_EXPERTS * n_tok + 2 * rows, PACKED_WIDTH), jnp.uint32),
        compiler_params=_params(("arbitrary",)),
        name="experts",
    )(blk_e, blk_off, blk_cnt, n_act, order, hn, w_gate, w_up, w_down)


def _combine_kernel(h_ref, gate_ref, gfin_ref, *rest, final_norm, n_prompt_tiles):
    yp_ref, ys_ref = rest[-2:]
    y = h_ref[...]
    for k, e_ref in enumerate(rest[:-2]):
        y = y + jnp.concatenate(_unpack_bf16_pair(e_ref[...]), axis=1) * gate_ref[:, k:k + 1]
    y = _rmsnorm_rows(y, gfin_ref[...]) if final_norm else y
    i = pl.program_id(0)

    @pl.when(i < n_prompt_tiles)
    def _():
        yp_ref[...] = y

    @pl.when(i >= n_prompt_tiles)
    def _():
        ys_ref[...] = y


def _combine(h, gates, g_final, ys, tt, n_prompt, final_norm):
    n_tok = h.shape[0]
    n_tiles, n_p = n_tok // tt, n_prompt // tt
    assert n_tok == n_tiles * tt and n_prompt == n_p * tt and n_tiles > n_p
    e_spec = lambda k: pl.BlockSpec((tt, PACKED_WIDTH), lambda i: (k * n_tiles + i, 0))
    return pl.pallas_call(
        functools.partial(_combine_kernel, final_norm=final_norm, n_prompt_tiles=n_p),
        grid=(n_tiles,),
        in_specs=[pl.BlockSpec((tt, D_MODEL), lambda i: (i, 0)),
                  pl.BlockSpec((tt, ROUTER_LANES), lambda i: (i, 0)),
                  pl.BlockSpec((1, D_MODEL), lambda i: (0, 0))] + [e_spec(k) for k in range(TOP_K_EXPERTS)],
        out_specs=[pl.BlockSpec((tt, D_MODEL), lambda i: (jnp.minimum(i, n_p - 1), 0)),
                   pl.BlockSpec((tt, D_MODEL), lambda i: (jnp.maximum(i - n_p, 0), 0))],
        out_shape=[jax.ShapeDtypeStruct((n_prompt, D_MODEL), F32),
                   jax.ShapeDtypeStruct((n_tok - n_prompt, D_MODEL), F32)],
        compiler_params=_params(("arbitrary",)),
        name="combine",
    )(h, gates, g_final, *([ys] * TOP_K_EXPERTS))


def _moe(h, hn, logits, g_final, w_gate, w_up, w_down, tt, n_prompt, final_norm):
    n_tok = h.shape[0]
    eid, gates = _router(logits)
    flat_e = eid.reshape(-1)
    nk = flat_e.shape[0]
    rows = EXPERT_ROWS
    _, order = lax.sort_key_val(flat_e, jnp.arange(nk, dtype=jnp.int32))
    experts = jnp.arange(N_EXPERTS, dtype=jnp.int32)
    counts = jnp.sum((flat_e[:, None] == experts[None, :]).astype(jnp.int32), axis=0)
    inclusive = lambda v: jnp.sum(jnp.where(experts[:, None] <= experts[None, :], v[:, None], 0), axis=0)
    start = inclusive(counts) - counts
    blocks_per_e = (counts + rows - 1) // rows
    blk_end = inclusive(blocks_per_e)
    blk_begin = blk_end - blocks_per_e
    n_blocks = -(-(nk + N_EXPERTS * (rows - 1)) // rows)
    blk = jnp.arange(n_blocks, dtype=jnp.int32)
    mine = (blk[:, None] >= blk_begin[None, :]) & (blk[:, None] < blk_end[None, :])
    pick = lambda v: jnp.sum(jnp.where(mine, v[None, :], 0), axis=1)
    first = (blk - pick(blk_begin)) * rows
    blk_cnt = jnp.clip(pick(counts) - first, 0, rows)
    blk_off = jnp.clip(pick(start) + first, 0, nk)
    order = jnp.concatenate([order, jnp.zeros((rows,), jnp.int32)])
    ys = _experts(pick(experts), blk_off, blk_cnt, blk_end[-1:], order, hn, w_gate, w_up, w_down)
    return _combine(h, gates, g_final, ys, tt, n_prompt, final_norm)


def _split_router(w_rg, b_rg, w_re, b_re):
    w = jnp.concatenate([w_rg, w_re], axis=1)
    w = jnp.pad(w, ((0, 0), (0, ROUTER_LANES - w.shape[1])))
    b = jnp.pad(jnp.concatenate([b_rg, b_re]), (0, ROUTER_LANES - N_GROUPS - N_EXPERTS)).reshape(1, ROUTER_LANES)
    w_hi = w.astype(BF16)
    w_lo = (w - w_hi.astype(F32)).astype(BF16)
    return w_hi, w_lo, b


def _layer_weights(l, g_mix, w_in, w_up_a, w_up_b, w_out, g_ffn, w_rg, b_rg, w_re, b_re, w_e_gate, w_e_up, w_e_down):
    n_rope = A_WIDTH + 2 * D_HEAD + IDX_WIDTH + D_IDX + N_IDX_HEADS
    w = w_in[l]
    w_a = jnp.pad(w[:, :n_rope], ((0, 0), (0, -n_rope % LANES))).astype(BF16)
    w_bg = w[:, n_rope:].astype(BF16)
    return dict(
        g_mix=g_mix[l].reshape(1, D_MODEL), w_a=w_a, w_bg=w_bg,
        w_up_a=w_up_a[l].astype(BF16), w_up_b=w_up_b[l].astype(BF16), w_out=w_out[l].astype(BF16),
        g_ffn=g_ffn[l].reshape(1, D_MODEL), router=_split_router(w_rg[l], b_rg[l], w_re[l], b_re[l]),
        w_e_gate=w_e_gate[l], w_e_up=w_e_up[l], w_e_down=w_e_down[l])


def _project(x2d, wts, pos, tm_rope, tm_plain, transposed_k_idx=False):
    tabs = _rope_tables(pos, D_HEAD) + _rope_tables(pos, D_IDX)
    roped = _inproj_rope(x2d, wts["g_mix"], wts["w_a"], tabs, tm_rope, pos.shape[0] // tm_rope, transposed_k_idx)
    return (tuple(roped),) + tuple(_inproj_plain(x2d, wts["g_mix"], wts["w_bg"], tm_plain))


def _pad_rows(a, n):
    return jnp.pad(a, ((0, 0), (0, n - a.shape[1]), (0, 0)))


def kernel(x_prompt, x_sample, cache_a_k, cache_a_v, cache_idx_k, cache_b_k, cache_b_v, page_table,
           g_mix, w_in, w_up_a, w_up_b, w_out, g_ffn, w_rg, b_rg, w_re, b_re,
           w_e_gate, w_e_up, w_e_down, g_final):
    depth = w_in.shape[0]
    batch, seq, _ = x_prompt.shape
    dec_batch, dec_seq, _ = x_sample.shape
    n_pages = page_table.shape[1]
    past = n_pages * PAGE_SIZE
    n_phys = cache_a_k.shape[1]
    gfin = g_final.reshape(1, D_MODEL)
    hp = x_prompt.reshape(batch * seq, D_MODEL)
    hs = x_sample.reshape(dec_batch * dec_seq, D_MODEL)
    rows_s = dec_batch * dec_seq
    new_p, new_s = [], []
    for l in range(depth):
        wts = _layer_weights(l, g_mix, w_in, w_up_a, w_up_b, w_out, g_ffn, w_rg, b_rg, w_re, b_re,
                             w_e_gate, w_e_up, w_e_down)
        last = l == depth - 1

        (q_a, k_a, v_a, q_i, kiwi, k_idx_t), k_b, v_b, qs = _project(hp, wts, np.arange(seq), ROPE_ROWS, PLAIN_ROWS,
                                                                     True)
        per_seq = lambda a: a.reshape(batch, seq, a.shape[-1])
        o_a = _dsa_prompt(per_seq(q_i), per_seq(kiwi), per_seq(q_a), per_seq(k_a), per_seq(v_a),
                          batch, seq, DSA_Q_ROWS).reshape(batch * seq, A_WIDTH)
        o_b = _sb_prompt(qs, k_b, v_b, batch, seq, SB_Q_ROWS)
        prompt_rows = (hp, o_a, o_b, qs)
        new_p.append((k_a.reshape(batch, seq, D_HEAD), v_a.reshape(batch, seq, D_HEAD),
                      jnp.swapaxes(k_idx_t, 1, 2),
                      k_b.reshape(batch, seq, N_HEADS_B, D_HEAD), v_b.reshape(batch, seq, N_HEADS_B, D_HEAD)))

        pos_s = np.tile(past + np.arange(dec_seq), dec_batch)
        (q_a, k_a, v_a, q_i, kiwi), k_b, v_b, qs = _project(hs, wts, pos_s, rows_s, rows_s)
        q_b = qs[:, 2 * D_MODEL:]
        per_b = lambda a: a.reshape(dec_batch, dec_seq, -1)
        heads_first = lambda a, nh, d: (per_b(a).reshape(dec_batch, dec_seq, nh, d).transpose(0, 2, 1, 3)
                                        .reshape(dec_batch, nh * dec_seq, d))
        w_idx = per_b(kiwi)[:, :, D_IDX:D_IDX + N_IDX_HEADS] * ((N_IDX_HEADS * D_IDX) ** -0.5)
        wi_s = w_idx.transpose(0, 2, 1).reshape(dec_batch, N_IDX_HEADS * dec_seq, 1)
        kin_s = _pad_rows(per_b(kiwi)[:, :, :D_IDX], LANES)
        o_a = _dsa_sample(page_table, heads_first(q_i, N_IDX_HEADS, D_IDX), wi_s, kin_s,
                          heads_first(q_a, N_HEADS_A, D_HEAD), _pad_rows(per_b(k_a), LANES),
                          _pad_rows(per_b(v_a), LANES), jnp.swapaxes(cache_idx_k[l], 1, 2),
                          cache_a_k[l], cache_a_v[l])
        blk_lanes = SB_BLOCK_KEYS * N_HEADS_B
        kb_new = _pad_rows(k_b.reshape(dec_batch, dec_seq * N_HEADS_B, D_HEAD), blk_lanes)
        vb_new = _pad_rows(v_b.reshape(dec_batch, dec_seq * N_HEADS_B, D_HEAD), blk_lanes)
        pool_rows = lambda c: c.reshape(n_phys, PAGE_SIZE * N_HEADS_B, D_HEAD)
        o_b = _sb_sample(page_table, heads_first(q_b, N_HEADS_B, D_HEAD), kb_new, vb_new,
                         pool_rows(cache_b_k[l]), pool_rows(cache_b_v[l]))
        tok_major = lambda o, nh: (o.reshape(dec_batch, nh, dec_seq, D_HEAD).transpose(0, 2, 1, 3)
                                   .reshape(rows_s, nh * D_HEAD).astype(BF16))
        sample_rows = (hs, tok_major(o_a, N_HEADS_A), tok_major(o_b, N_HEADS_B), qs)
        new_s.append((k_a.reshape(dec_batch, dec_seq, D_HEAD), v_a.reshape(dec_batch, dec_seq, D_HEAD),
                      kiwi[:, :D_IDX].reshape(dec_batch, dec_seq, D_IDX),
                      k_b.reshape(dec_batch, dec_seq, N_HEADS_B, D_HEAD),
                      v_b.reshape(dec_batch, dec_seq, N_HEADS_B, D_HEAD)))

        h, hn, logits = _merge(prompt_rows, sample_rows, wts["w_up_a"], wts["w_up_b"], wts["w_out"], wts["g_ffn"],
                               *wts["router"], MERGE_ROWS)
        hp, hs = _moe(h, hn, logits, gfin, wts["w_e_gate"], wts["w_e_up"], wts["w_e_down"], MERGE_ROWS,
                      batch * seq, last)

    stack = lambda rows_, i: jnp.stack([r[i] for r in rows_])
    return (hp.reshape(batch, seq, D_MODEL), hs.reshape(dec_batch, dec_seq, D_MODEL),
            stack(new_p, 0), stack(new_p, 1), stack(new_p, 2), stack(new_p, 3), stack(new_p, 4),
            stack(new_s, 0), stack(new_s, 1), stack(new_s, 2), stack(new_s, 3), stack(new_s, 4))
```
